```python
import math
import jax
import jax.numpy as jnp
from jax import lax
import numpy as np

D_MODEL = 2048
BATCH = 4
SEQ = 8192
DEPTH = 1
DEC_BATCH = 1
DEC_SEQ = 8192
PAST_LEN = 128

MIX_WIDTH = D_MODEL
ATTN_WIDTH = MIX_WIDTH // 2
RWKV_WIDTH = MIX_WIDTH - ATTN_WIDTH
ATTN_HEADS = 8
ATTN_VDIM = ATTN_WIDTH // ATTN_HEADS
ATTN_QKDIM = ATTN_VDIM // 2
Q_BLOCK = 128
RWKV_HEAD = 64
RWKV_HEADS = RWKV_WIDTH // RWKV_HEAD
N_DIR = 2
DECAY_LORA = 64
ICLR_LORA = 64
GATE_LORA = 160
D_FF = -((-8 * D_MODEL) // (3 * 256)) * 256
NORM_EPS = 1e-6
LNX_EPS = 64e-5
ATTN_QK_COLS = ATTN_HEADS * 2 * ATTN_QKDIM
RWKV_SIZES = (RWKV_WIDTH, RWKV_WIDTH, RWKV_WIDTH, N_DIR * DECAY_LORA, N_DIR * ICLR_LORA, GATE_LORA)
RWKV_COLS = sum(RWKV_SIZES)
IN_COLS = 2 * ATTN_QK_COLS + ATTN_WIDTH + RWKV_COLS

kernel_name = 'bidir_diffattn_rwkv7_hybrid_encoder'


def lambda_init_fn(layer_idx):
    return 0.8 - 0.6 * math.exp(-0.3 * layer_idx)


def rms_norm(x, g):
    xf = x.astype(jnp.float32)
    y = xf * lax.rsqrt(jnp.mean(xf * xf, axis=-1, keepdims=True) + NORM_EPS)
    return (y * g.astype(jnp.float32)).astype(x.dtype)


def diff_attention(q, k, v, lam, subln, lam_init):
    B, S = q.shape[0], q.shape[1]
    nblk = S // Q_BLOCK
    scale = ATTN_QKDIM ** -0.5
    slopes = 2.0 ** (-8.0 * (jnp.arange(ATTN_HEADS, dtype=jnp.float32) + 1.0) / ATTN_HEADS)
    kpos = jnp.arange(S, dtype=jnp.float32)
    qb = q.reshape(B, nblk, Q_BLOCK, ATTN_HEADS, 2, ATTN_QKDIM).transpose(1, 0, 2, 3, 4, 5)

    def one_block(args):
        qi, bi = args
        s = jnp.einsum('bqhcd,bshcd->bhcqs', qi, k).astype(jnp.float32) * scale
        qpos = (bi * Q_BLOCK + jnp.arange(Q_BLOCK)).astype(jnp.float32)
        dist = jnp.abs(qpos[:, None] - kpos[None, :])
        s = s - slopes[None, :, None, None, None] * dist[None, None, None]
        p = jax.nn.softmax(s, axis=-1)
        a = p[:, :, 0] - lam * p[:, :, 1]
        return jnp.einsum('bhqs,bshd->bqhd', a.astype(v.dtype), v)

    o = lax.map(one_block, (qb, jnp.arange(nblk)))
    o = o.transpose(1, 0, 2, 3, 4).reshape(B, S, ATTN_HEADS, ATTN_VDIM)
    o = rms_norm(o, subln) * (1.0 - lam_init)
    return o.reshape(B, S, ATTN_HEADS * ATTN_VDIM)


def wkv_scan(r, w, k, v, a, b, reverse):
    B, S, H, N = r.shape
    xs = tuple(t.transpose(1, 0, 2, 3) for t in (r, w, k, v, a, b))

    def step(state, inp):
        rt, wt, kt, vt, at, bt = inp
        sa = jnp.einsum('bhvk,bhk->bhv', state, at)
        state = state * wt[:, :, None, :] + sa[..., None] * bt[:, :, None, :] + vt[..., None] * kt[:, :, None, :]
        yt = jnp.einsum('bhvk,bhk->bhv', state, rt)
        return state, yt

    s0 = jnp.zeros((B, H, N, N), jnp.float32)
    _, y = lax.scan(step, s0, xs, reverse=reverse)
    return y.transpose(1, 0, 2, 3)


def rwkv7_bidir(z, tshift_prev, tshift_next, w0, w2, a0, a2, g2, k_k, k_a, r_k, lnx_w, lnx_b):
    B, S, _ = z.shape
    f32 = jnp.float32
    z = z.astype(f32)
    z_prev = jnp.pad(z[:, :-1], ((0, 0), (1, 0), (0, 0)))
    z_next = jnp.pad(z[:, 1:], ((0, 0), (0, 1), (0, 0)))
    z = z + tshift_prev.astype(f32) * (z_prev - z) + tshift_next.astype(f32) * (z_next - z)
    r, k, v, wd, ad, gd = jnp.split(z, np.cumsum(RWKV_SIZES)[:-1].tolist(), axis=-1)
    heads = lambda t: t.reshape(B, S, RWKV_HEADS, RWKV_HEAD)
    wd = jnp.tanh(wd.reshape(B, S, N_DIR, DECAY_LORA))
    ad = ad.reshape(B, S, N_DIR, ICLR_LORA)
    w_log = w0.astype(f32) + jnp.einsum('bsdr,drc->bsdc', wd, w2.astype(f32))
    decay = jnp.exp(-jnp.exp(-jax.nn.softplus(-w_log) - 0.5))
    iclr = jax.nn.sigmoid(a0.astype(f32) + jnp.einsum('bsdr,drc->bsdc', ad, a2.astype(f32)))
    gate = jax.nn.sigmoid(gd) @ g2.astype(f32)
    kk = heads(k * k_k.astype(f32))
    kk = kk * lax.rsqrt(jnp.maximum(jnp.sum(kk * kk, axis=-1, keepdims=True), 1e-24))
    r_h, v_h = heads(r), heads(v)
    r_k32 = r_k.astype(f32)
    k_a32 = k_a.astype(f32)
    ys, bonus = [], []
    for d in range(N_DIR):
        a_d = iclr[:, :, d]
        k_d = heads(k * (1.0 + (a_d - 1.0) * k_a32))
        ys.append(wkv_scan(r_h, heads(decay[:, :, d]), k_d, v_h, -kk, kk * heads(a_d), reverse=(d == 1)))
        bonus.append(jnp.sum(r_h * k_d * r_k32, axis=-1, keepdims=True) * v_h)
    y = ys[0] + ys[1]
    mu = jnp.mean(y, axis=-1, keepdims=True)
    var = jnp.mean(jnp.square(y - mu), axis=-1, keepdims=True)
    yn = ((y - mu) * lax.rsqrt(var + LNX_EPS)).reshape(B, S, RWKV_WIDTH)
    yn = yn * lnx_w.astype(f32) + lnx_b.astype(f32)
    return (yn + (bonus[0] + bonus[1]).reshape(B, S, RWKV_WIDTH)) * gate


def encoder_layer(x, layer_idx, norm_mix_pre, norm_mix_post, w_in, w_out, lambda_q1, lambda_k1,
                  lambda_q2, lambda_k2, attn_subln, tshift_prev, tshift_next, w0, w2, a0, a2, g2,
                  k_k, k_a, r_k, lnx_w, lnx_b, norm_ffn_pre, norm_ffn_post, w_gate, w_up, w_down):
    B, S, _ = x.shape
    h = rms_norm(x, norm_mix_pre)
    p = h @ w_in
    q, k, v, zr = jnp.split(p, [ATTN_QK_COLS, 2 * ATTN_QK_COLS, 2 * ATTN_QK_COLS + ATTN_WIDTH], axis=-1)
    lam_init = lambda_init_fn(layer_idx)
    f32 = jnp.float32
    lam = (jnp.exp(jnp.sum(lambda_q1.astype(f32) * lambda_k1.astype(f32)))
           - jnp.exp(jnp.sum(lambda_q2.astype(f32) * lambda_k2.astype(f32))) + lam_init)
    attn = diff_attention(q.reshape(B, S, ATTN_HEADS, 2, ATTN_QKDIM),
                          k.reshape(B, S, ATTN_HEADS, 2, ATTN_QKDIM),
                          v.reshape(B, S, ATTN_HEADS, ATTN_VDIM), lam, attn_subln, lam_init)
    rw = rwkv7_bidir(zr, tshift_prev, tshift_next, w0, w2, a0, a2, g2, k_k, k_a, r_k,
                     lnx_w, lnx_b).astype(x.dtype)
    m = jnp.concatenate([attn, rw], axis=-1) @ w_out
    x = x + rms_norm(m, norm_mix_post)
    h = rms_norm(x, norm_ffn_pre)
    f = (jax.nn.silu(h @ w_gate) * (h @ w_up)) @ w_down
    return x + rms_norm(f, norm_ffn_post)


def setup_inputs(seed: int = 0) -> dict:
    key = jax.random.key(seed)
    ks = jax.random.split(key, 28)
    f32 = jnp.float32
    L = DEPTH
    nrm = lambda kk, shape, s: s * jax.random.normal(kk, shape, f32)
    gain = lambda kk, shape: 1.0 + 0.1 * jax.random.normal(kk, shape, f32)
    return {
        'x_prompt': nrm(ks[0], (BATCH, SEQ, D_MODEL), 1.0),
        'x_sample': nrm(ks[1], (DEC_BATCH, DEC_SEQ, D_MODEL), 1.0),
        'norm_mix_pre': gain(ks[2], (L, D_MODEL)),
        'norm_mix_post': gain(ks[3], (L, D_MODEL)),
        'w_in': nrm(ks[4], (L, D_MODEL, IN_COLS), D_MODEL ** -0.5),
        'w_out': nrm(ks[5], (L, MIX_WIDTH, D_MODEL), MIX_WIDTH ** -0.5),
        'lambda_q1': nrm(ks[6], (L, ATTN_QKDIM), 0.1),
        'lambda_k1': nrm(ks[7], (L, ATTN_QKDIM), 0.1),
        'lambda_q2': nrm(ks[8], (L, ATTN_QKDIM), 0.1),
        'lambda_k2': nrm(ks[9], (L, ATTN_QKDIM), 0.1),
        'attn_subln': gain(ks[10], (L, ATTN_VDIM)),
        'tshift_prev': jax.random.uniform(ks[11], (L, RWKV_COLS), f32, 0.0, 0.5),
        'tshift_next': jax.random.uniform(ks[12], (L, RWKV_COLS), f32, 0.0, 0.5),
        'w0': jax.random.uniform(ks[13], (L, N_DIR, RWKV_WIDTH), f32, -6.0, -1.0),
        'w2': nrm(ks[14], (L, N_DIR, DECAY_LORA, RWKV_WIDTH), 0.5 * DECAY_LORA ** -0.5),
        'a0': nrm(ks[15], (L, N_DIR, RWKV_WIDTH), 0.5),
        'a2': nrm(ks[16], (L, N_DIR, ICLR_LORA, RWKV_WIDTH), 0.5 * ICLR_LORA ** -0.5),
        'g2': nrm(ks[17], (L, GATE_LORA, RWKV_WIDTH), GATE_LORA ** -0.5),
        'k_k': 0.85 + 0.1 * jax.random.normal(ks[18], (L, RWKV_WIDTH), f32),
        'k_a': gain(ks[19], (L, RWKV_WIDTH)),
        'r_k': nrm(ks[20], (L, RWKV_HEADS, RWKV_HEAD), 0.1),
        'lnx_w': gain(ks[21], (L, RWKV_WIDTH)),
        'lnx_b': nrm(ks[22], (L, RWKV_WIDTH), 0.02),
        'norm_ffn_pre': gain(ks[23], (L, D_MODEL)),
        'norm_ffn_post': gain(ks[24], (L, D_MODEL)),
        'w_gate': nrm(ks[25], (L, D_MODEL, D_FF), D_MODEL ** -0.5),
        'w_up': nrm(ks[26], (L, D_MODEL, D_FF), D_MODEL ** -0.5),
        'w_down': nrm(ks[27], (L, D_FF, D_MODEL), D_FF ** -0.5),
    }


def reference(x_prompt, x_sample, norm_mix_pre, norm_mix_post, w_in, w_out, lambda_q1, lambda_k1,
              lambda_q2, lambda_k2, attn_subln, tshift_prev, tshift_next, w0, w2, a0, a2, g2,
              k_k, k_a, r_k, lnx_w, lnx_b, norm_ffn_pre, norm_ffn_post, w_gate, w_up, w_down):
    def trunk(x):
        for li in range(DEPTH):
            x = encoder_layer(x, li, norm_mix_pre[li], norm_mix_post[li], w_in[li], w_out[li],
                              lambda_q1[li], lambda_k1[li], lambda_q2[li], lambda_k2[li],
                              attn_subln[li], tshift_prev[li], tshift_next[li], w0[li], w2[li],
                              a0[li], a2[li], g2[li], k_k[li], k_a[li], r_k[li], lnx_w[li],
                              lnx_b[li], norm_ffn_pre[li], norm_ffn_post[li], w_gate[li],
                              w_up[li], w_down[li])
        return x

    y_prompt = trunk(x_prompt)
    y_sample = trunk(x_sample)
    return (y_prompt, y_sample)
```

```python
import functools
import math

import jax
import jax.numpy as jnp
from jax import lax
from jax.experimental import pallas as pl
from jax.experimental.pallas import tpu as pltpu

F32 = jnp.float32
BF16 = jnp.bfloat16

D_MODEL = 2048
ATTN_HEADS = 8
ATTN_VDIM = 128
ATTN_QKDIM = 64
ATTN_WIDTH = ATTN_HEADS * ATTN_VDIM
RWKV_HEAD = 64
RWKV_HEADS = 16
RWKV_WIDTH = RWKV_HEAD * RWKV_HEADS
N_DIR = 2
LORA = 64
GATE_LORA = 160
GATE_LORA_PAD = 256
QKV_COLS = 3 * ATTN_WIDTH
RWKV_COLS = 3 * RWKV_WIDTH + 2 * N_DIR * LORA + GATE_LORA
RWKV_COLS_PAD = 3 * RWKV_WIDTH + 2 * N_DIR * LORA + GATE_LORA_PAD
D_FF = 5632
NORM_EPS = 1e-6
LNX_EPS = 64e-5
LAMBDA_INIT = 0.8 - 0.6 * math.exp(-0.3 * 0)

LANES = 128
CHUNK = 64
PAIR = LANES // RWKV_HEAD


def _dot(a, b):
    return jnp.dot(a, b, preferred_element_type=F32)


def _dot_nt(a, b):
    return lax.dot_general(a, b, (((1,), (1,)), ((), ())), preferred_element_type=F32)


def _dot_tn(a, b):
    return lax.dot_general(a, b, (((0,), (0,)), ((), ())), preferred_element_type=F32)


def _split(x):
    hi = x.astype(BF16)
    lo = (x - hi.astype(F32)).astype(BF16)
    return hi, lo


def _dot3(a, b):
    ah, al = _split(a)
    bh, bl = _split(b)
    return _dot(ah, bh) + _dot(al, bh) + _dot(ah, bl)


def _sigmoid(x):
    return 1.0 / (1.0 + jnp.exp(-x))


def _head_ones(width):
    r = lax.broadcasted_iota(jnp.int32, (width, width), 0) // RWKV_HEAD
    c = lax.broadcasted_iota(jnp.int32, (width, width), 1) // RWKV_HEAD
    return jnp.where(r == c, 1.0, 0.0).astype(BF16)


def _head_sum(x, ones):
    outs = []
    for g in range(x.shape[1] // LANES):
        hi, lo = _split(x[:, g * LANES:(g + 1) * LANES])
        outs.append(_dot(hi, ones) + _dot(lo, ones))
    return jnp.concatenate(outs, axis=1)


def _inproj_kernel(x_ref, g_ref, w_ref, qkv_ref, z_ref, h_ref, *, n_qkv_tiles):
    n = pl.program_id(1)

    @pl.when(n == 0)
    def _():
        x = x_ref[...]
        ms = jnp.mean(x * x, axis=-1, keepdims=True)
        h_ref[...] = (x * lax.rsqrt(ms + NORM_EPS) * g_ref[...]).astype(BF16)

    acc = _dot(h_ref[...], w_ref[...])

    @pl.when(n < n_qkv_tiles)
    def _():
        qkv_ref[...] = acc.astype(BF16)

    @pl.when(n >= n_qkv_tiles)
    def _():
        z_ref[...] = acc


def _inproj(x2d, gain, w_in_bf16, *, tm, tn):
    T = x2d.shape[0]
    n_cols = w_in_bf16.shape[1]
    nq = QKV_COLS // tn
    grid = (T // tm, n_cols // tn)
    return pl.pallas_call(
        functools.partial(_inproj_kernel, n_qkv_tiles=nq),
        out_shape=(jax.ShapeDtypeStruct((T, QKV_COLS), BF16),
                   jax.ShapeDtypeStruct((T, RWKV_COLS_PAD), F32)),
        grid=grid,
        in_specs=[pl.BlockSpec((tm, D_MODEL), lambda m, n: (m, 0)),
                  pl.BlockSpec((1, D_MODEL), lambda m, n: (0, 0)),
                  pl.BlockSpec((D_MODEL, tn), lambda m, n: (0, n))],
        out_specs=(pl.BlockSpec((tm, tn), lambda m, n: (m, jnp.minimum(n, nq - 1))),
                   pl.BlockSpec((tm, tn), lambda m, n: (m, jnp.maximum(n - nq, 0)))),
        scratch_shapes=[pltpu.VMEM((tm, D_MODEL), BF16)],
        compiler_params=pltpu.CompilerParams(dimension_semantics=("parallel", "arbitrary")),
        name="inproj",
    )(x2d, gain, w_in_bf16)


def _attn_kernel(slopes_ref, lam_ref, sub_ref, q_ref, k_ref, v_ref, o_ref,
                 m1_ref, l1_ref, a1_ref, m2_ref, l2_ref, a2_ref, *, tq, tk, seq):
    h = pl.program_id(1)
    qi = pl.program_id(2)
    slope = slopes_ref[h]

    q = q_ref[0] * (ATTN_QKDIM ** -0.5)
    lane = lax.broadcasted_iota(jnp.int32, q.shape, 1)
    q1 = jnp.where(lane < ATTN_QKDIM, q, 0).astype(BF16)
    q2 = jnp.where(lane >= ATTN_QKDIM, q, 0).astype(BF16)

    rel = (lax.broadcasted_iota(jnp.int32, (tq, tk), 0)
           - lax.broadcasted_iota(jnp.int32, (tq, tk), 1)).astype(F32)

    m1_ref[...] = jnp.full((tq, 1), -jnp.inf, F32)
    m2_ref[...] = jnp.full((tq, 1), -jnp.inf, F32)
    l1_ref[...] = jnp.zeros((tq, 1), F32)
    l2_ref[...] = jnp.zeros((tq, 1), F32)
    a1_ref[...] = jnp.zeros((tq, ATTN_VDIM), F32)
    a2_ref[...] = jnp.zeros((tq, ATTN_VDIM), F32)

    def one_map(qm, kt, vt, bias, m_ref, l_ref, a_ref):
        s = _dot_nt(qm, kt) + bias
        m_old = m_ref[...]
        m_new = jnp.maximum(m_old, jnp.max(s, axis=-1, keepdims=True))
        alpha = jnp.exp(m_old - m_new)
        p = jnp.exp(s - m_new)
        l_ref[...] = alpha * l_ref[...] + jnp.sum(p, axis=-1, keepdims=True)
        a_ref[...] = alpha * a_ref[...] + _dot(p.astype(BF16), vt)
        m_ref[...] = m_new

    def body(j, carry):
        off = pl.multiple_of(j * tk, tk)
        kt = k_ref[0, pl.ds(off, tk), :]
        vt = v_ref[0, pl.ds(off, tk), :]
        base = (qi * tq - j * tk).astype(F32)
        bias = -slope * jnp.abs(rel + base)
        one_map(q1, kt, vt, bias, m1_ref, l1_ref, a1_ref)
        one_map(q2, kt, vt, bias, m2_ref, l2_ref, a2_ref)
        return carry

    lax.fori_loop(0, seq // tk, body, 0)

    lp = lam_ref[...]
    lam = (jnp.exp(jnp.sum(lp[0:1] * lp[1:2], axis=-1, keepdims=True))
           - jnp.exp(jnp.sum(lp[2:3] * lp[3:4], axis=-1, keepdims=True)) + LAMBDA_INIT)
    o = a1_ref[...] / l1_ref[...] - lam * (a2_ref[...] / l2_ref[...])
    ms = jnp.mean(o * o, axis=-1, keepdims=True)
    o = o * lax.rsqrt(ms + NORM_EPS) * sub_ref[...] * (1.0 - LAMBDA_INIT)
    o_ref[0] = o.astype(o_ref.dtype)


def _diff_attention(qkv, lam_params, subln, *, tq, tk):
    B, S, _ = qkv.shape
    slopes = jnp.asarray([2.0 ** (-8.0 * (h + 1.0) / ATTN_HEADS) for h in range(ATTN_HEADS)], F32)
    grid_spec = pltpu.PrefetchScalarGridSpec(
        num_scalar_prefetch=1,
        grid=(B, ATTN_HEADS, S // tq),
        in_specs=[pl.BlockSpec((4, ATTN_QKDIM), lambda b, h, i, s: (0, 0)),
                  pl.BlockSpec((1, ATTN_VDIM), lambda b, h, i, s: (0, 0)),
                  pl.BlockSpec((1, tq, LANES), lambda b, h, i, s: (b, i, h)),
                  pl.BlockSpec((1, S, LANES), lambda b, h, i, s: (b, 0, ATTN_HEADS + h)),
                  pl.BlockSpec((1, S, LANES), lambda b, h, i, s: (b, 0, 2 * ATTN_HEADS + h))],
        out_specs=pl.BlockSpec((1, tq, LANES), lambda b, h, i, s: (b, i, h)),
        scratch_shapes=[pltpu.VMEM((tq, 1), F32), pltpu.VMEM((tq, 1), F32), pltpu.VMEM((tq, ATTN_VDIM), F32),
                        pltpu.VMEM((tq, 1), F32), pltpu.VMEM((tq, 1), F32), pltpu.VMEM((tq, ATTN_VDIM), F32)],
    )
    return pl.pallas_call(
        functools.partial(_attn_kernel, tq=tq, tk=tk, seq=S),
        out_shape=jax.ShapeDtypeStruct((B, S, ATTN_WIDTH), BF16),
        grid_spec=grid_spec,
        compiler_params=pltpu.CompilerParams(dimension_semantics=("parallel", "parallel", "arbitrary")),
        name="diff_attn",
    )(slopes, lam_params, subln, qkv, qkv, qkv)


def _prep_kernel(z_ref, zp_ref, zn_ref, tp_ref, tn_ref, w0_ref, w2_ref, a0_ref, a2_ref, g2_ref,
                 kk_ref, ka_ref, rk_ref,
                 r_o, v_o, kk_o, k0_o, k1_o, b0_o, b1_o, lw0_o, lw1_o, bonus_o, gate_o):
    i = pl.program_id(1)
    nt = pl.num_programs(1)
    z = z_ref[0]
    ts = z.shape[0]
    row = lax.broadcasted_iota(jnp.int32, (ts, 1), 0)
    prev_row = jnp.where(i > 0, zp_ref[0, 7:8, :], 0.0)
    next_row = jnp.where(i < nt - 1, zn_ref[0, 0:1, :], 0.0)
    z_prev = jnp.where(row == 0, prev_row, pltpu.roll(z, 1, 0))
    z_next = jnp.where(row == ts - 1, next_row, pltpu.roll(z, ts - 1, 0))
    zs = z + tp_ref[...] * (z_prev - z) + tn_ref[...] * (z_next - z)

    W = RWKV_WIDTH
    r = zs[:, 0:W]
    k = zs[:, W:2 * W]
    v = zs[:, 2 * W:3 * W]
    wd = jnp.tanh(zs[:, 3 * W:3 * W + LANES])
    ad = zs[:, 3 * W + LANES:3 * W + 2 * LANES]
    gd = zs[:, 3 * W + 2 * LANES:]

    w_log = w0_ref[...] + _dot3(wd, w2_ref[...])
    lw = -_sigmoid(w_log) * math.exp(-0.5)
    iclr = _sigmoid(a0_ref[...] + _dot3(ad, a2_ref[...]))
    gate = _dot3(_sigmoid(gd), g2_ref[...])

    ones = _head_ones(LANES)
    kk = k * kk_ref[...]
    kk = kk * lax.rsqrt(jnp.maximum(_head_sum(kk * kk, ones), 1e-24))
    ka = ka_ref[...]
    rk = rk_ref[...]
    bonus = jnp.zeros_like(r)
    k_outs = (k0_o, k1_o)
    b_outs = (b0_o, b1_o)
    lw_outs = (lw0_o, lw1_o)
    for d in range(N_DIR):
        a_d = iclr[:, d * W:(d + 1) * W]
        k_d = k * (1.0 + (a_d - 1.0) * ka)
        k_outs[d][0] = k_d
        b_outs[d][0] = kk * a_d
        lw_outs[d][0] = lw[:, d * W:(d + 1) * W]
        bonus = bonus + _head_sum(r * k_d * rk, ones) * v
    r_o[0] = r
    v_o[0] = v
    kk_o[0] = kk
    bonus_o[0] = bonus
    gate_o[0] = gate


def _rwkv_prep(z, tp, tn, w0, w2bd, a0, a2bd, g2p, k_k, k_a, r_k, *, ts):
    B, S, ZC = z.shape
    nt = S // ts
    hb = ts // 8
    last8 = S // 8 - 1
    const = lambda shape: pl.BlockSpec(shape, lambda b, i: (0, 0))
    out_sds = jax.ShapeDtypeStruct((B, S, RWKV_WIDTH), F32)
    out_spec = pl.BlockSpec((1, ts, RWKV_WIDTH), lambda b, i: (b, i, 0))
    return pl.pallas_call(
        _prep_kernel,
        out_shape=(out_sds,) * 11,
        grid=(B, nt),
        in_specs=[pl.BlockSpec((1, ts, ZC), lambda b, i: (b, i, 0)),
                  pl.BlockSpec((1, 8, ZC), lambda b, i: (b, jnp.maximum(i * hb - 1, 0), 0)),
                  pl.BlockSpec((1, 8, ZC), lambda b, i: (b, jnp.minimum((i + 1) * hb, last8), 0)),
                  const((1, ZC)), const((1, ZC)),
                  const((1, N_DIR * RWKV_WIDTH)), const((LANES, N_DIR * RWKV_WIDTH)),
                  const((1, N_DIR * RWKV_WIDTH)), const((LANES, N_DIR * RWKV_WIDTH)),
                  const((GATE_LORA_PAD, RWKV_WIDTH)),
                  const((1, RWKV_WIDTH)), const((1, RWKV_WIDTH)), const((1, RWKV_WIDTH))],
        out_specs=(out_spec,) * 11,
        compiler_params=pltpu.CompilerParams(dimension_semantics=("parallel", "arbitrary")),
        name="rwkv_prep",
    )(z, z, z, tp, tn, w0, w2bd, a0, a2bd, g2p, k_k, k_a, r_k)


def _scan_kernel(r_ref, kk_ref, k_ref, b_ref, v_ref, lw_ref, y_ref, s_ref, *, tb, reverse):
    i = pl.program_id(2)
    C = CHUNK
    W = LANES
    nc = tb // C

    @pl.when(i == 0)
    def _():
        s_ref[...] = jnp.zeros_like(s_ref)

    rt = lax.broadcasted_iota(jnp.int32, (tb, tb), 0)
    ct = lax.broadcasted_iota(jnp.int32, (tb, tb), 1)
    same = (rt // C) == (ct // C)
    before = (ct >= rt) if reverse else (ct <= rt)
    tri_in = jnp.where(same & before, 1.0, 0.0).astype(BF16)
    tri_out = jnp.where(same & jnp.logical_not(before), 1.0, 0.0).astype(BF16)
    lw = lw_ref[0]
    lw_hi, lw_lo = _split(lw)
    c = _dot(tri_in, lw_hi) + _dot(tri_in, lw_lo)
    d = _dot(tri_out, lw_hi) + _dot(tri_out, lw_lo)

    e_c = jnp.exp(c)
    e_nc = jnp.exp(-c)
    e_d = jnp.exp(d)
    at_all = -kk_ref[0] * jnp.exp(c - lw)
    rt_all = r_ref[0] * e_c
    bt_all = b_ref[0] * e_nc
    kt_all = k_ref[0] * e_nc
    bh_all = b_ref[0] * e_d
    kh_all = k_ref[0] * e_d
    v_all = v_ref[0]

    r2 = lax.broadcasted_iota(jnp.int32, (W, W), 0)
    c2 = lax.broadcasted_iota(jnp.int32, (W, W), 1)
    bd_mask = (r2 // C) == (c2 // RWKV_HEAD)
    eye = r2 == c2
    tr = lax.broadcasted_iota(jnp.int32, (C, W), 0)
    sc = lax.broadcasted_iota(jnp.int32, (C, W), 1) % C
    strict = (sc > tr) if reverse else (sc < tr)
    incl = (sc >= tr) if reverse else (sc <= tr)

    def bd(x):
        return jnp.where(bd_mask, jnp.concatenate([x] * PAIR, axis=0), 0.0).astype(BF16)

    def bd2(xa, xb):
        return jnp.concatenate([bd(xa), bd(xb)], axis=1)

    for jj in range(nc):
        j = (nc - 1 - jj) if reverse else jj
        sl = slice(j * C, (j + 1) * C)
        end = j * C if reverse else (j + 1) * C - 1
        At, Rt, Bt, Kt = at_all[sl], rt_all[sl], bt_all[sl], kt_all[sl]
        Bh, Kh, V = bh_all[sl], kh_all[sl], v_all[sl]
        g_end = e_c[end:end + 1, :]

        lhs = jnp.concatenate([At, Rt], axis=0).astype(BF16)
        sb = _dot_nt(lhs, bd(Bt))
        sk = _dot_nt(lhs, bd(Kt))
        Lab = jnp.where(strict, sb[:C], 0.0)
        Lak = jnp.where(strict, sk[:C], 0.0)
        Mrb = jnp.where(incl, sb[C:], 0.0).astype(BF16)
        Mrk = jnp.where(incl, sk[C:], 0.0).astype(BF16)
        bdV = bd(V)
        xa = At
        xu = _dot(Lak.astype(BF16), bdV)
        lp = Lab
        n_dbl = C.bit_length() - 1
        for it in range(n_dbl):
            upd = _dot(lp.astype(BF16), bd2(xa, xu))
            xa = xa + upd[:, :W]
            xu = xu + upd[:, W:]
            if it + 1 < n_dbl:
                lp = _dot(lp.astype(BF16), bd(lp))
        rb = _dot(Mrb, bd2(xa, xu))
        Rp = Rt + rb[:, :W]
        Y0 = rb[:, W:] + _dot(Mrk, bdV)
        au = jnp.concatenate([xa, xu], axis=1).astype(BF16)
        pb = _dot_tn(Bh.astype(BF16), au)
        kv = _dot_tn(Kh.astype(BF16), V.astype(BF16))
        Pbd = jnp.where(bd_mask, pb[:, :W], 0.0) + jnp.where(eye, g_end, 0.0)
        Qbd = jnp.where(bd_mask, pb[:, W:] + kv, 0.0)

        s16 = s_ref[...].astype(BF16)
        y_ref[0, sl, :] = _dot(Rp.astype(BF16), s16) + Y0
        s_ref[...] = _dot(Pbd.astype(BF16), s16) + Qbd


def _rwkv_scan(r, kk, k_d, b_d, v, lw_d, *, tb, reverse):
    B, S, Wd = r.shape
    nblk = S // tb
    if reverse:
        idx = lambda b, p, i: (b, nblk - 1 - i, p)
    else:
        idx = lambda b, p, i: (b, i, p)
    spec = pl.BlockSpec((1, tb, LANES), idx)
    return pl.pallas_call(
        functools.partial(_scan_kernel, tb=tb, reverse=reverse),
        out_shape=jax.ShapeDtypeStruct((B, S, Wd), F32),
        grid=(B, Wd // LANES, nblk),
        in_specs=[spec] * 6,
        out_specs=spec,
        scratch_shapes=[pltpu.VMEM((LANES, LANES), F32)],
        compiler_params=pltpu.CompilerParams(dimension_semantics=("parallel", "parallel", "arbitrary")),
        name="rwkv_scan_bwd" if reverse else "rwkv_scan_fwd",
    )(r, kk, k_d, b_d, v, lw_d)


def _outproj_kernel(x_ref, attn_ref, yf_ref, yb_ref, bonus_ref, gate_ref, lw_ref, lb_ref, w_ref, g_ref, o_ref):
    y = yf_ref[...] + yb_ref[...]
    mean_mat = _head_ones(LANES)
    inv_n = 1.0 / RWKV_HEAD
    mu = _head_sum(y, mean_mat) * inv_n
    yc = y - mu
    var = _head_sum(yc * yc, mean_mat) * inv_n
    yn = yc * lax.rsqrt(var + LNX_EPS) * lw_ref[...] + lb_ref[...]
    rw = ((yn + bonus_ref[...]) * gate_ref[...]).astype(BF16)
    m = _dot(attn_ref[...], w_ref[0:ATTN_WIDTH, :]) + _dot(rw, w_ref[ATTN_WIDTH:, :])
    ms = jnp.mean(m * m, axis=-1, keepdims=True)
    o_ref[...] = x_ref[...] + m * lax.rsqrt(ms + NORM_EPS) * g_ref[...]


def _outproj(x2d, attn2d, yf, yb, bonus, gate, lnx_w, lnx_b, w_out_bf16, gain, *, tm):
    T = x2d.shape[0]
    row = lambda w: pl.BlockSpec((tm, w), lambda m: (m, 0))
    const = lambda shape: pl.BlockSpec(shape, lambda m: (0, 0))
    return pl.pallas_call(
        _outproj_kernel,
        out_shape=jax.ShapeDtypeStruct((T, D_MODEL), F32),
        grid=(T // tm,),
        in_specs=[row(D_MODEL), row(ATTN_WIDTH), row(RWKV_WIDTH), row(RWKV_WIDTH), row(RWKV_WIDTH), row(RWKV_WIDTH),
                  const((1, RWKV_WIDTH)), const((1, RWKV_WIDTH)),
                  const((D_MODEL, D_MODEL)), const((1, D_MODEL))],
        out_specs=row(D_MODEL),
        compiler_params=pltpu.CompilerParams(dimension_semantics=("parallel",)),
        name="outproj",
    )(x2d, attn2d, yf, yb, bonus, gate, lnx_w, lnx_b, w_out_bf16, gain)


def _ffn_kernel(x_ref, gpre_ref, wg_ref, wu_ref, wd_ref, gpost_ref, o_ref, h_ref, acc_ref):
    j = pl.program_id(1)

    @pl.when(j == 0)
    def _():
        x = x_ref[...]
        ms = jnp.mean(x * x, axis=-1, keepdims=True)
        h_ref[...] = (x * lax.rsqrt(ms + NORM_EPS) * gpre_ref[...]).astype(BF16)
        acc_ref[...] = jnp.zeros_like(acc_ref)

    h = h_ref[...]
    g = _dot(h, wg_ref[...])
    u = _dot(h, wu_ref[...])
    a = (g * _sigmoid(g) * u).astype(BF16)
    acc_ref[...] += _dot(a, wd_ref[...])

    @pl.when(j == pl.num_programs(1) - 1)
    def _():
        f = acc_ref[...]
        ms = jnp.mean(f * f, axis=-1, keepdims=True)
        o_ref[...] = x_ref[...] + f * lax.rsqrt(ms + NORM_EPS) * gpost_ref[...]


def _ffn(x2d, gpre, wg, wu, wd, gpost, *, tm, tf):
    T = x2d.shape[0]
    return pl.pallas_call(
        _ffn_kernel,
        out_shape=jax.ShapeDtypeStruct((T, D_MODEL), F32),
        grid=(T // tm, D_FF // tf),
        in_specs=[pl.BlockSpec((tm, D_MODEL), lambda m, j: (m, 0)),
                  pl.BlockSpec((1, D_MODEL), lambda m, j: (0, 0)),
                  pl.BlockSpec((D_MODEL, tf), lambda m, j: (0, j)),
                  pl.BlockSpec((D_MODEL, tf), lambda m, j: (0, j)),
                  pl.BlockSpec((tf, D_MODEL), lambda m, j: (j, 0)),
                  pl.BlockSpec((1, D_MODEL), lambda m, j: (0, 0))],
        out_specs=pl.BlockSpec((tm, D_MODEL), lambda m, j: (m, 0)),
        scratch_shapes=[pltpu.VMEM((tm, D_MODEL), BF16), pltpu.VMEM((tm, D_MODEL), F32)],
        compiler_params=pltpu.CompilerParams(dimension_semantics=("parallel", "arbitrary")),
        name="ffn",
    )(x2d, gpre, wg, wu, wd, gpost)


def _tiles(S):
    return dict(tm=512, tn=512, tq=256, tk=512, ts=256, tb=512, tf=512)


def _lora_blockdiag(w):
    zero = jnp.zeros_like(w[0])
    return jnp.concatenate([jnp.concatenate([w[0], zero], axis=1),
                            jnp.concatenate([zero, w[1]], axis=1)], axis=0)


def _encoder_layer(x, norm_mix_pre, norm_mix_post, w_in, w_out, lambda_q1, lambda_k1, lambda_q2, lambda_k2,
                   attn_subln, tshift_prev, tshift_next, w0, w2, a0, a2, g2, k_k, k_a, r_k, lnx_w, lnx_b,
                   norm_ffn_pre, norm_ffn_post, w_gate, w_up, w_down):
    B, S, D = x.shape
    T = B * S
    t = _tiles(S)
    row = lambda a: a.reshape(1, -1).astype(F32)
    pad_cols = RWKV_COLS_PAD - RWKV_COLS

    w_in_p = jnp.pad(w_in, ((0, 0), (0, pad_cols))).astype(BF16)
    x2d = x.reshape(T, D)
    qkv, z = _inproj(x2d, row(norm_mix_pre), w_in_p, tm=t["tm"], tn=t["tn"])

    lam_params = jnp.stack([lambda_q1, lambda_k1, lambda_q2, lambda_k2]).astype(F32)
    attn = _diff_attention(qkv.reshape(B, S, QKV_COLS), lam_params, row(attn_subln), tq=t["tq"], tk=t["tk"])

    tp = jnp.pad(tshift_prev, (0, pad_cols)).reshape(1, -1)
    tn_ = jnp.pad(tshift_next, (0, pad_cols)).reshape(1, -1)
    g2p = jnp.pad(g2, ((0, GATE_LORA_PAD - GATE_LORA), (0, 0)))
    (r, v, kk, k0, k1, b0, b1, lw0, lw1, bonus, gate) = _rwkv_prep(
        z.reshape(B, S, RWKV_COLS_PAD), tp, tn_, row(w0), _lora_blockdiag(w2), row(a0), _lora_blockdiag(a2), g2p,
        row(k_k), row(k_a), row(r_k), ts=t["ts"])
    yf = _rwkv_scan(r, kk, k0, b0, v, lw0, tb=t["tb"], reverse=False)
    yb = _rwkv_scan(r, kk, k1, b1, v, lw1, tb=t["tb"], reverse=True)

    flat = lambda a: a.reshape(T, -1)
    x1 = _outproj(x2d, flat(attn), flat(yf), flat(yb), flat(bonus), flat(gate), row(lnx_w), row(lnx_b),
                  w_out.astype(BF16), row(norm_mix_post), tm=t["tm"])
    out = _ffn(x1, row(norm_ffn_pre), w_gate.astype(BF16), w_up.astype(BF16), w_down.astype(BF16),
               row(norm_ffn_post), tm=t["tm"], tf=t["tf"])
    return out.reshape(B, S, D)


def kernel(x_prompt, x_sample, norm_mix_pre, norm_mix_post, w_in, w_out, lambda_q1, lambda_k1, lambda_q2,
           lambda_k2, attn_subln, tshift_prev, tshift_next, w0, w2, a0, a2, g2, k_k, k_a, r_k, lnx_w, lnx_b,
           norm_ffn_pre, norm_ffn_post, w_gate, w_up, w_down):
    assert x_prompt.shape[1:] == x_sample.shape[1:], "both trunks share the sequence length"
    assert norm_mix_pre.shape[0] == 1, "single layer"
    nb = x_prompt.shape[0]
    x = jnp.concatenate([x_prompt, x_sample], axis=0)
    y = _encoder_layer(x, norm_mix_pre[0], norm_mix_post[0], w_in[0], w_out[0], lambda_q1[0], lambda_k1[0],
                       lambda_q2[0], lambda_k2[0], attn_subln[0], tshift_prev[0], tshift_next[0], w0[0], w2[0],
                       a0[0], a2[0], g2[0], k_k[0], k_a[0], r_k[0], lnx_w[0], lnx_b[0], norm_ffn_pre[0],
                       norm_ffn_post[0], w_gate[0], w_up[0], w_down[0])
    return (y[:nb], y[nb:])
```

```python
import functools
import math

import jax
import jax.numpy as jnp
from jax import lax
from jax.experimental import pallas as pl
from jax.experimental.pallas import tpu as pltpu

F32 = jnp.float32
BF16 = jnp.bfloat16

D_MODEL = 2048
ATTN_HEADS = 8
ATTN_VDIM = 128
ATTN_QKDIM = 64
ATTN_WIDTH = ATTN_HEADS * ATTN_VDIM
RWKV_HEAD = 64
RWKV_HEADS = 16
RWKV_WIDTH = RWKV_HEAD * RWKV_HEADS
N_DIR = 2
LORA = 64
GATE_LORA = 160
GATE_LORA_PAD = 256
QKV_COLS = 3 * ATTN_WIDTH
RWKV_COLS = 3 * RWKV_WIDTH + 2 * N_DIR * LORA + GATE_LORA
RWKV_COLS_PAD = 3 * RWKV_WIDTH + 2 * N_DIR * LORA + GATE_LORA_PAD
D_FF = 5632
NORM_EPS = 1e-6
LNX_EPS = 64e-5
LAMBDA_INIT = 0.8 - 0.6 * math.exp(-0.3 * 0)

LANES = 128
CHUNK = 64
PAIR = LANES // RWKV_HEAD


def _dot(a, b):
    return jnp.dot(a, b, preferred_element_type=F32)


def _dot_nt(a, b):
    return lax.dot_general(a, b, (((1,), (1,)), ((), ())), preferred_element_type=F32)


def _dot_tn(a, b):
    return lax.dot_general(a, b, (((0,), (0,)), ((), ())), preferred_element_type=F32)


def _split(x):
    hi = x.astype(BF16)
    lo = (x - hi.astype(F32)).astype(BF16)
    return hi, lo


def _dot3(a, b):
    ah, al = _split(a)
    bh, bl = _split(b)
    return _dot(ah, bh) + _dot(al, bh) + _dot(ah, bl)


def _sigmoid(x):
    return 1.0 / (1.0 + jnp.exp(-x))


def _head_ones(width):
    r = lax.broadcasted_iota(jnp.int32, (width, width), 0) // RWKV_HEAD
    c = lax.broadcasted_iota(jnp.int32, (width, width), 1) // RWKV_HEAD
    return jnp.where(r == c, 1.0, 0.0).astype(BF16)


def _head_sum(x, ones):
    outs = []
    for g in range(x.shape[1] // LANES):
        hi, lo = _split(x[:, g * LANES:(g + 1) * LANES])
        outs.append(_dot(hi, ones) + _dot(lo, ones))
    return jnp.concatenate(outs, axis=1)


def _inproj_kernel(x_ref, g_ref, w_ref, qkv_ref, z_ref, h_ref, *, n_qkv_tiles):
    n = pl.program_id(1)

    @pl.when(n == 0)
    def _():
        x = x_ref[...]
        ms = jnp.mean(x * x, axis=-1, keepdims=True)
        h_ref[...] = (x * lax.rsqrt(ms + NORM_EPS) * g_ref[...]).astype(BF16)

    acc = _dot(h_ref[...], w_ref[...])

    @pl.when(n < n_qkv_tiles)
    def _():
        qkv_ref[...] = acc.astype(BF16)

    @pl.when(n >= n_qkv_tiles)
    def _():
        z_ref[...] = acc


def _inproj(x2d, gain, w_in_bf16, *, tm, tn):
    T = x2d.shape[0]
    n_cols = w_in_bf16.shape[1]
    nq = QKV_COLS // tn
    grid = (T // tm, n_cols // tn)
    return pl.pallas_call(
        functools.partial(_inproj_kernel, n_qkv_tiles=nq),
        out_shape=(jax.ShapeDtypeStruct((T, QKV_COLS), BF16),
                   jax.ShapeDtypeStruct((T, RWKV_COLS_PAD), F32)),
        grid=grid,
        in_specs=[pl.BlockSpec((tm, D_MODEL), lambda m, n: (m, 0)),
                  pl.BlockSpec((1, D_MODEL), lambda m, n: (0, 0)),
                  pl.BlockSpec((D_MODEL, tn), lambda m, n: (0, n))],
        out_specs=(pl.BlockSpec((tm, tn), lambda m, n: (m, jnp.minimum(n, nq - 1))),
                   pl.BlockSpec((tm, tn), lambda m, n: (m, jnp.maximum(n - nq, 0)))),
        scratch_shapes=[pltpu.VMEM((tm, D_MODEL), BF16)],
        compiler_params=pltpu.CompilerParams(dimension_semantics=("parallel", "arbitrary")),
        name="inproj",
    )(x2d, gain, w_in_bf16)


LOG2E = math.log2(math.e)
Q_PRESCALE = ATTN_QKDIM ** -0.5 * LOG2E
AUX = 6
NEG_BIG = -1e30


def _pos_aux(slope2, n, first, sign_pos, sign_one):
    row = lax.broadcasted_iota(jnp.int32, (n, LANES), 0).astype(F32)
    lane = lax.broadcasted_iota(jnp.int32, (n, LANES), 1)
    val = slope2 * row * sign_pos
    hi = val.astype(BF16).astype(F32)
    mid = (val - hi).astype(BF16).astype(F32)
    lo = (val - hi - mid).astype(BF16).astype(F32)
    out = jnp.where(lane == first, hi, 0.0)
    out = jnp.where(lane == first + 1, mid, out)
    out = jnp.where(lane == first + 2, lo, out)
    out = jnp.where((lane >= first + 3) & (lane < first + AUX), sign_one, out)
    return out.astype(BF16)


def _attn_kernel(slopes_ref, lam_ref, sub_ref, q_ref, k_ref, v_ref, o_ref,
                 ka_ref, vt_ref, m_ref, l_ref, acc_ref, s_ref, p_ref, al_ref, *, t, seq):
    h = pl.program_id(1)
    qi = pl.program_id(2)
    nt = seq // t
    slope2 = slopes_ref[h] * LOG2E
    lane = lax.broadcasted_iota(jnp.int32, (t, LANES), 1)
    own = (lane < ATTN_QKDIM, lane >= ATTN_QKDIM)
    aux0 = (ATTN_QKDIM, 0)

    @pl.when(qi == 0)
    def _():
        ak = [_pos_aux(slope2, t, aux0[c], 1.0, 1.0) for c in range(2)]

        def build(j, carry):
            off = pl.multiple_of(j * t, t)
            kt = k_ref[0, pl.ds(off, t), :]
            for c in range(2):
                ka_ref[c, j] = jnp.where(own[c], kt, ak[c])
            vt_ref[j] = v_ref[0, pl.ds(off, t), :].astype(F32).T.astype(BF16)
            return carry

        lax.fori_loop(0, nt, build, 0)

    q = q_ref[0]
    zero = jnp.zeros((t, LANES), BF16)
    q_var = {}
    for c in range(2):
        a = aux0[c]
        lane_pos = (lane >= a + 3) & (lane < a + AUX)
        lane_one = (lane >= a) & (lane < a + 3)
        pos = _pos_aux(slope2, t, a + 3, 1.0, 0.0)
        left = jnp.where(lane_one, 1.0, jnp.where(lane_pos, -pos.astype(F32), 0.0)).astype(BF16)
        right = jnp.where(lane_one, -1.0, jnp.where(lane_pos, pos.astype(F32), 0.0)).astype(BF16)
        q_var[c] = (jnp.where(own[c], q, left), jnp.where(own[c], q, zero), jnp.where(own[c], q, right))

    rel = (lax.broadcasted_iota(jnp.int32, (t, t), 0) - lax.broadcasted_iota(jnp.int32, (t, t), 1)).astype(F32)
    bias_diag = -slope2 * jnp.abs(rel)

    m_ref[...] = jnp.full(m_ref.shape, NEG_BIG, F32)
    l_ref[...] = jnp.zeros(l_ref.shape, F32)
    acc_ref[...] = jnp.zeros(acc_ref.shape, F32)

    def softmax_update(c, s, cst):
        m_old = m_ref[c]
        m_new = jnp.maximum(m_old, jnp.max(s, axis=0, keepdims=True) + cst)
        alpha = jnp.exp2(m_old - m_new)
        p = jnp.exp2(s - (m_new - cst))
        l_ref[c] = alpha * l_ref[c] + jnp.sum(p, axis=0, keepdims=True)
        m_ref[c] = m_new
        return alpha, p.astype(BF16)

    for c in range(2):
        s = _dot_nt(ka_ref[c, qi], q_var[c][1]) + bias_diag
        alpha, p = softmax_update(c, s, 0.0)
        acc_ref[c] = alpha * acc_ref[c] + _dot(vt_ref[qi], p)

    n_off = nt - 1

    def key_tile(n):
        return jnp.where(n >= qi, n + 1, n)

    def stage_scores(n, slot):
        j = key_tile(n)
        for c in range(2):
            qsel = jnp.where(j < qi, q_var[c][0], q_var[c][2])
            s_ref[slot, c] = _dot_nt(ka_ref[c, j], qsel)

    def stage_softmax(n, slot):
        j = key_tile(n)
        cst = -slope2 * (jnp.abs(qi - j) * t).astype(F32)
        for c in range(2):
            alpha, p = softmax_update(c, s_ref[slot, c], cst)
            p_ref[slot, c] = p
            al_ref[slot, c] = alpha

    def stage_values(n, slot):
        j = key_tile(n)
        for c in range(2):
            acc_ref[c] = al_ref[slot, c] * acc_ref[c] + _dot(vt_ref[j], p_ref[slot, c])

    def step(n, par):
        static = isinstance(n, int)
        if not static or n - 2 >= 0 and n - 2 < n_off:
            stage_values(n - 2, par)
        if not static or n - 1 >= 0 and n - 1 < n_off:
            stage_softmax(n - 1, 1 - par)
        if not static or n < n_off:
            stage_scores(n, par)

    n_steps = n_off + 2 if n_off > 0 else 0
    full_lo, full_hi = 2, n_off
    n_pairs = max(full_hi - full_lo, 0) // 2
    for n in range(min(full_lo, n_steps)):
        step(n, n % 2)
    if n_pairs > 0:
        def pair_body(mi, carry):
            n0 = full_lo + 2 * mi
            step(n0, 0)
            step(n0 + 1, 1)
            return carry
        lax.fori_loop(0, n_pairs, pair_body, 0)
    for n in range(full_lo + 2 * n_pairs, n_steps):
        step(n, n % 2)

    lp = lam_ref[...]
    lam = (jnp.exp(jnp.sum(lp[0:1] * lp[1:2], axis=-1, keepdims=True))
           - jnp.exp(jnp.sum(lp[2:3] * lp[3:4], axis=-1, keepdims=True)) + LAMBDA_INIT)
    o_t = acc_ref[0] / l_ref[0] - lam * (acc_ref[1] / l_ref[1])
    o = o_t.T
    ms = jnp.mean(o * o, axis=-1, keepdims=True)
    o = o * lax.rsqrt(ms + NORM_EPS) * sub_ref[...] * (1.0 - LAMBDA_INIT)
    o_ref[0] = o.astype(o_ref.dtype)


def _diff_attention(qkv, lam_params, subln, *, t):
    B, S, _ = qkv.shape
    nt = S // t
    slopes = jnp.asarray([2.0 ** (-8.0 * (h + 1.0) / ATTN_HEADS) for h in range(ATTN_HEADS)], F32)
    grid_spec = pltpu.PrefetchScalarGridSpec(
        num_scalar_prefetch=1,
        grid=(B, ATTN_HEADS, nt),
        in_specs=[pl.BlockSpec((4, ATTN_QKDIM), lambda b, h, i, s: (0, 0)),
                  pl.BlockSpec((1, ATTN_VDIM), lambda b, h, i, s: (0, 0)),
                  pl.BlockSpec((1, t, LANES), lambda b, h, i, s: (b, i, h)),
                  pl.BlockSpec((1, S, LANES), lambda b, h, i, s: (b, 0, ATTN_HEADS + h)),
                  pl.BlockSpec((1, S, LANES), lambda b, h, i, s: (b, 0, 2 * ATTN_HEADS + h))],
        out_specs=pl.BlockSpec((1, t, LANES), lambda b, h, i, s: (b, i, h)),
        scratch_shapes=[pltpu.VMEM((2, nt, t, LANES), BF16),
                        pltpu.VMEM((nt, ATTN_VDIM, t), BF16),
                        pltpu.VMEM((2, 1, t), F32), pltpu.VMEM((2, 1, t), F32),
                        pltpu.VMEM((2, ATTN_VDIM, t), F32),
                        pltpu.VMEM((2, 2, t, t), F32),
                        pltpu.VMEM((2, 2, t, t), BF16),
                        pltpu.VMEM((2, 2, 1, t), F32)],
    )
    return pl.pallas_call(
        functools.partial(_attn_kernel, t=t, seq=S),
        out_shape=jax.ShapeDtypeStruct((B, S, ATTN_WIDTH), BF16),
        grid_spec=grid_spec,
        compiler_params=pltpu.CompilerParams(dimension_semantics=("parallel", "parallel", "arbitrary")),
        name="diff_attn",
    )(slopes, lam_params, subln, qkv, qkv, qkv)


def _prep_kernel(z_ref, zp_ref, zn_ref, tp_ref, tn_ref, w0_ref, w2_ref, a0_ref, a2_ref, g2_ref,
                 kk_ref, ka_ref, rk_ref,
                 r_o, v_o, kk_o, k0_o, k1_o, b0_o, b1_o, lw0_o, lw1_o, bonus_o, gate_o):
    i = pl.program_id(1)
    nt = pl.num_programs(1)
    z = z_ref[0]
    ts = z.shape[0]
    row = lax.broadcasted_iota(jnp.int32, (ts, 1), 0)
    prev_row = jnp.where(i > 0, zp_ref[0, 7:8, :], 0.0)
    next_row = jnp.where(i < nt - 1, zn_ref[0, 0:1, :], 0.0)
    z_prev = jnp.where(row == 0, prev_row, pltpu.roll(z, 1, 0))
    z_next = jnp.where(row == ts - 1, next_row, pltpu.roll(z, ts - 1, 0))
    zs = z + tp_ref[...] * (z_prev - z) + tn_ref[...] * (z_next - z)

    W = RWKV_WIDTH
    r = zs[:, 0:W]
    k = zs[:, W:2 * W]
    v = zs[:, 2 * W:3 * W]
    wd = jnp.tanh(zs[:, 3 * W:3 * W + LANES])
    ad = zs[:, 3 * W + LANES:3 * W + 2 * LANES]
    gd = zs[:, 3 * W + 2 * LANES:]

    w_log = w0_ref[...] + _dot3(wd, w2_ref[...])
    lw = -_sigmoid(w_log) * math.exp(-0.5)
    iclr = _sigmoid(a0_ref[...] + _dot3(ad, a2_ref[...]))
    gate = _dot3(_sigmoid(gd), g2_ref[...])

    ones = _head_ones(LANES)
    kk = k * kk_ref[...]
    kk = kk * lax.rsqrt(jnp.maximum(_head_sum(kk * kk, ones), 1e-24))
    ka = ka_ref[...]
    rk = rk_ref[...]
    bonus = jnp.zeros_like(r)
    k_outs = (k0_o, k1_o)
    b_outs = (b0_o, b1_o)
    lw_outs = (lw0_o, lw1_o)
    for d in range(N_DIR):
        a_d = iclr[:, d * W:(d + 1) * W]
        k_d = k * (1.0 + (a_d - 1.0) * ka)
        k_outs[d][0] = k_d
        b_outs[d][0] = kk * a_d
        lw_outs[d][0] = lw[:, d * W:(d + 1) * W]
        bonus = bonus + _head_sum(r * k_d * rk, ones) * v
    r_o[0] = r
    v_o[0] = v
    kk_o[0] = kk
    bonus_o[0] = bonus
    gate_o[0] = gate


def _rwkv_prep(z, tp, tn, w0, w2bd, a0, a2bd, g2p, k_k, k_a, r_k, *, ts):
    B, S, ZC = z.shape
    nt = S // ts
    hb = ts // 8
    last8 = S // 8 - 1
    const = lambda shape: pl.BlockSpec(shape, lambda b, i: (0, 0))
    out_sds = jax.ShapeDtypeStruct((B, S, RWKV_WIDTH), F32)
    out_spec = pl.BlockSpec((1, ts, RWKV_WIDTH), lambda b, i: (b, i, 0))
    return pl.pallas_call(
        _prep_kernel,
        out_shape=(out_sds,) * 11,
        grid=(B, nt),
        in_specs=[pl.BlockSpec((1, ts, ZC), lambda b, i: (b, i, 0)),
                  pl.BlockSpec((1, 8, ZC), lambda b, i: (b, jnp.maximum(i * hb - 1, 0), 0)),
                  pl.BlockSpec((1, 8, ZC), lambda b, i: (b, jnp.minimum((i + 1) * hb, last8), 0)),
                  const((1, ZC)), const((1, ZC)),
                  const((1, N_DIR * RWKV_WIDTH)), const((LANES, N_DIR * RWKV_WIDTH)),
                  const((1, N_DIR * RWKV_WIDTH)), const((LANES, N_DIR * RWKV_WIDTH)),
                  const((GATE_LORA_PAD, RWKV_WIDTH)),
                  const((1, RWKV_WIDTH)), const((1, RWKV_WIDTH)), const((1, RWKV_WIDTH))],
        out_specs=(out_spec,) * 11,
        compiler_params=pltpu.CompilerParams(dimension_semantics=("parallel", "arbitrary")),
        name="rwkv_prep",
    )(z, z, z, tp, tn, w0, w2bd, a0, a2bd, g2p, k_k, k_a, r_k)


def _scan_kernel(r_ref, kk_ref, k_ref, b_ref, v_ref, lw_ref, y_ref, s_ref, *, tb, reverse):
    i = pl.program_id(2)
    C = CHUNK
    W = LANES
    nc = tb // C

    @pl.when(i == 0)
    def _():
        s_ref[...] = jnp.zeros_like(s_ref)

    rt = lax.broadcasted_iota(jnp.int32, (tb, tb), 0)
    ct = lax.broadcasted_iota(jnp.int32, (tb, tb), 1)
    same = (rt // C) == (ct // C)
    before = (ct >= rt) if reverse else (ct <= rt)
    tri_in = jnp.where(same & before, 1.0, 0.0).astype(BF16)
    tri_out = jnp.where(same & jnp.logical_not(before), 1.0, 0.0).astype(BF16)
    lw = lw_ref[0]
    lw_hi, lw_lo = _split(lw)
    c = _dot(tri_in, lw_hi) + _dot(tri_in, lw_lo)
    d = _dot(tri_out, lw_hi) + _dot(tri_out, lw_lo)

    e_c = jnp.exp(c)
    e_nc = jnp.exp(-c)
    e_d = jnp.exp(d)
    at_all = -kk_ref[0] * jnp.exp(c - lw)
    rt_all = r_ref[0] * e_c
    bt_all = b_ref[0] * e_nc
    kt_all = k_ref[0] * e_nc
    bh_all = b_ref[0] * e_d
    kh_all = k_ref[0] * e_d
    v_all = v_ref[0]

    r2 = lax.broadcasted_iota(jnp.int32, (W, W), 0)
    c2 = lax.broadcasted_iota(jnp.int32, (W, W), 1)
    bd_mask = (r2 // C) == (c2 // RWKV_HEAD)
    eye = r2 == c2
    tr = lax.broadcasted_iota(jnp.int32, (C, W), 0)
    sc = lax.broadcasted_iota(jnp.int32, (C, W), 1) % C
    strict = (sc > tr) if reverse else (sc < tr)
    incl = (sc >= tr) if reverse else (sc <= tr)

    def bd(x):
        return jnp.where(bd_mask, jnp.concatenate([x] * PAIR, axis=0), 0.0).astype(BF16)

    def bd2(xa, xb):
        return jnp.concatenate([bd(xa), bd(xb)], axis=1)

    order = [(nc - 1 - jj) if reverse else jj for jj in range(nc)]
    sls = [slice(j * C, (j + 1) * C) for j in order]
    ends = [j * C if reverse else (j + 1) * C - 1 for j in order]
    At = [at_all[sl] for sl in sls]
    Rt = [rt_all[sl] for sl in sls]
    V = [v_all[sl] for sl in sls]
    bdV = [bd(x) for x in V]
    lhs = [jnp.concatenate([a, r], axis=0).astype(BF16) for a, r in zip(At, Rt)]
    sb = [_dot_nt(l, bd(bt_all[sl])) for l, sl in zip(lhs, sls)]
    sk = [_dot_nt(l, bd(kt_all[sl])) for l, sl in zip(lhs, sls)]
    lp = [jnp.where(strict, x[:C], 0.0) for x in sb]
    Lak = [jnp.where(strict, x[:C], 0.0).astype(BF16) for x in sk]
    Mrb = [jnp.where(incl, x[C:], 0.0).astype(BF16) for x in sb]
    Mrk = [jnp.where(incl, x[C:], 0.0).astype(BF16) for x in sk]
    xa = list(At)
    xu = [_dot(l, b) for l, b in zip(Lak, bdV)]
    n_dbl = C.bit_length() - 1
    for it in range(n_dbl):
        lp16 = [x.astype(BF16) for x in lp]
        upd = [_dot(l, bd2(a, u)) for l, a, u in zip(lp16, xa, xu)]
        xa = [a + x[:, :W] for a, x in zip(xa, upd)]
        xu = [u + x[:, W:] for u, x in zip(xu, upd)]
        if it + 1 < n_dbl:
            lp = [_dot(l16, bd(l)) for l16, l in zip(lp16, lp)]
    rb = [_dot(m, bd2(a, u)) for m, a, u in zip(Mrb, xa, xu)]
    Rp = [(r + x[:, :W]).astype(BF16) for r, x in zip(Rt, rb)]
    Y0 = [x[:, W:] + _dot(m, b) for x, m, b in zip(rb, Mrk, bdV)]
    au = [jnp.concatenate([a, u], axis=1).astype(BF16) for a, u in zip(xa, xu)]
    pb = [_dot_tn(bh_all[sl].astype(BF16), x) for sl, x in zip(sls, au)]
    kv = [_dot_tn(kh_all[sl].astype(BF16), x.astype(BF16)) for sl, x in zip(sls, V)]
    Pbd = [(jnp.where(bd_mask, x[:, :W], 0.0) + jnp.where(eye, e_c[e:e + 1, :], 0.0)).astype(BF16)
           for x, e in zip(pb, ends)]
    Qbd = [jnp.where(bd_mask, x[:, W:] + y, 0.0) for x, y in zip(pb, kv)]

    s = s_ref[...]
    for n in range(nc):
        s16 = s.astype(BF16)
        y_ref[0, sls[n], :] = _dot(Rp[n], s16) + Y0[n]
        s = _dot(Pbd[n], s16) + Qbd[n]
    s_ref[...] = s


def _rwkv_scan(r, kk, k_d, b_d, v, lw_d, *, tb, reverse):
    B, S, Wd = r.shape
    nblk = S // tb
    if reverse:
        idx = lambda b, p, i: (b, nblk - 1 - i, p)
    else:
        idx = lambda b, p, i: (b, i, p)
    spec = pl.BlockSpec((1, tb, LANES), idx)
    return pl.pallas_call(
        functools.partial(_scan_kernel, tb=tb, reverse=reverse),
        out_shape=jax.ShapeDtypeStruct((B, S, Wd), F32),
        grid=(B, Wd // LANES, nblk),
        in_specs=[spec] * 6,
        out_specs=spec,
        scratch_shapes=[pltpu.VMEM((LANES, LANES), F32)],
        compiler_params=pltpu.CompilerParams(dimension_semantics=("parallel", "parallel", "arbitrary")),
        name="rwkv_scan_bwd" if reverse else "rwkv_scan_fwd",
    )(r, kk, k_d, b_d, v, lw_d)


def _outproj_kernel(x_ref, attn_ref, yf_ref, yb_ref, bonus_ref, gate_ref, lw_ref, lb_ref, w_ref, g_ref, o_ref):
    y = yf_ref[...] + yb_ref[...]
    mean_mat = _head_ones(LANES)
    inv_n = 1.0 / RWKV_HEAD
    mu = _head_sum(y, mean_mat) * inv_n
    yc = y - mu
    var = _head_sum(yc * yc, mean_mat) * inv_n
    yn = yc * lax.rsqrt(var + LNX_EPS) * lw_ref[...] + lb_ref[...]
    rw = ((yn + bonus_ref[...]) * gate_ref[...]).astype(BF16)
    m = _dot(attn_ref[...], w_ref[0:ATTN_WIDTH, :]) + _dot(rw, w_ref[ATTN_WIDTH:, :])
    ms = jnp.mean(m * m, axis=-1, keepdims=True)
    o_ref[...] = x_ref[...] + m * lax.rsqrt(ms + NORM_EPS) * g_ref[...]


def _outproj(x2d, attn2d, yf, yb, bonus, gate, lnx_w, lnx_b, w_out_bf16, gain, *, tm):
    T = x2d.shape[0]
    row = lambda w: pl.BlockSpec((tm, w), lambda m: (m, 0))
    const = lambda shape: pl.BlockSpec(shape, lambda m: (0, 0))
    return pl.pallas_call(
        _outproj_kernel,
        out_shape=jax.ShapeDtypeStruct((T, D_MODEL), F32),
        grid=(T // tm,),
        in_specs=[row(D_MODEL), row(ATTN_WIDTH), row(RWKV_WIDTH), row(RWKV_WIDTH), row(RWKV_WIDTH), row(RWKV_WIDTH),
                  const((1, RWKV_WIDTH)), const((1, RWKV_WIDTH)),
                  const((D_MODEL, D_MODEL)), const((1, D_MODEL))],
        out_specs=row(D_MODEL),
        compiler_params=pltpu.CompilerParams(dimension_semantics=("parallel",)),
        name="outproj",
    )(x2d, attn2d, yf, yb, bonus, gate, lnx_w, lnx_b, w_out_bf16, gain)


def _ffn_kernel(x_ref, gpre_ref, wg_ref, wu_ref, wd_ref, gpost_ref, o_ref, h_ref, acc_ref):
    j = pl.program_id(1)

    @pl.when(j == 0)
    def _():
        x = x_ref[...]
        ms = jnp.mean(x * x, axis=-1, keepdims=True)
        h_ref[...] = (x * lax.rsqrt(ms + NORM_EPS) * gpre_ref[...]).astype(BF16)
        acc_ref[...] = jnp.zeros_like(acc_ref)

    h = h_ref[...]
    g = _dot(h, wg_ref[...])
    u = _dot(h, wu_ref[...])
    a = (g * _sigmoid(g) * u).astype(BF16)
    acc_ref[...] += _dot(a, wd_ref[...])

    @pl.when(j == pl.num_programs(1) - 1)
    def _():
        f = acc_ref[...]
        ms = jnp.mean(f * f, axis=-1, keepdims=True)
        o_ref[...] = x_ref[...] + f * lax.rsqrt(ms + NORM_EPS) * gpost_ref[...]


def _ffn(x2d, gpre, wg, wu, wd, gpost, *, tm, tf):
    T = x2d.shape[0]
    return pl.pallas_call(
        _ffn_kernel,
        out_shape=jax.ShapeDtypeStruct((T, D_MODEL), F32),
        grid=(T // tm, D_FF // tf),
        in_specs=[pl.BlockSpec((tm, D_MODEL), lambda m, j: (m, 0)),
                  pl.BlockSpec((1, D_MODEL), lambda m, j: (0, 0)),
                  pl.BlockSpec((D_MODEL, tf), lambda m, j: (0, j)),
                  pl.BlockSpec((D_MODEL, tf), lambda m, j: (0, j)),
                  pl.BlockSpec((tf, D_MODEL), lambda m, j: (j, 0)),
                  pl.BlockSpec((1, D_MODEL), lambda m, j: (0, 0))],
        out_specs=pl.BlockSpec((tm, D_MODEL), lambda m, j: (m, 0)),
        scratch_shapes=[pltpu.VMEM((tm, D_MODEL), BF16), pltpu.VMEM((tm, D_MODEL), F32)],
        compiler_params=pltpu.CompilerParams(dimension_semantics=("parallel", "arbitrary")),
        name="ffn",
    )(x2d, gpre, wg, wu, wd, gpost)


def _tiles(S):
    return dict(tm=512, tn=512, ta=512, ts=256, tb=512, tf=512)


def _lora_blockdiag(w):
    zero = jnp.zeros_like(w[0])
    return jnp.concatenate([jnp.concatenate([w[0], zero], axis=1),
                            jnp.concatenate([zero, w[1]], axis=1)], axis=0)


def _encoder_layer(x, norm_mix_pre, norm_mix_post, w_in, w_out, lambda_q1, lambda_k1, lambda_q2, lambda_k2,
                   attn_subln, tshift_prev, tshift_next, w0, w2, a0, a2, g2, k_k, k_a, r_k, lnx_w, lnx_b,
                   norm_ffn_pre, norm_ffn_post, w_gate, w_up, w_down):
    B, S, D = x.shape
    T = B * S
    t = _tiles(S)
    row = lambda a: a.reshape(1, -1).astype(F32)
    pad_cols = RWKV_COLS_PAD - RWKV_COLS

    col_scale = jnp.where(jnp.arange(w_in.shape[1]) < ATTN_WIDTH, Q_PRESCALE, 1.0).astype(F32)
    w_in_p = jnp.pad(w_in * col_scale, ((0, 0), (0, pad_cols))).astype(BF16)
    x2d = x.reshape(T, D)
    qkv, z = _inproj(x2d, row(norm_mix_pre), w_in_p, tm=t["tm"], tn=t["tn"])

    lam_params = jnp.stack([lambda_q1, lambda_k1, lambda_q2, lambda_k2]).astype(F32)
    attn = _diff_attention(qkv.reshape(B, S, QKV_COLS), lam_params, row(attn_subln), t=t["ta"])

    tp = jnp.pad(tshift_prev, (0, pad_cols)).reshape(1, -1)
    tn_ = jnp.pad(tshift_next, (0, pad_cols)).reshape(1, -1)
    g2p = jnp.pad(g2, ((0, GATE_LORA_PAD - GATE_LORA), (0, 0)))
    (r, v, kk, k0, k1, b0, b1, lw0, lw1, bonus, gate) = _rwkv_prep(
        z.reshape(B, S, RWKV_COLS_PAD), tp, tn_, row(w0), _lora_blockdiag(w2), row(a0), _lora_blockdiag(a2), g2p,
        row(k_k), row(k_a), row(r_k), ts=t["ts"])
    yf = _rwkv_scan(r, kk, k0, b0, v, lw0, tb=t["tb"], reverse=False)
    yb = _rwkv_scan(r, kk, k1, b1, v, lw1, tb=t["tb"], reverse=True)

    flat = lambda a: a.reshape(T, -1)
    x1 = _outproj(x2d, flat(attn), flat(yf), flat(yb), flat(bonus), flat(gate), row(lnx_w), row(lnx_b),
                  w_out.astype(BF16), row(norm_mix_post), tm=t["tm"])
    out = _ffn(x1, row(norm_ffn_pre), w_gate.astype(BF16), w_up.astype(BF16), w_down.astype(BF16),
               row(norm_ffn_post), tm=t["tm"], tf=t["tf"])
    return out.reshape(B, S, D)


def kernel(x_prompt, x_sample, norm_mix_pre, norm_mix_post, w_in, w_out, lambda_q1, lambda_k1, lambda_q2,
           lambda_k2, attn_subln, tshift_prev, tshift_next, w0, w2, a0, a2, g2, k_k, k_a, r_k, lnx_w, lnx_b,
           norm_ffn_pre, norm_ffn_post, w_gate, w_up, w_down):
    assert x_prompt.shape[1:] == x_sample.shape[1:], "both trunks share the sequence length"
    assert norm_mix_pre.shape[0] == 1, "single layer"
    nb = x_prompt.shape[0]
    x = jnp.concatenate([x_prompt, x_sample], axis=0)
    y = _encoder_layer(x, norm_mix_pre[0], norm_mix_post[0], w_in[0], w_out[0], lambda_q1[0], lambda_k1[0],
                       lambda_q2[0], lambda_k2[0], attn_subln[0], tshift_prev[0], tshift_next[0], w0[0], w2[0],
                       a0[0], a2[0], g2[0], k_k[0], k_a[0], r_k[0], lnx_w[0], lnx_b[0], norm_ffn_pre[0],
                       norm_ffn_post[0], w_gate[0], w_up[0], w_down[0])
    return (y[:nb], y[nb:])
```

```python
import functools
import math

import jax
import jax.numpy as jnp
from jax import lax
from jax.experimental import pallas as pl
from jax.experimental.pallas import tpu as pltpu

F32 = jnp.float32
BF16 = jnp.bfloat16

D_MODEL = 2048
ATTN_HEADS = 8
ATTN_VDIM = 128
ATTN_QKDIM = 64
ATTN_WIDTH = ATTN_HEADS * ATTN_VDIM
RWKV_HEAD = 64
RWKV_HEADS = 16
RWKV_WIDTH = RWKV_HEAD * RWKV_HEADS
N_DIR = 2
LORA = 64
GATE_LORA = 160
GATE_LORA_PAD = 256
QKV_COLS = 3 * ATTN_WIDTH
RWKV_COLS = 3 * RWKV_WIDTH + 2 * N_DIR * LORA + GATE_LORA
RWKV_COLS_PAD = 3 * RWKV_WIDTH + 2 * N_DIR * LORA + GATE_LORA_PAD
D_FF = 5632
NORM_EPS = 1e-6
LNX_EPS = 64e-5
LAMBDA_INIT = 0.8 - 0.6 * math.exp(-0.3 * 0)

LANES = 128
CHUNK = 64
PAIR = LANES // RWKV_HEAD


def _dot(a, b):
    return jnp.dot(a, b, preferred_element_type=F32)


def _dot_nt(a, b):
    return lax.dot_general(a, b, (((1,), (1,)), ((), ())), preferred_element_type=F32)


def _dot_tn(a, b):
    return lax.dot_general(a, b, (((0,), (0,)), ((), ())), preferred_element_type=F32)


def _split(x):
    hi = x.astype(BF16)
    lo = (x - hi.astype(F32)).astype(BF16)
    return hi, lo


def _dot3(a, b):
    ah, al = _split(a)
    bh, bl = _split(b)
    return _dot(ah, bh) + _dot(al, bh) + _dot(ah, bl)


def _sigmoid(x):
    return 1.0 / (1.0 + jnp.exp(-x))


def _head_ones(width):
    r = lax.broadcasted_iota(jnp.int32, (width, width), 0) // RWKV_HEAD
    c = lax.broadcasted_iota(jnp.int32, (width, width), 1) // RWKV_HEAD
    return jnp.where(r == c, 1.0, 0.0).astype(BF16)


def _head_sum(x, ones):
    outs = []
    for g in range(x.shape[1] // LANES):
        hi, lo = _split(x[:, g * LANES:(g + 1) * LANES])
        outs.append(_dot(hi, ones) + _dot(lo, ones))
    return jnp.concatenate(outs, axis=1)


def _inproj_kernel(x_ref, g_ref, w_ref, qkv_ref, z_ref, h_ref, *, n_qkv_tiles):
    n = pl.program_id(1)

    @pl.when(n == 0)
    def _():
        x = x_ref[...]
        ms = jnp.mean(x * x, axis=-1, keepdims=True)
        h_ref[...] = (x * lax.rsqrt(ms + NORM_EPS) * g_ref[...]).astype(BF16)

    acc = _dot(h_ref[...], w_ref[...])

    @pl.when(n < n_qkv_tiles)
    def _():
        qkv_ref[...] = acc.astype(BF16)

    @pl.when(n >= n_qkv_tiles)
    def _():
        z_ref[...] = acc


def _inproj(x2d, gain, w_in_bf16, *, tm, tn):
    T = x2d.shape[0]
    n_cols = w_in_bf16.shape[1]
    nq = QKV_COLS // tn
    grid = (T // tm, n_cols // tn)
    return pl.pallas_call(
        functools.partial(_inproj_kernel, n_qkv_tiles=nq),
        out_shape=(jax.ShapeDtypeStruct((T, QKV_COLS), BF16),
                   jax.ShapeDtypeStruct((T, RWKV_COLS_PAD), F32)),
        grid=grid,
        in_specs=[pl.BlockSpec((tm, D_MODEL), lambda m, n: (m, 0)),
                  pl.BlockSpec((1, D_MODEL), lambda m, n: (0, 0)),
                  pl.BlockSpec((D_MODEL, tn), lambda m, n: (0, n))],
        out_specs=(pl.BlockSpec((tm, tn), lambda m, n: (m, jnp.minimum(n, nq - 1))),
                   pl.BlockSpec((tm, tn), lambda m, n: (m, jnp.maximum(n - nq, 0)))),
        scratch_shapes=[pltpu.VMEM((tm, D_MODEL), BF16)],
        compiler_params=pltpu.CompilerParams(dimension_semantics=("parallel", "arbitrary")),
        name="inproj",
    )(x2d, gain, w_in_bf16)


LOG2E = math.log2(math.e)
Q_PRESCALE = ATTN_QKDIM ** -0.5 * LOG2E
AUX = 6
NEG_BIG = -1e30
SKIP_BITS = 80.0
NORM_SLACK = 1.01


def _pos_aux(slope2, n, first, sign_pos, sign_one):
    row = lax.broadcasted_iota(jnp.int32, (n, LANES), 0).astype(F32)
    lane = lax.broadcasted_iota(jnp.int32, (n, LANES), 1)
    val = slope2 * row * sign_pos
    hi = val.astype(BF16).astype(F32)
    mid = (val - hi).astype(BF16).astype(F32)
    lo = (val - hi - mid).astype(BF16).astype(F32)
    out = jnp.where(lane == first, hi, 0.0)
    out = jnp.where(lane == first + 1, mid, out)
    out = jnp.where(lane == first + 2, lo, out)
    out = jnp.where((lane >= first + 3) & (lane < first + AUX), sign_one, out)
    return out.astype(BF16)


def _attn_kernel(slopes_ref, lam_ref, sub_ref, q_ref, k_ref, v_ref, o_ref,
                 ka_ref, vt_ref, kn_ref, m_ref, l_ref, acc_ref, s_ref, p_ref, al_ref, *, t, seq):
    h = pl.program_id(1)
    qi = pl.program_id(2)
    nt = seq // t
    slope2 = slopes_ref[h] * LOG2E
    lane = lax.broadcasted_iota(jnp.int32, (t, LANES), 1)
    own = (lane < ATTN_QKDIM, lane >= ATTN_QKDIM)
    aux0 = (ATTN_QKDIM, 0)

    @pl.when(qi == 0)
    def _():
        ak = [_pos_aux(slope2, t, aux0[c], 1.0, 1.0) for c in range(2)]

        def build(j, kn):
            off = pl.multiple_of(j * t, t)
            kt = k_ref[0, pl.ds(off, t), :]
            for c in range(2):
                ka_ref[c, j] = jnp.where(own[c], kt, ak[c])
            vt_ref[j] = v_ref[0, pl.ds(off, t), :].astype(F32).T.astype(BF16)
            k32 = kt.astype(F32)
            ksq = k32 * k32
            rows = jnp.maximum(jnp.sum(jnp.where(own[0], ksq, 0.0), axis=1, keepdims=True),
                               jnp.sum(jnp.where(own[1], ksq, 0.0), axis=1, keepdims=True))
            return jnp.maximum(kn, jnp.max(rows, axis=0, keepdims=True))

        kn_ref[...] = lax.fori_loop(0, nt, build, jnp.zeros((1, 1), F32))

    q = q_ref[0]
    zero = jnp.zeros((t, LANES), BF16)
    q_var = {}
    for c in range(2):
        a = aux0[c]
        lane_pos = (lane >= a + 3) & (lane < a + AUX)
        lane_one = (lane >= a) & (lane < a + 3)
        pos = _pos_aux(slope2, t, a + 3, 1.0, 0.0)
        left = jnp.where(lane_one, 1.0, jnp.where(lane_pos, -pos.astype(F32), 0.0)).astype(BF16)
        right = jnp.where(lane_one, -1.0, jnp.where(lane_pos, pos.astype(F32), 0.0)).astype(BF16)
        q_var[c] = (jnp.where(own[c], q, left), jnp.where(own[c], q, zero), jnp.where(own[c], q, right))

    rel = (lax.broadcasted_iota(jnp.int32, (t, t), 0) - lax.broadcasted_iota(jnp.int32, (t, t), 1)).astype(F32)
    bias_diag = -slope2 * jnp.abs(rel)

    m_ref[...] = jnp.full(m_ref.shape, NEG_BIG, F32)
    l_ref[...] = jnp.zeros(l_ref.shape, F32)
    acc_ref[...] = jnp.zeros(acc_ref.shape, F32)

    def softmax_update(c, s, cst):
        m_old = m_ref[c]
        m_new = jnp.maximum(m_old, jnp.max(s, axis=0, keepdims=True) + cst)
        alpha = jnp.exp2(m_old - m_new)
        p = jnp.exp2(s - (m_new - cst))
        l_ref[c] = alpha * l_ref[c] + jnp.sum(p, axis=0, keepdims=True)
        m_ref[c] = m_new
        return alpha, p.astype(BF16)

    for c in range(2):
        s = _dot_nt(ka_ref[c, qi], q_var[c][1]) + bias_diag
        alpha, p = softmax_update(c, s, 0.0)
        acc_ref[c] = alpha * acc_ref[c] + _dot(vt_ref[qi], p)

    q32 = q.astype(F32)
    qsq = q32 * q32
    qn = jnp.maximum(jnp.sum(jnp.where(own[0], qsq, 0.0), axis=1, keepdims=True),
                     jnp.sum(jnp.where(own[1], qsq, 0.0), axis=1, keepdims=True))
    qk_bound = 2.0 * NORM_SLACK * jnp.sqrt(jnp.max(qn, axis=0, keepdims=True) * kn_ref[...])
    dist_needed = (qk_bound + SKIP_BITS) / slope2
    reach = jnp.clip(jnp.ceil((dist_needed - 1.0) / t), 0.0, nt - 1.0)[0, 0].astype(jnp.int32)
    n_left = jnp.minimum(qi, reach)
    n_off = n_left + jnp.minimum(nt - 1 - qi, reach)

    def key_tile(n):
        return jnp.clip(jnp.where(n < n_left, qi - 1 - n, qi + 1 + n - n_left), 0, nt - 1)

    def stage_scores(n, slot):
        j = key_tile(n)
        for c in range(2):
            qsel = jnp.where(j < qi, q_var[c][0], q_var[c][2])
            s_ref[slot, c] = _dot_nt(ka_ref[c, j], qsel)

    def stage_softmax(n, slot):
        j = key_tile(n)
        cst = jnp.where(n < n_off, -slope2 * (jnp.abs(qi - j) * t).astype(F32), NEG_BIG)
        for c in range(2):
            alpha, p = softmax_update(c, s_ref[slot, c], cst)
            p_ref[slot, c] = p
            al_ref[slot, c] = alpha

    def stage_values(n, slot):
        j = key_tile(n)
        for c in range(2):
            acc_ref[c] = al_ref[slot, c] * acc_ref[c] + _dot(vt_ref[j], p_ref[slot, c])

    stage_scores(0, 0)
    stage_softmax(0, 0)
    stage_scores(1, 1)

    def pair_body(mi, carry):
        n0 = 2 + 2 * mi
        stage_values(n0 - 2, 0)
        stage_softmax(n0 - 1, 1)
        stage_scores(n0, 0)
        stage_values(n0 - 1, 1)
        stage_softmax(n0, 0)
        stage_scores(n0 + 1, 1)
        return carry

    lax.fori_loop(0, (n_off + 1) // 2, pair_body, 0)

    lp = lam_ref[...]
    lam = (jnp.exp(jnp.sum(lp[0:1] * lp[1:2], axis=-1, keepdims=True))
           - jnp.exp(jnp.sum(lp[2:3] * lp[3:4], axis=-1, keepdims=True)) + LAMBDA_INIT)
    o_t = acc_ref[0] / l_ref[0] - lam * (acc_ref[1] / l_ref[1])
    o = o_t.T
    ms = jnp.mean(o * o, axis=-1, keepdims=True)
    o = o * lax.rsqrt(ms + NORM_EPS) * sub_ref[...] * (1.0 - LAMBDA_INIT)
    o_ref[0] = o.astype(o_ref.dtype)


def _diff_attention(qkv, lam_params, subln, *, t):
    B, S, _ = qkv.shape
    nt = S // t
    slopes = jnp.asarray([2.0 ** (-8.0 * (h + 1.0) / ATTN_HEADS) for h in range(ATTN_HEADS)], F32)
    grid_spec = pltpu.PrefetchScalarGridSpec(
        num_scalar_prefetch=1,
        grid=(B, ATTN_HEADS, nt),
        in_specs=[pl.BlockSpec((4, ATTN_QKDIM), lambda b, h, i, s: (0, 0)),
                  pl.BlockSpec((1, ATTN_VDIM), lambda b, h, i, s: (0, 0)),
                  pl.BlockSpec((1, t, LANES), lambda b, h, i, s: (b, i, h)),
                  pl.BlockSpec((1, S, LANES), lambda b, h, i, s: (b, 0, ATTN_HEADS + h)),
                  pl.BlockSpec((1, S, LANES), lambda b, h, i, s: (b, 0, 2 * ATTN_HEADS + h))],
        out_specs=pl.BlockSpec((1, t, LANES), lambda b, h, i, s: (b, i, h)),
        scratch_shapes=[pltpu.VMEM((2, nt, t, LANES), BF16),
                        pltpu.VMEM((nt, ATTN_VDIM, t), BF16),
                        pltpu.VMEM((1, 1), F32),
                        pltpu.VMEM((2, 1, t), F32), pltpu.VMEM((2, 1, t), F32),
                        pltpu.VMEM((2, ATTN_VDIM, t), F32),
                        pltpu.VMEM((2, 2, t, t), F32),
                        pltpu.VMEM((2, 2, t, t), BF16),
                        pltpu.VMEM((2, 2, 1, t), F32)],
    )
    return pl.pallas_call(
        functools.partial(_attn_kernel, t=t, seq=S),
        out_shape=jax.ShapeDtypeStruct((B, S, ATTN_WIDTH), BF16),
        grid_spec=grid_spec,
        compiler_params=pltpu.CompilerParams(dimension_semantics=("parallel", "parallel", "arbitrary")),
        name="diff_attn",
    )(slopes, lam_params, subln, qkv, qkv, qkv)


def _prep_kernel(z_ref, zp_ref, zn_ref, tp_ref, tn_ref, w0_ref, w2_ref, a0_ref, a2_ref, g2_ref,
                 kk_ref, ka_ref, rk_ref,
                 r_o, v_o, kk_o, k0_o, k1_o, b0_o, b1_o, lw0_o, lw1_o, bonus_o, gate_o):
    i = pl.program_id(1)
    nt = pl.num_programs(1)
    z = z_ref[0]
    ts = z.shape[0]
    row = lax.broadcasted_iota(jnp.int32, (ts, 1), 0)
    prev_row = jnp.where(i > 0, zp_ref[0, 7:8, :], 0.0)
    next_row = jnp.where(i < nt - 1, zn_ref[0, 0:1, :], 0.0)
    z_prev = jnp.where(row == 0, prev_row, pltpu.roll(z, 1, 0))
    z_next = jnp.where(row == ts - 1, next_row, pltpu.roll(z, ts - 1, 0))
    zs = z + tp_ref[...] * (z_prev - z) + tn_ref[...] * (z_next - z)

    W = RWKV_WIDTH
    r = zs[:, 0:W]
    k = zs[:, W:2 * W]
    v = zs[:, 2 * W:3 * W]
    wd = jnp.tanh(zs[:, 3 * W:3 * W + LANES])
    ad = zs[:, 3 * W + LANES:3 * W + 2 * LANES]
    gd = zs[:, 3 * W + 2 * LANES:]

    w_log = w0_ref[...] + _dot3(wd, w2_ref[...])
    lw = -_sigmoid(w_log) * math.exp(-0.5)
    iclr = _sigmoid(a0_ref[...] + _dot3(ad, a2_ref[...]))
    gate = _dot3(_sigmoid(gd), g2_ref[...])

    ones = _head_ones(LANES)
    kk = k * kk_ref[...]
    kk = kk * lax.rsqrt(jnp.maximum(_head_sum(kk * kk, ones), 1e-24))
    ka = ka_ref[...]
    rk = rk_ref[...]
    bonus = jnp.zeros_like(r)
    k_outs = (k0_o, k1_o)
    b_outs = (b0_o, b1_o)
    lw_outs = (lw0_o, lw1_o)
    for d in range(N_DIR):
        a_d = iclr[:, d * W:(d + 1) * W]
        k_d = k * (1.0 + (a_d - 1.0) * ka)
        k_outs[d][0] = k_d
        b_outs[d][0] = kk * a_d
        lw_outs[d][0] = lw[:, d * W:(d + 1) * W]
        bonus = bonus + _head_sum(r * k_d * rk, ones) * v
    r_o[0] = r
    v_o[0] = v
    kk_o[0] = kk
    bonus_o[0] = bonus
    gate_o[0] = gate


def _rwkv_prep(z, tp, tn, w0, w2bd, a0, a2bd, g2p, k_k, k_a, r_k, *, ts):
    B, S, ZC = z.shape
    nt = S // ts
    hb = ts // 8
    last8 = S // 8 - 1
    const = lambda shape: pl.BlockSpec(shape, lambda b, i: (0, 0))
    out_sds = jax.ShapeDtypeStruct((B, S, RWKV_WIDTH), F32)
    out_spec = pl.BlockSpec((1, ts, RWKV_WIDTH), lambda b, i: (b, i, 0))
    return pl.pallas_call(
        _prep_kernel,
        out_shape=(out_sds,) * 11,
        grid=(B, nt),
        in_specs=[pl.BlockSpec((1, ts, ZC), lambda b, i: (b, i, 0)),
                  pl.BlockSpec((1, 8, ZC), lambda b, i: (b, jnp.maximum(i * hb - 1, 0), 0)),
                  pl.BlockSpec((1, 8, ZC), lambda b, i: (b, jnp.minimum((i + 1) * hb, last8), 0)),
                  const((1, ZC)), const((1, ZC)),
                  const((1, N_DIR * RWKV_WIDTH)), const((LANES, N_DIR * RWKV_WIDTH)),
                  const((1, N_DIR * RWKV_WIDTH)), const((LANES, N_DIR * RWKV_WIDTH)),
                  const((GATE_LORA_PAD, RWKV_WIDTH)),
                  const((1, RWKV_WIDTH)), const((1, RWKV_WIDTH)), const((1, RWKV_WIDTH))],
        out_specs=(out_spec,) * 11,
        compiler_params=pltpu.CompilerParams(dimension_semantics=("parallel", "arbitrary")),
        name="rwkv_prep",
    )(z, z, z, tp, tn, w0, w2bd, a0, a2bd, g2p, k_k, k_a, r_k)


def _scan_kernel(r_ref, kk_ref, k_ref, b_ref, v_ref, lw_ref, y_ref, s_ref, *, tb, reverse):
    i = pl.program_id(2)
    C = CHUNK
    W = LANES
    nc = tb // C

    @pl.when(i == 0)
    def _():
        s_ref[...] = jnp.zeros_like(s_ref)

    rt = lax.broadcasted_iota(jnp.int32, (tb, tb), 0)
    ct = lax.broadcasted_iota(jnp.int32, (tb, tb), 1)
    same = (rt // C) == (ct // C)
    before = (ct >= rt) if reverse else (ct <= rt)
    tri_in = jnp.where(same & before, 1.0, 0.0).astype(BF16)
    tri_out = jnp.where(same & jnp.logical_not(before), 1.0, 0.0).astype(BF16)
    lw = lw_ref[0]
    lw_hi, lw_lo = _split(lw)
    c = _dot(tri_in, lw_hi) + _dot(tri_in, lw_lo)
    d = _dot(tri_out, lw_hi) + _dot(tri_out, lw_lo)

    e_c = jnp.exp(c)
    e_nc = jnp.exp(-c)
    e_d = jnp.exp(d)
    at_all = -kk_ref[0] * jnp.exp(c - lw)
    rt_all = r_ref[0] * e_c
    bt_all = b_ref[0] * e_nc
    kt_all = k_ref[0] * e_nc
    bh_all = b_ref[0] * e_d
    kh_all = k_ref[0] * e_d
    v_all = v_ref[0]

    r2 = lax.broadcasted_iota(jnp.int32, (W, W), 0)
    c2 = lax.broadcasted_iota(jnp.int32, (W, W), 1)
    bd_mask = (r2 // C) == (c2 // RWKV_HEAD)
    eye = r2 == c2
    tr = lax.broadcasted_iota(jnp.int32, (C, W), 0)
    sc = lax.broadcasted_iota(jnp.int32, (C, W), 1) % C
    strict = (sc > tr) if reverse else (sc < tr)
    incl = (sc >= tr) if reverse else (sc <= tr)

    def bd(x):
        return jnp.where(bd_mask, jnp.concatenate([x] * PAIR, axis=0), 0.0).astype(BF16)

    def bd2(xa, xb):
        return jnp.concatenate([bd(xa), bd(xb)], axis=1)

    order = [(nc - 1 - jj) if reverse else jj for jj in range(nc)]
    sls = [slice(j * C, (j + 1) * C) for j in order]
    ends = [j * C if reverse else (j + 1) * C - 1 for j in order]
    At = [at_all[sl] for sl in sls]
    Rt = [rt_all[sl] for sl in sls]
    V = [v_all[sl] for sl in sls]
    bdV = [bd(x) for x in V]
    lhs = [jnp.concatenate([a, r], axis=0).astype(BF16) for a, r in zip(At, Rt)]
    sb = [_dot_nt(l, bd(bt_all[sl])) for l, sl in zip(lhs, sls)]
    sk = [_dot_nt(l, bd(kt_all[sl])) for l, sl in zip(lhs, sls)]
    lp = [jnp.where(strict, x[:C], 0.0) for x in sb]
    Lak = [jnp.where(strict, x[:C], 0.0).astype(BF16) for x in sk]
    Mrb = [jnp.where(incl, x[C:], 0.0).astype(BF16) for x in sb]
    Mrk = [jnp.where(incl, x[C:], 0.0).astype(BF16) for x in sk]
    xa = list(At)
    xu = [_dot(l, b) for l, b in zip(Lak, bdV)]
    n_dbl = C.bit_length() - 1
    for it in range(n_dbl):
        lp16 = [x.astype(BF16) for x in lp]
        upd = [_dot(l, bd2(a, u)) for l, a, u in zip(lp16, xa, xu)]
        xa = [a + x[:, :W] for a, x in zip(xa, upd)]
        xu = [u + x[:, W:] for u, x in zip(xu, upd)]
        if it + 1 < n_dbl:
            lp = [_dot(l16, bd(l)) for l16, l in zip(lp16, lp)]
    rb = [_dot(m, bd2(a, u)) for m, a, u in zip(Mrb, xa, xu)]
    Rp = [(r + x[:, :W]).astype(BF16) for r, x in zip(Rt, rb)]
    Y0 = [x[:, W:] + _dot(m, b) for x, m, b in zip(rb, Mrk, bdV)]
    au = [jnp.concatenate([a, u], axis=1).astype(BF16) for a, u in zip(xa, xu)]
    pb = [_dot_tn(bh_all[sl].astype(BF16), x) for sl, x in zip(sls, au)]
    kv = [_dot_tn(kh_all[sl].astype(BF16), x.astype(BF16)) for sl, x in zip(sls, V)]
    Pbd = [(jnp.where(bd_mask, x[:, :W], 0.0) + jnp.where(eye, e_c[e:e + 1, :], 0.0)).astype(BF16)
           for x, e in zip(pb, ends)]
    Qbd = [jnp.where(bd_mask, x[:, W:] + y, 0.0) for x, y in zip(pb, kv)]

    s = s_ref[...]
    for n in range(nc):
        s16 = s.astype(BF16)
        y_ref[0, sls[n], :] = _dot(Rp[n], s16) + Y0[n]
        s = _dot(Pbd[n], s16) + Qbd[n]
    s_ref[...] = s


def _rwkv_scan(r, kk, k_d, b_d, v, lw_d, *, tb, reverse):
    B, S, Wd = r.shape
    nblk = S // tb
    if reverse:
        idx = lambda b, p, i: (b, nblk - 1 - i, p)
    else:
        idx = lambda b, p, i: (b, i, p)
    spec = pl.BlockSpec((1, tb, LANES), idx)
    return pl.pallas_call(
        functools.partial(_scan_kernel, tb=tb, reverse=reverse),
        out_shape=jax.ShapeDtypeStruct((B, S, Wd), F32),
        grid=(B, Wd // LANES, nblk),
        in_specs=[spec] * 6,
        out_specs=spec,
        scratch_shapes=[pltpu.VMEM((LANES, LANES), F32)],
        compiler_params=pltpu.CompilerParams(dimension_semantics=("parallel", "parallel", "arbitrary")),
        name="rwkv_scan_bwd" if reverse else "rwkv_scan_fwd",
    )(r, kk, k_d, b_d, v, lw_d)


def _outproj_kernel(x_ref, attn_ref, yf_ref, yb_ref, bonus_ref, gate_ref, lw_ref, lb_ref, w_ref, g_ref, o_ref):
    y = yf_ref[...] + yb_ref[...]
    mean_mat = _head_ones(LANES)
    inv_n = 1.0 / RWKV_HEAD
    mu = _head_sum(y, mean_mat) * inv_n
    yc = y - mu
    var = _head_sum(yc * yc, mean_mat) * inv_n
    yn = yc * lax.rsqrt(var + LNX_EPS) * lw_ref[...] + lb_ref[...]
    rw = ((yn + bonus_ref[...]) * gate_ref[...]).astype(BF16)
    m = _dot(attn_ref[...], w_ref[0:ATTN_WIDTH, :]) + _dot(rw, w_ref[ATTN_WIDTH:, :])
    ms = jnp.mean(m * m, axis=-1, keepdims=True)
    o_ref[...] = x_ref[...] + m * lax.rsqrt(ms + NORM_EPS) * g_ref[...]


def _outproj(x2d, attn2d, yf, yb, bonus, gate, lnx_w, lnx_b, w_out_bf16, gain, *, tm):
    T = x2d.shape[0]
    row = lambda w: pl.BlockSpec((tm, w), lambda m: (m, 0))
    const = lambda shape: pl.BlockSpec(shape, lambda m: (0, 0))
    return pl.pallas_call(
        _outproj_kernel,
        out_shape=jax.ShapeDtypeStruct((T, D_MODEL), F32),
        grid=(T // tm,),
        in_specs=[row(D_MODEL), row(ATTN_WIDTH), row(RWKV_WIDTH), row(RWKV_WIDTH), row(RWKV_WIDTH), row(RWKV_WIDTH),
                  const((1, RWKV_WIDTH)), const((1, RWKV_WIDTH)),
                  const((D_MODEL, D_MODEL)), const((1, D_MODEL))],
        out_specs=row(D_MODEL),
        compiler_params=pltpu.CompilerParams(dimension_semantics=("parallel",)),
        name="outproj",
    )(x2d, attn2d, yf, yb, bonus, gate, lnx_w, lnx_b, w_out_bf16, gain)


def _ffn_kernel(x_ref, gpre_ref, wg_ref, wu_ref, wd_ref, gpost_ref, o_ref, h_ref, acc_ref):
    j = pl.program_id(1)

    @pl.when(j == 0)
    def _():
        x = x_ref[...]
        ms = jnp.mean(x * x, axis=-1, keepdims=True)
        h_ref[...] = (x * lax.rsqrt(ms + NORM_EPS) * gpre_ref[...]).astype(BF16)
        acc_ref[...] = jnp.zeros_like(acc_ref)

    h = h_ref[...]
    g = _dot(h, wg_ref[...])
    u = _dot(h, wu_ref[...])
    a = (g * _sigmoid(g) * u).astype(BF16)
    acc_ref[...] += _dot(a, wd_ref[...])

    @pl.when(j == pl.num_programs(1) - 1)
    def _():
        f = acc_ref[...]
        ms = jnp.mean(f * f, axis=-1, keepdims=True)
        o_ref[...] = x_ref[...] + f * lax.rsqrt(ms + NORM_EPS) * gpost_ref[...]


def _ffn(x2d, gpre, wg, wu, wd, gpost, *, tm, tf):
    T = x2d.shape[0]
    return pl.pallas_call(
        _ffn_kernel,
        out_shape=jax.ShapeDtypeStruct((T, D_MODEL), F32),
        grid=(T // tm, D_FF // tf),
        in_specs=[pl.BlockSpec((tm, D_MODEL), lambda m, j: (m, 0)),
                  pl.BlockSpec((1, D_MODEL), lambda m, j: (0, 0)),
                  pl.BlockSpec((D_MODEL, tf), lambda m, j: (0, j)),
                  pl.BlockSpec((D_MODEL, tf), lambda m, j: (0, j)),
                  pl.BlockSpec((tf, D_MODEL), lambda m, j: (j, 0)),
                  pl.BlockSpec((1, D_MODEL), lambda m, j: (0, 0))],
        out_specs=pl.BlockSpec((tm, D_MODEL), lambda m, j: (m, 0)),
        scratch_shapes=[pltpu.VMEM((tm, D_MODEL), BF16), pltpu.VMEM((tm, D_MODEL), F32)],
        compiler_params=pltpu.CompilerParams(dimension_semantics=("parallel", "arbitrary")),
        name="ffn",
    )(x2d, gpre, wg, wu, wd, gpost)


def _tiles(S):
    return dict(tm=512, tn=512, ta=512, ts=256, tb=512, tf=512)


def _lora_blockdiag(w):
    zero = jnp.zeros_like(w[0])
    return jnp.concatenate([jnp.concatenate([w[0], zero], axis=1),
                            jnp.concatenate([zero, w[1]], axis=1)], axis=0)


def _encoder_layer(x, norm_mix_pre, norm_mix_post, w_in, w_out, lambda_q1, lambda_k1, lambda_q2, lambda_k2,
                   attn_subln, tshift_prev, tshift_next, w0, w2, a0, a2, g2, k_k, k_a, r_k, lnx_w, lnx_b,
                   norm_ffn_pre, norm_ffn_post, w_gate, w_up, w_down):
    B, S, D = x.shape
    T = B * S
    t = _tiles(S)
    row = lambda a: a.reshape(1, -1).astype(F32)
    pad_cols = RWKV_COLS_PAD - RWKV_COLS

    col_scale = jnp.where(jnp.arange(w_in.shape[1]) < ATTN_WIDTH, Q_PRESCALE, 1.0).astype(F32)
    w_in_p = jnp.pad(w_in * col_scale, ((0, 0), (0, pad_cols))).astype(BF16)
    x2d = x.reshape(T, D)
    qkv, z = _inproj(x2d, row(norm_mix_pre), w_in_p, tm=t["tm"], tn=t["tn"])

    lam_params = jnp.stack([lambda_q1, lambda_k1, lambda_q2, lambda_k2]).astype(F32)
    attn = _diff_attention(qkv.reshape(B, S, QKV_COLS), lam_params, row(attn_subln), t=t["ta"])

    tp = jnp.pad(tshift_prev, (0, pad_cols)).reshape(1, -1)
    tn_ = jnp.pad(tshift_next, (0, pad_cols)).reshape(1, -1)
    g2p = jnp.pad(g2, ((0, GATE_LORA_PAD - GATE_LORA), (0, 0)))
    (r, v, kk, k0, k1, b0, b1, lw0, lw1, bonus, gate) = _rwkv_prep(
        z.reshape(B, S, RWKV_COLS_PAD), tp, tn_, row(w0), _lora_blockdiag(w2), row(a0), _lora_blockdiag(a2), g2p,
        row(k_k), row(k_a), row(r_k), ts=t["ts"])
    yf = _rwkv_scan(r, kk, k0, b0, v, lw0, tb=t["tb"], reverse=False)
    yb = _rwkv_scan(r, kk, k1, b1, v, lw1, tb=t["tb"], reverse=True)

    flat = lambda a: a.reshape(T, -1)
    x1 = _outproj(x2d, flat(attn), flat(yf), flat(yb), flat(bonus), flat(gate), row(lnx_w), row(lnx_b),
                  w_out.astype(BF16), row(norm_mix_post), tm=t["tm"])
    out = _ffn(x1, row(norm_ffn_pre), w_gate.astype(BF16), w_up.astype(BF16), w_down.astype(BF16),
               row(norm_ffn_post), tm=t["tm"], tf=t["tf"])
    return out.reshape(B, S, D)


def kernel(x_prompt, x_sample, norm_mix_pre, norm_mix_post, w_in, w_out, lambda_q1, lambda_k1, lambda_q2,
           lambda_k2, attn_subln, tshift_prev, tshift_next, w0, w2, a0, a2, g2, k_k, k_a, r_k, lnx_w, lnx_b,
           norm_ffn_pre, norm_ffn_post, w_gate, w_up, w_down):
    assert x_prompt.shape[1:] == x_sample.shape[1:], "both trunks share the sequence length"
    assert norm_mix_pre.shape[0] == 1, "single layer"
    nb = x_prompt.shape[0]
    x = jnp.concatenate([x_prompt, x_sample], axis=0)
    y = _encoder_layer(x, norm_mix_pre[0], norm_mix_post[0], w_in[0], w_out[0], lambda_q1[0], lambda_k1[0],
                       lambda_q2[0], lambda_k2[0], attn_subln[0], tshift_prev[0], tshift_next[0], w0[0], w2[0],
                       a0[0], a2[0], g2[0], k_k[0], k_a[0], r_k[0], lnx_w[0], lnx_b[0], norm_ffn_pre[0],
                       norm_ffn_post[0], w_gate[0], w_up[0], w_down[0])
    return (y[:nb], y[nb:])
```

```python
import functools
import math

import jax
import jax.numpy as jnp
from jax import lax
from jax.experimental import pallas as pl
from jax.experimental.pallas import tpu as pltpu

F32 = jnp.float32
BF16 = jnp.bfloat16

D_MODEL = 2048
ATTN_HEADS = 8
ATTN_VDIM = 128
ATTN_QKDIM = 64
ATTN_WIDTH = ATTN_HEADS * ATTN_VDIM
RWKV_HEAD = 64
RWKV_HEADS = 16
RWKV_WIDTH = RWKV_HEAD * RWKV_HEADS
N_DIR = 2
LORA = 64
GATE_LORA = 160
GATE_LORA_PAD = 256
QKV_COLS = 3 * ATTN_WIDTH
RWKV_COLS = 3 * RWKV_WIDTH + 2 * N_DIR * LORA + GATE_LORA
RWKV_COLS_PAD = 3 * RWKV_WIDTH + 2 * N_DIR * LORA + GATE_LORA_PAD
D_FF = 5632
NORM_EPS = 1e-6
LNX_EPS = 64e-5
LAMBDA_INIT = 0.8 - 0.6 * math.exp(-0.3 * 0)

LANES = 128
CHUNK = 64
PAIR = LANES // RWKV_HEAD


def _dot(a, b):
    return jnp.dot(a, b, preferred_element_type=F32)


def _dot_nt(a, b):
    return lax.dot_general(a, b, (((1,), (1,)), ((), ())), preferred_element_type=F32)


def _dot_tn(a, b):
    return lax.dot_general(a, b, (((0,), (0,)), ((), ())), preferred_element_type=F32)


def _split(x):
    hi = x.astype(BF16)
    lo = (x - hi.astype(F32)).astype(BF16)
    return hi, lo


def _dot3(a, b):
    ah, al = _split(a)
    bh, bl = _split(b)
    return _dot(ah, bh) + _dot(al, bh) + _dot(ah, bl)


def _sigmoid(x):
    return 1.0 / (1.0 + jnp.exp(-x))


def _head_ones(width):
    r = lax.broadcasted_iota(jnp.int32, (width, width), 0) // RWKV_HEAD
    c = lax.broadcasted_iota(jnp.int32, (width, width), 1) // RWKV_HEAD
    return jnp.where(r == c, 1.0, 0.0).astype(BF16)


def _head_sum(x, ones):
    outs = []
    for g in range(x.shape[1] // LANES):
        hi, lo = _split(x[:, g * LANES:(g + 1) * LANES])
        outs.append(_dot(hi, ones) + _dot(lo, ones))
    return jnp.concatenate(outs, axis=1)


def _inproj_kernel(x_ref, g_ref, w_ref, qkv_ref, z_ref, h_ref, *, n_qkv_tiles):
    n = pl.program_id(1)

    @pl.when(n == 0)
    def _():
        x = x_ref[...]
        ms = jnp.mean(x * x, axis=-1, keepdims=True)
        h_ref[...] = (x * lax.rsqrt(ms + NORM_EPS) * g_ref[...]).astype(BF16)

    acc = _dot(h_ref[...], w_ref[...])

    @pl.when(n < n_qkv_tiles)
    def _():
        qkv_ref[...] = acc.astype(BF16)

    @pl.when(n >= n_qkv_tiles)
    def _():
        z_ref[...] = acc


def _inproj(x2d, gain, w_in_bf16, *, tm, tn):
    T = x2d.shape[0]
    n_cols = w_in_bf16.shape[1]
    nq = QKV_COLS // tn
    grid = (T // tm, n_cols // tn)
    return pl.pallas_call(
        functools.partial(_inproj_kernel, n_qkv_tiles=nq),
        out_shape=(jax.ShapeDtypeStruct((T, QKV_COLS), BF16),
                   jax.ShapeDtypeStruct((T, RWKV_COLS_PAD), F32)),
        grid=grid,
        in_specs=[pl.BlockSpec((tm, D_MODEL), lambda m, n: (m, 0)),
                  pl.BlockSpec((1, D_MODEL), lambda m, n: (0, 0)),
                  pl.BlockSpec((D_MODEL, tn), lambda m, n: (0, n))],
        out_specs=(pl.BlockSpec((tm, tn), lambda m, n: (m, jnp.minimum(n, nq - 1))),
                   pl.BlockSpec((tm, tn), lambda m, n: (m, jnp.maximum(n - nq, 0)))),
        scratch_shapes=[pltpu.VMEM((tm, D_MODEL), BF16)],
        compiler_params=pltpu.CompilerParams(dimension_semantics=("parallel", "arbitrary")),
        name="inproj",
    )(x2d, gain, w_in_bf16)


LOG2E = math.log2(math.e)
Q_PRESCALE = ATTN_QKDIM ** -0.5 * LOG2E
AUX = 6
NEG_BIG = -1e30
SKIP_BITS = 80.0
NORM_SLACK = 1.01


def _pos_aux(slope2, n, first, sign_pos, sign_one):
    row = lax.broadcasted_iota(jnp.int32, (n, LANES), 0).astype(F32)
    lane = lax.broadcasted_iota(jnp.int32, (n, LANES), 1)
    val = slope2 * row * sign_pos
    hi = val.astype(BF16).astype(F32)
    mid = (val - hi).astype(BF16).astype(F32)
    lo = (val - hi - mid).astype(BF16).astype(F32)
    out = jnp.where(lane == first, hi, 0.0)
    out = jnp.where(lane == first + 1, mid, out)
    out = jnp.where(lane == first + 2, lo, out)
    out = jnp.where((lane >= first + 3) & (lane < first + AUX), sign_one, out)
    return out.astype(BF16)


def _attn_kernel(slopes_ref, lam_ref, sub_ref, q_ref, k_ref, v_ref, o_ref,
                 ka_ref, vt_ref, kn_ref, m_ref, l_ref, acc_ref, s_ref, p_ref, al_ref, *, t, seq):
    h = pl.program_id(1)
    qi = pl.program_id(2)
    nt = seq // t
    slope2 = slopes_ref[h] * LOG2E
    lane = lax.broadcasted_iota(jnp.int32, (t, LANES), 1)
    own = (lane < ATTN_QKDIM, lane >= ATTN_QKDIM)
    aux0 = (ATTN_QKDIM, 0)

    @pl.when(qi == 0)
    def _():
        ak = [_pos_aux(slope2, t, aux0[c], 1.0, 1.0) for c in range(2)]

        def build(j, kn):
            off = pl.multiple_of(j * t, t)
            kt = k_ref[0, pl.ds(off, t), :]
            for c in range(2):
                ka_ref[c, j] = jnp.where(own[c], kt, ak[c])
            vt_ref[j] = v_ref[0, pl.ds(off, t), :].astype(F32).T.astype(BF16)
            k32 = kt.astype(F32)
            ksq = k32 * k32
            rows = jnp.maximum(jnp.sum(jnp.where(own[0], ksq, 0.0), axis=1, keepdims=True),
                               jnp.sum(jnp.where(own[1], ksq, 0.0), axis=1, keepdims=True))
            return jnp.maximum(kn, jnp.max(rows, axis=0, keepdims=True))

        kn_ref[...] = lax.fori_loop(0, nt, build, jnp.zeros((1, 1), F32))

    q = q_ref[0]
    zero = jnp.zeros((t, LANES), BF16)
    q_var = {}
    for c in range(2):
        a = aux0[c]
        lane_pos = (lane >= a + 3) & (lane < a + AUX)
        lane_one = (lane >= a) & (lane < a + 3)
        pos = _pos_aux(slope2, t, a + 3, 1.0, 0.0)
        left = jnp.where(lane_one, 1.0, jnp.where(lane_pos, -pos.astype(F32), 0.0)).astype(BF16)
        right = jnp.where(lane_one, -1.0, jnp.where(lane_pos, pos.astype(F32), 0.0)).astype(BF16)
        q_var[c] = (jnp.where(own[c], q, left), jnp.where(own[c], q, zero), jnp.where(own[c], q, right))

    rel = (lax.broadcasted_iota(jnp.int32, (t, t), 0) - lax.broadcasted_iota(jnp.int32, (t, t), 1)).astype(F32)
    bias_diag = -slope2 * jnp.abs(rel)

    m_ref[...] = jnp.full(m_ref.shape, NEG_BIG, F32)
    l_ref[...] = jnp.zeros(l_ref.shape, F32)
    acc_ref[...] = jnp.zeros(acc_ref.shape, F32)

    def softmax_update(c, s, cst):
        m_old = m_ref[c]
        m_new = jnp.maximum(m_old, jnp.max(s, axis=0, keepdims=True) + cst)
        alpha = jnp.exp2(m_old - m_new)
        p = jnp.exp2(s - (m_new - cst))
        l_ref[c] = alpha * l_ref[c] + jnp.sum(p, axis=0, keepdims=True)
        m_ref[c] = m_new
        return alpha, p.astype(BF16)

    for c in range(2):
        s = _dot_nt(ka_ref[c, qi], q_var[c][1]) + bias_diag
        alpha, p = softmax_update(c, s, 0.0)
        acc_ref[c] = alpha * acc_ref[c] + _dot(vt_ref[qi], p)

    q32 = q.astype(F32)
    qsq = q32 * q32
    qn = jnp.maximum(jnp.sum(jnp.where(own[0], qsq, 0.0), axis=1, keepdims=True),
                     jnp.sum(jnp.where(own[1], qsq, 0.0), axis=1, keepdims=True))
    qk_bound = 2.0 * NORM_SLACK * jnp.sqrt(jnp.max(qn, axis=0, keepdims=True) * kn_ref[...])
    dist_needed = (qk_bound + SKIP_BITS) / slope2
    reach = jnp.clip(jnp.ceil((dist_needed - 1.0) / t), 0.0, nt - 1.0)[0, 0].astype(jnp.int32)
    n_left = jnp.minimum(qi, reach)
    n_off = n_left + jnp.minimum(nt - 1 - qi, reach)

    def key_tile(n):
        return jnp.clip(jnp.where(n < n_left, qi - 1 - n, qi + 1 + n - n_left), 0, nt - 1)

    def stage_scores(n, slot):
        j = key_tile(n)
        for c in range(2):
            qsel = jnp.where(j < qi, q_var[c][0], q_var[c][2])
            s_ref[slot, c] = _dot_nt(ka_ref[c, j], qsel)

    def stage_softmax(n, slot):
        j = key_tile(n)
        cst = jnp.where(n < n_off, -slope2 * (jnp.abs(qi - j) * t).astype(F32), NEG_BIG)
        for c in range(2):
            alpha, p = softmax_update(c, s_ref[slot, c], cst)
            p_ref[slot, c] = p
            al_ref[slot, c] = alpha

    def stage_values(n, slot):
        j = key_tile(n)
        for c in range(2):
            acc_ref[c] = al_ref[slot, c] * acc_ref[c] + _dot(vt_ref[j], p_ref[slot, c])

    stage_scores(0, 0)
    stage_softmax(0, 0)
    stage_scores(1, 1)

    def pair_body(mi, carry):
        n0 = 2 + 2 * mi
        stage_values(n0 - 2, 0)
        stage_softmax(n0 - 1, 1)
        stage_scores(n0, 0)
        stage_values(n0 - 1, 1)
        stage_softmax(n0, 0)
        stage_scores(n0 + 1, 1)
        return carry

    lax.fori_loop(0, (n_off + 1) // 2, pair_body, 0)

    lp = lam_ref[...]
    lam = (jnp.exp(jnp.sum(lp[0:1] * lp[1:2], axis=-1, keepdims=True))
           - jnp.exp(jnp.sum(lp[2:3] * lp[3:4], axis=-1, keepdims=True)) + LAMBDA_INIT)
    o_t = acc_ref[0] / l_ref[0] - lam * (acc_ref[1] / l_ref[1])
    o = o_t.T
    ms = jnp.mean(o * o, axis=-1, keepdims=True)
    o = o * lax.rsqrt(ms + NORM_EPS) * sub_ref[...] * (1.0 - LAMBDA_INIT)
    o_ref[0] = o.astype(o_ref.dtype)


def _diff_attention(qkv, lam_params, subln, *, t):
    B, S, _ = qkv.shape
    nt = S // t
    slopes = jnp.asarray([2.0 ** (-8.0 * (h + 1.0) / ATTN_HEADS) for h in range(ATTN_HEADS)], F32)
    grid_spec = pltpu.PrefetchScalarGridSpec(
        num_scalar_prefetch=1,
        grid=(B, ATTN_HEADS, nt),
        in_specs=[pl.BlockSpec((4, ATTN_QKDIM), lambda b, h, i, s: (0, 0)),
                  pl.BlockSpec((1, ATTN_VDIM), lambda b, h, i, s: (0, 0)),
                  pl.BlockSpec((1, t, LANES), lambda b, h, i, s: (b, i, h)),
                  pl.BlockSpec((1, S, LANES), lambda b, h, i, s: (b, 0, ATTN_HEADS + h)),
                  pl.BlockSpec((1, S, LANES), lambda b, h, i, s: (b, 0, 2 * ATTN_HEADS + h))],
        out_specs=pl.BlockSpec((1, t, LANES), lambda b, h, i, s: (b, i, h)),
        scratch_shapes=[pltpu.VMEM((2, nt, t, LANES), BF16),
                        pltpu.VMEM((nt, ATTN_VDIM, t), BF16),
                        pltpu.VMEM((1, 1), F32),
                        pltpu.VMEM((2, 1, t), F32), pltpu.VMEM((2, 1, t), F32),
                        pltpu.VMEM((2, ATTN_VDIM, t), F32),
                        pltpu.VMEM((2, 2, t, t), F32),
                        pltpu.VMEM((2, 2, t, t), BF16),
                        pltpu.VMEM((2, 2, 1, t), F32)],
    )
    return pl.pallas_call(
        functools.partial(_attn_kernel, t=t, seq=S),
        out_shape=jax.ShapeDtypeStruct((B, S, ATTN_WIDTH), BF16),
        grid_spec=grid_spec,
        compiler_params=pltpu.CompilerParams(dimension_semantics=("parallel", "parallel", "arbitrary")),
        name="diff_attn",
    )(slopes, lam_params, subln, qkv, qkv, qkv)


def _prep_kernel(z_ref, zp_ref, zn_ref, tp_ref, tn_ref, w0_ref, w2_ref, a0_ref, a2_ref, g2_ref,
                 kk_ref, ka_ref, rk_ref,
                 r_o, v_o, kk_o, k0_o, k1_o, b0_o, b1_o, lw0_o, lw1_o, bonus_o, gate_o):
    i = pl.program_id(1)
    nt = pl.num_programs(1)
    z = z_ref[0]
    ts = z.shape[0]
    row = lax.broadcasted_iota(jnp.int32, (ts, 1), 0)
    prev_row = jnp.where(i > 0, zp_ref[0, 7:8, :], 0.0)
    next_row = jnp.where(i < nt - 1, zn_ref[0, 0:1, :], 0.0)
    z_prev = jnp.where(row == 0, prev_row, pltpu.roll(z, 1, 0))
    z_next = jnp.where(row == ts - 1, next_row, pltpu.roll(z, ts - 1, 0))
    zs = z + tp_ref[...] * (z_prev - z) + tn_ref[...] * (z_next - z)

    W = RWKV_WIDTH
    r = zs[:, 0:W]
    k = zs[:, W:2 * W]
    v = zs[:, 2 * W:3 * W]
    wd = jnp.tanh(zs[:, 3 * W:3 * W + LANES])
    ad = zs[:, 3 * W + LANES:3 * W + 2 * LANES]
    gd = zs[:, 3 * W + 2 * LANES:]

    w_log = w0_ref[...] + _dot3(wd, w2_ref[...])
    lw = -_sigmoid(w_log) * math.exp(-0.5)
    iclr = _sigmoid(a0_ref[...] + _dot3(ad, a2_ref[...]))
    gate = _dot3(_sigmoid(gd), g2_ref[...])

    ones = _head_ones(LANES)
    kk = k * kk_ref[...]
    kk = kk * lax.rsqrt(jnp.maximum(_head_sum(kk * kk, ones), 1e-24))
    ka = ka_ref[...]
    rk = rk_ref[...]
    bonus = jnp.zeros_like(r)
    k_outs = (k0_o, k1_o)
    b_outs = (b0_o, b1_o)
    lw_outs = (lw0_o, lw1_o)
    for d in range(N_DIR):
        a_d = iclr[:, d * W:(d + 1) * W]
        k_d = k * (1.0 + (a_d - 1.0) * ka)
        k_outs[d][0] = k_d.astype(k_outs[d].dtype)
        b_outs[d][0] = (kk * a_d).astype(b_outs[d].dtype)
        lw_outs[d][0] = lw[:, d * W:(d + 1) * W]
        bonus = bonus + _head_sum(r * k_d * rk, ones) * v
    r_o[0] = r.astype(r_o.dtype)
    v_o[0] = v.astype(v_o.dtype)
    kk_o[0] = kk.astype(kk_o.dtype)
    bonus_o[0] = bonus
    gate_o[0] = gate.astype(gate_o.dtype)


def _rwkv_prep(z, tp, tn, w0, w2bd, a0, a2bd, g2p, k_k, k_a, r_k, *, ts):
    B, S, ZC = z.shape
    nt = S // ts
    hb = ts // 8
    last8 = S // 8 - 1
    const = lambda shape: pl.BlockSpec(shape, lambda b, i: (0, 0))
    out_dtypes = (BF16,) * 7 + (F32, F32, F32, BF16)
    out_spec = pl.BlockSpec((1, ts, RWKV_WIDTH), lambda b, i: (b, i, 0))
    return pl.pallas_call(
        _prep_kernel,
        out_shape=tuple(jax.ShapeDtypeStruct((B, S, RWKV_WIDTH), dt) for dt in out_dtypes),
        grid=(B, nt),
        in_specs=[pl.BlockSpec((1, ts, ZC), lambda b, i: (b, i, 0)),
                  pl.BlockSpec((1, 8, ZC), lambda b, i: (b, jnp.maximum(i * hb - 1, 0), 0)),
                  pl.BlockSpec((1, 8, ZC), lambda b, i: (b, jnp.minimum((i + 1) * hb, last8), 0)),
                  const((1, ZC)), const((1, ZC)),
                  const((1, N_DIR * RWKV_WIDTH)), const((LANES, N_DIR * RWKV_WIDTH)),
                  const((1, N_DIR * RWKV_WIDTH)), const((LANES, N_DIR * RWKV_WIDTH)),
                  const((GATE_LORA_PAD, RWKV_WIDTH)),
                  const((1, RWKV_WIDTH)), const((1, RWKV_WIDTH)), const((1, RWKV_WIDTH))],
        out_specs=(out_spec,) * 11,
        compiler_params=pltpu.CompilerParams(dimension_semantics=("parallel", "arbitrary")),
        name="rwkv_prep",
    )(z, z, z, tp, tn, w0, w2bd, a0, a2bd, g2p, k_k, k_a, r_k)


def _scan_kernel(r_ref, kk_ref, k_ref, b_ref, v_ref, lw_ref, y_ref, s_ref, rp_ref, y0_ref, pm_ref, qm_ref,
                 *, tb, reverse):
    C = CHUNK
    W = LANES
    nc = tb // C
    order = [(nc - 1 - jj) if reverse else jj for jj in range(nc)]
    sls = [slice(j * C, (j + 1) * C) for j in order]
    ends = [j * C if reverse else (j + 1) * C - 1 for j in order]

    @pl.when(pl.program_id(2) == 0)
    def _():
        s_ref[...] = jnp.zeros_like(s_ref)
        rp_ref[...] = jnp.zeros_like(rp_ref)
        y0_ref[...] = jnp.zeros_like(y0_ref)
        pm_ref[...] = jnp.zeros_like(pm_ref)
        qm_ref[...] = jnp.zeros_like(qm_ref)

    lw = lw_ref[0]
    kk32, r32, b32, k32, v_all = (x[0].astype(F32) for x in (kk_ref, r_ref, b_ref, k_ref, v_ref))
    chain = {"s": s_ref[...], "n": 0}

    def chain_step():
        n = chain["n"]
        if n < nc:
            s16 = chain["s"].astype(BF16)
            y_ref[0, sls[n], :] = _dot(rp_ref[n], s16) + y0_ref[n]
            chain["s"] = _dot(pm_ref[n], s16) + qm_ref[n]
            chain["n"] = n + 1

    pos = lax.broadcasted_iota(jnp.int32, (tb, 1), 0) % C
    c = lw
    shift = 1
    while shift < C:
        if reverse:
            c = c + jnp.where(pos < C - shift, pltpu.roll(c, tb - shift, 0), 0.0)
        else:
            c = c + jnp.where(pos >= shift, pltpu.roll(c, shift, 0), 0.0)
        shift *= 2
    d = jnp.concatenate([c[e:e + 1, :] - c[j * C:(j + 1) * C] for j, e in
                         sorted(zip(order, ends))], axis=0)

    e_c = jnp.exp(c)
    e_nc = jnp.exp(-c)
    e_d = jnp.exp(d)
    at_all = -kk32 * jnp.exp(c - lw)
    rt_all = r32 * e_c
    bt_all = b32 * e_nc
    kt_all = k32 * e_nc
    bh_all = b32 * e_d
    kh_all = k32 * e_d

    r2 = lax.broadcasted_iota(jnp.int32, (W, W), 0)
    c2 = lax.broadcasted_iota(jnp.int32, (W, W), 1)
    bd_mask = (r2 // C) == (c2 // RWKV_HEAD)
    eye = r2 == c2
    tr = lax.broadcasted_iota(jnp.int32, (C, W), 0)
    sc = lax.broadcasted_iota(jnp.int32, (C, W), 1) % C
    strict = (sc > tr) if reverse else (sc < tr)
    incl = (sc >= tr) if reverse else (sc <= tr)

    def bd(x):
        return jnp.where(bd_mask, jnp.concatenate([x] * PAIR, axis=0), 0.0).astype(BF16)

    def bd2(xa, xb):
        return jnp.concatenate([bd(xa), bd(xb)], axis=1)

    At = [at_all[sl] for sl in sls]
    Rt = [rt_all[sl] for sl in sls]
    V = [v_all[sl] for sl in sls]
    bdV = [bd(x) for x in V]
    lhs = [jnp.concatenate([a, r], axis=0).astype(BF16) for a, r in zip(At, Rt)]
    sb = [_dot_nt(l, bd(bt_all[sl])) for l, sl in zip(lhs, sls)]
    sk = [_dot_nt(l, bd(kt_all[sl])) for l, sl in zip(lhs, sls)]
    lp = [jnp.where(strict, x[:C], 0.0) for x in sb]
    Lak = [jnp.where(strict, x[:C], 0.0).astype(BF16) for x in sk]
    Mrb = [jnp.where(incl, x[C:], 0.0).astype(BF16) for x in sb]
    Mrk = [jnp.where(incl, x[C:], 0.0).astype(BF16) for x in sk]
    chain_step()
    xa = list(At)
    xu = [_dot(l, b) for l, b in zip(Lak, bdV)]
    chain_step()
    n_dbl = C.bit_length() - 1
    for it in range(n_dbl):
        lp16 = [x.astype(BF16) for x in lp]
        upd = [_dot(l, bd2(a, u)) for l, a, u in zip(lp16, xa, xu)]
        xa = [a + x[:, :W] for a, x in zip(xa, upd)]
        xu = [u + x[:, W:] for u, x in zip(xu, upd)]
        chain_step()
        if it + 1 < n_dbl:
            lp = [_dot(l16, bd(l)) for l16, l in zip(lp16, lp)]
    rb = [_dot(m, bd2(a, u)) for m, a, u in zip(Mrb, xa, xu)]
    while chain["n"] < nc:
        chain_step()
    s_ref[...] = chain["s"]
    au = [jnp.concatenate([a, u], axis=1).astype(BF16) for a, u in zip(xa, xu)]
    pb = [_dot_tn(bh_all[sl].astype(BF16), x) for sl, x in zip(sls, au)]
    kv = [_dot_tn(kh_all[sl].astype(BF16), x.astype(BF16)) for sl, x in zip(sls, V)]
    for n in range(nc):
        rp_ref[n] = (Rt[n] + rb[n][:, :W]).astype(BF16)
        y0_ref[n] = rb[n][:, W:] + _dot(Mrk[n], bdV[n])
        pm_ref[n] = (jnp.where(bd_mask, pb[n][:, :W], 0.0)
                     + jnp.where(eye, e_c[ends[n]:ends[n] + 1, :], 0.0)).astype(BF16)
        qm_ref[n] = jnp.where(bd_mask, pb[n][:, W:] + kv[n], 0.0)


def _rwkv_scan(r, kk, k_d, b_d, v, lw_d, *, tb, reverse):
    B, S, Wd = r.shape
    nblk = S // tb
    nc = tb // CHUNK
    if reverse:
        in_idx = lambda b, p, i: (b, nblk - 1 - jnp.minimum(i, nblk - 1), p)
        out_idx = lambda b, p, i: (b, nblk - 1 - jnp.maximum(i - 1, 0), p)
    else:
        in_idx = lambda b, p, i: (b, jnp.minimum(i, nblk - 1), p)
        out_idx = lambda b, p, i: (b, jnp.maximum(i - 1, 0), p)
    spec = pl.BlockSpec((1, tb, LANES), in_idx)
    return pl.pallas_call(
        functools.partial(_scan_kernel, tb=tb, reverse=reverse),
        out_shape=jax.ShapeDtypeStruct((B, S, Wd), F32),
        grid=(B, Wd // LANES, nblk + 1),
        in_specs=[spec] * 6,
        out_specs=pl.BlockSpec((1, tb, LANES), out_idx),
        scratch_shapes=[pltpu.VMEM((LANES, LANES), F32),
                        pltpu.VMEM((nc, CHUNK, LANES), BF16), pltpu.VMEM((nc, CHUNK, LANES), F32),
                        pltpu.VMEM((nc, LANES, LANES), BF16), pltpu.VMEM((nc, LANES, LANES), F32)],
        compiler_params=pltpu.CompilerParams(dimension_semantics=("parallel", "parallel", "arbitrary")),
        name="rwkv_scan_bwd" if reverse else "rwkv_scan_fwd",
    )(r, kk, k_d, b_d, v, lw_d)


def _outproj_kernel(x_ref, attn_ref, yf_ref, yb_ref, bonus_ref, gate_ref, lw_ref, lb_ref, w_ref, g_ref, o_ref):
    y = yf_ref[...] + yb_ref[...]
    mean_mat = _head_ones(LANES)
    inv_n = 1.0 / RWKV_HEAD
    mu = _head_sum(y, mean_mat) * inv_n
    yc = y - mu
    var = _head_sum(yc * yc, mean_mat) * inv_n
    yn = yc * lax.rsqrt(var + LNX_EPS) * lw_ref[...] + lb_ref[...]
    rw = ((yn + bonus_ref[...]) * gate_ref[...]).astype(BF16)
    m = _dot(attn_ref[...], w_ref[0:ATTN_WIDTH, :]) + _dot(rw, w_ref[ATTN_WIDTH:, :])
    ms = jnp.mean(m * m, axis=-1, keepdims=True)
    o_ref[...] = x_ref[...] + m * lax.rsqrt(ms + NORM_EPS) * g_ref[...]


def _outproj(x2d, attn2d, yf, yb, bonus, gate, lnx_w, lnx_b, w_out_bf16, gain, *, tm):
    T = x2d.shape[0]
    row = lambda w: pl.BlockSpec((tm, w), lambda m: (m, 0))
    const = lambda shape: pl.BlockSpec(shape, lambda m: (0, 0))
    return pl.pallas_call(
        _outproj_kernel,
        out_shape=jax.ShapeDtypeStruct((T, D_MODEL), F32),
        grid=(T // tm,),
        in_specs=[row(D_MODEL), row(ATTN_WIDTH), row(RWKV_WIDTH), row(RWKV_WIDTH), row(RWKV_WIDTH), row(RWKV_WIDTH),
                  const((1, RWKV_WIDTH)), const((1, RWKV_WIDTH)),
                  const((D_MODEL, D_MODEL)), const((1, D_MODEL))],
        out_specs=row(D_MODEL),
        compiler_params=pltpu.CompilerParams(dimension_semantics=("parallel",)),
        name="outproj",
    )(x2d, attn2d, yf, yb, bonus, gate, lnx_w, lnx_b, w_out_bf16, gain)


def _ffn_kernel(x_ref, gpre_ref, wg_ref, wu_ref, wd_ref, gpost_ref, o_ref, h_ref, acc_ref):
    j = pl.program_id(1)

    @pl.when(j == 0)
    def _():
        x = x_ref[...]
        ms = jnp.mean(x * x, axis=-1, keepdims=True)
        h_ref[...] = (x * lax.rsqrt(ms + NORM_EPS) * gpre_ref[...]).astype(BF16)
        acc_ref[...] = jnp.zeros_like(acc_ref)

    h = h_ref[...]
    g = _dot(h, wg_ref[...])
    u = _dot(h, wu_ref[...])
    a = (g * _sigmoid(g) * u).astype(BF16)
    acc_ref[...] += _dot(a, wd_ref[...])

    @pl.when(j == pl.num_programs(1) - 1)
    def _():
        f = acc_ref[...]
        ms = jnp.mean(f * f, axis=-1, keepdims=True)
        o_ref[...] = x_ref[...] + f * lax.rsqrt(ms + NORM_EPS) * gpost_ref[...]


def _ffn(x2d, gpre, wg, wu, wd, gpost, *, tm, tf):
    T = x2d.shape[0]
    return pl.pallas_call(
        _ffn_kernel,
        out_shape=jax.ShapeDtypeStruct((T, D_MODEL), F32),
        grid=(T // tm, D_FF // tf),
        in_specs=[pl.BlockSpec((tm, D_MODEL), lambda m, j: (m, 0)),
                  pl.BlockSpec((1, D_MODEL), lambda m, j: (0, 0)),
                  pl.BlockSpec((D_MODEL, tf), lambda m, j: (0, j)),
                  pl.BlockSpec((D_MODEL, tf), lambda m, j: (0, j)),
                  pl.BlockSpec((tf, D_MODEL), lambda m, j: (j, 0)),
                  pl.BlockSpec((1, D_MODEL), lambda m, j: (0, 0))],
        out_specs=pl.BlockSpec((tm, D_MODEL), lambda m, j: (m, 0)),
        scratch_shapes=[pltpu.VMEM((tm, D_MODEL), BF16), pltpu.VMEM((tm, D_MODEL), F32)],
        compiler_params=pltpu.CompilerParams(dimension_semantics=("parallel", "arbitrary")),
        name="ffn",
    )(x2d, gpre, wg, wu, wd, gpost)


def _tiles(S):
    return dict(tmi=1024, tm=512, tn=512, ta=512, ts=256, tb=512, tf=512)


def _lora_blockdiag(w):
    zero = jnp.zeros_like(w[0])
    return jnp.concatenate([jnp.concatenate([w[0], zero], axis=1),
                            jnp.concatenate([zero, w[1]], axis=1)], axis=0)


def _encoder_layer(x, norm_mix_pre, norm_mix_post, w_in, w_out, lambda_q1, lambda_k1, lambda_q2, lambda_k2,
                   attn_subln, tshift_prev, tshift_next, w0, w2, a0, a2, g2, k_k, k_a, r_k, lnx_w, lnx_b,
                   norm_ffn_pre, norm_ffn_post, w_gate, w_up, w_down):
    B, S, D = x.shape
    T = B * S
    t = _tiles(S)
    row = lambda a: a.reshape(1, -1).astype(F32)
    pad_cols = RWKV_COLS_PAD - RWKV_COLS

    col_scale = jnp.where(jnp.arange(w_in.shape[1]) < ATTN_WIDTH, Q_PRESCALE, 1.0).astype(F32)
    w_in_p = jnp.pad(w_in * col_scale, ((0, 0), (0, pad_cols))).astype(BF16)
    x2d = x.reshape(T, D)
    qkv, z = _inproj(x2d, row(norm_mix_pre), w_in_p, tm=t["tmi"], tn=t["tn"])

    lam_params = jnp.stack([lambda_q1, lambda_k1, lambda_q2, lambda_k2]).astype(F32)
    attn = _diff_attention(qkv.reshape(B, S, QKV_COLS), lam_params, row(attn_subln), t=t["ta"])

    tp = jnp.pad(tshift_prev, (0, pad_cols)).reshape(1, -1)
    tn_ = jnp.pad(tshift_next, (0, pad_cols)).reshape(1, -1)
    g2p = jnp.pad(g2, ((0, GATE_LORA_PAD - GATE_LORA), (0, 0)))
    (r, v, kk, k0, k1, b0, b1, lw0, lw1, bonus, gate) = _rwkv_prep(
        z.reshape(B, S, RWKV_COLS_PAD), tp, tn_, row(w0), _lora_blockdiag(w2), row(a0), _lora_blockdiag(a2), g2p,
        row(k_k), row(k_a), row(r_k), ts=t["ts"])
    yf = _rwkv_scan(r, kk, k0, b0, v, lw0, tb=t["tb"], reverse=False)
    yb = _rwkv_scan(r, kk, k1, b1, v, lw1, tb=t["tb"], reverse=True)

    flat = lambda a: a.reshape(T, -1)
    x1 = _outproj(x2d, flat(attn), flat(yf), flat(yb), flat(bonus), flat(gate), row(lnx_w), row(lnx_b),
                  w_out.astype(BF16), row(norm_mix_post), tm=t["tm"])
    out = _ffn(x1, row(norm_ffn_pre), w_gate.astype(BF16), w_up.astype(BF16), w_down.astype(BF16),
               row(norm_ffn_post), tm=t["tm"], tf=t["tf"])
    return out.reshape(B, S, D)


def kernel(x_prompt, x_sample, norm_mix_pre, norm_mix_post, w_in, w_out, lambda_q1, lambda_k1, lambda_q2,
           lambda_k2, attn_subln, tshift_prev, tshift_next, w0, w2, a0, a2, g2, k_k, k_a, r_k, lnx_w, lnx_b,
           norm_ffn_pre, norm_ffn_post, w_gate, w_up, w_down):
    assert x_prompt.shape[1:] == x_sample.shape[1:], "both trunks share the sequence length"
    assert norm_mix_pre.shape[0] == 1, "single layer"
    nb = x_prompt.shape[0]
    x = jnp.concatenate([x_prompt, x_sample], axis=0)
    y = _encoder_layer(x, norm_mix_pre[0], norm_mix_post[0], w_in[0], w_out[0], lambda_q1[0], lambda_k1[0],
                       lambda_q2[0], lambda_k2[0], attn_subln[0], tshift_prev[0], tshift_next[0], w0[0], w2[0],
                       a0[0], a2[0], g2[0], k_k[0], k_a[0], r_k[0], lnx_w[0], lnx_b[0], norm_ffn_pre[0],
                       norm_ffn_post[0], w_gate[0], w_up[0], w_down[0])
    return (y[:nb], y[nb:])
```

```python
import functools
import math

import jax
import jax.numpy as jnp
from jax import lax
from jax.experimental import pallas as pl
from jax.experimental.pallas import tpu as pltpu

F32 = jnp.float32
BF16 = jnp.bfloat16

D_MODEL = 2048
ATTN_HEADS = 8
ATTN_VDIM = 128
ATTN_QKDIM = 64
ATTN_WIDTH = ATTN_HEADS * ATTN_VDIM
RWKV_HEAD = 64
RWKV_HEADS = 16
RWKV_WIDTH = RWKV_HEAD * RWKV_HEADS
N_DIR = 2
LORA = 64
GATE_LORA = 160
GATE_LORA_PAD = 256
QKV_COLS = 3 * ATTN_WIDTH
RWKV_COLS = 3 * RWKV_WIDTH + 2 * N_DIR * LORA + GATE_LORA
RWKV_COLS_PAD = 3 * RWKV_WIDTH + 2 * N_DIR * LORA + GATE_LORA_PAD
D_FF = 5632
NORM_EPS = 1e-6
LNX_EPS = 64e-5
LAMBDA_INIT = 0.8 - 0.6 * math.exp(-0.3 * 0)

LANES = 128
CHUNK = 64
PAIR = LANES // RWKV_HEAD


def _dot(a, b):
    return jnp.dot(a, b, preferred_element_type=F32)


def _dot_nt(a, b):
    return lax.dot_general(a, b, (((1,), (1,)), ((), ())), preferred_element_type=F32)


def _dot_tn(a, b):
    return lax.dot_general(a, b, (((0,), (0,)), ((), ())), preferred_element_type=F32)


def _split(x):
    hi = x.astype(BF16)
    lo = (x - hi.astype(F32)).astype(BF16)
    return hi, lo


def _dot3(a, b):
    ah, al = _split(a)
    bh, bl = _split(b)
    return _dot(ah, bh) + _dot(al, bh) + _dot(ah, bl)


def _sigmoid(x):
    return 1.0 / (1.0 + jnp.exp(-x))


def _head_ones(width):
    r = lax.broadcasted_iota(jnp.int32, (width, width), 0) // RWKV_HEAD
    c = lax.broadcasted_iota(jnp.int32, (width, width), 1) // RWKV_HEAD
    return jnp.where(r == c, 1.0, 0.0).astype(BF16)


def _head_sum(x, ones):
    outs = []
    for g in range(x.shape[1] // LANES):
        hi, lo = _split(x[:, g * LANES:(g + 1) * LANES])
        outs.append(_dot(hi, ones) + _dot(lo, ones))
    return jnp.concatenate(outs, axis=1)


def _inproj_kernel(x_ref, g_ref, w_ref, qkv_ref, z_ref, h_ref, *, n_qkv_tiles):
    n = pl.program_id(1)

    @pl.when(n == 0)
    def _():
        x = x_ref[...]
        ms = jnp.mean(x * x, axis=-1, keepdims=True)
        h_ref[...] = (x * lax.rsqrt(ms + NORM_EPS) * g_ref[...]).astype(BF16)

    acc = _dot(h_ref[...], w_ref[...])

    @pl.when(n < n_qkv_tiles)
    def _():
        qkv_ref[...] = acc.astype(BF16)

    @pl.when(n >= n_qkv_tiles)
    def _():
        z_ref[...] = acc


def _inproj(x2d, gain, w_in_bf16, *, tm, tn):
    T = x2d.shape[0]
    n_cols = w_in_bf16.shape[1]
    nq = QKV_COLS // tn
    grid = (T // tm, n_cols // tn)
    return pl.pallas_call(
        functools.partial(_inproj_kernel, n_qkv_tiles=nq),
        out_shape=(jax.ShapeDtypeStruct((T, QKV_COLS), BF16),
                   jax.ShapeDtypeStruct((T, RWKV_COLS_PAD), F32)),
        grid=grid,
        in_specs=[pl.BlockSpec((tm, D_MODEL), lambda m, n: (m, 0)),
                  pl.BlockSpec((1, D_MODEL), lambda m, n: (0, 0)),
                  pl.BlockSpec((D_MODEL, tn), lambda m, n: (0, n))],
        out_specs=(pl.BlockSpec((tm, tn), lambda m, n: (m, jnp.minimum(n, nq - 1))),
                   pl.BlockSpec((tm, tn), lambda m, n: (m, jnp.maximum(n - nq, 0)))),
        scratch_shapes=[pltpu.VMEM((tm, D_MODEL), BF16)],
        compiler_params=pltpu.CompilerParams(dimension_semantics=("parallel", "arbitrary")),
        name="inproj",
    )(x2d, gain, w_in_bf16)


LOG2E = math.log2(math.e)
Q_PRESCALE = ATTN_QKDIM ** -0.5 * LOG2E
AUX = 6
NEG_BIG = -1e30
SKIP_BITS = 80.0
NORM_SLACK = 1.01


def _pos_aux(slope2, n, first, sign_pos, sign_one):
    row = lax.broadcasted_iota(jnp.int32, (n, LANES), 0).astype(F32)
    lane = lax.broadcasted_iota(jnp.int32, (n, LANES), 1)
    val = slope2 * row * sign_pos
    hi = val.astype(BF16).astype(F32)
    mid = (val - hi).astype(BF16).astype(F32)
    lo = (val - hi - mid).astype(BF16).astype(F32)
    out = jnp.where(lane == first, hi, 0.0)
    out = jnp.where(lane == first + 1, mid, out)
    out = jnp.where(lane == first + 2, lo, out)
    out = jnp.where((lane >= first + 3) & (lane < first + AUX), sign_one, out)
    return out.astype(BF16)


def _attn_kernel(slopes_ref, lam_ref, sub_ref, q_ref, k_ref, v_ref, o_ref,
                 ka_ref, vt_ref, kn_ref, qa_ref, bd_ref, m_ref, l_ref, acc_ref, s_ref, sm_ref, p_ref, al_ref,
                 *, t, seq):
    h = pl.program_id(1)
    qi = pl.program_id(2)
    nt = seq // t
    slope2 = slopes_ref[h] * LOG2E
    lane = lax.broadcasted_iota(jnp.int32, (t, LANES), 1)
    own = (lane < ATTN_QKDIM, lane >= ATTN_QKDIM)
    aux0 = (ATTN_QKDIM, 0)

    @pl.when(qi == 0)
    def _():
        ak = [_pos_aux(slope2, t, aux0[c], 1.0, 1.0) for c in range(2)]

        def build(j, kn):
            off = pl.multiple_of(j * t, t)
            kt = k_ref[0, pl.ds(off, t), :]
            for c in range(2):
                ka_ref[c, j] = jnp.where(own[c], kt, ak[c])
            vt_ref[j] = v_ref[0, pl.ds(off, t), :].astype(F32).T.astype(BF16)
            k32 = kt.astype(F32)
            ksq = k32 * k32
            rows = jnp.maximum(jnp.sum(jnp.where(own[0], ksq, 0.0), axis=1, keepdims=True),
                               jnp.sum(jnp.where(own[1], ksq, 0.0), axis=1, keepdims=True))
            return jnp.maximum(kn, jnp.max(rows, axis=0, keepdims=True))

        kn_ref[...] = lax.fori_loop(0, nt, build, jnp.zeros((1, 1), F32))

        for c in range(2):
            a = aux0[c]
            lane_pos = (lane >= a + 3) & (lane < a + AUX)
            lane_one = (lane >= a) & (lane < a + 3)
            pos = _pos_aux(slope2, t, a + 3, 1.0, 0.0).astype(F32)
            qa_ref[c, 0] = jnp.where(lane_one, 1.0, jnp.where(lane_pos, -pos, 0.0)).astype(BF16)
            qa_ref[c, 1] = jnp.where(lane_one, -1.0, jnp.where(lane_pos, pos, 0.0)).astype(BF16)
        rel = (lax.broadcasted_iota(jnp.int32, (t, t), 0)
               - lax.broadcasted_iota(jnp.int32, (t, t), 1)).astype(F32)
        bd_ref[...] = -slope2 * jnp.abs(rel)

    q = q_ref[0]
    zero = jnp.zeros((t, LANES), BF16)
    q_var = {c: (jnp.where(own[c], q, qa_ref[c, 0]), jnp.where(own[c], q, zero), jnp.where(own[c], q, qa_ref[c, 1]))
             for c in range(2)}

    m_ref[...] = jnp.full(m_ref.shape, NEG_BIG, F32)
    l_ref[...] = jnp.zeros(l_ref.shape, F32)
    acc_ref[...] = jnp.zeros(acc_ref.shape, F32)

    def softmax_update(c, s, smax, cst):
        m_old = m_ref[c]
        m_new = jnp.maximum(m_old, smax + cst)
        alpha = jnp.exp2(m_old - m_new)
        p = jnp.exp2(s - (m_new - cst))
        l_ref[c] = alpha * l_ref[c] + jnp.sum(p, axis=0, keepdims=True)
        m_ref[c] = m_new
        return alpha, p.astype(BF16)

    q32 = q.astype(F32)
    qsq = q32 * q32
    qn = jnp.maximum(jnp.sum(jnp.where(own[0], qsq, 0.0), axis=1, keepdims=True),
                     jnp.sum(jnp.where(own[1], qsq, 0.0), axis=1, keepdims=True))
    qk_bound = 2.0 * NORM_SLACK * jnp.sqrt(jnp.max(qn, axis=0, keepdims=True) * kn_ref[...])
    dist_needed = (qk_bound + SKIP_BITS) / slope2
    reach = jnp.clip(jnp.ceil((dist_needed - 1.0) / t), 0.0, nt - 1.0)[0, 0].astype(jnp.int32)
    n_left = jnp.minimum(qi, reach)
    n_off = n_left + jnp.minimum(nt - 1 - qi, reach)

    def key_tile(n):
        return jnp.clip(jnp.where(n < n_left, qi - 1 - n, qi + 1 + n - n_left), 0, nt - 1)

    def stage_scores(n, slot):
        j = key_tile(n)
        for c in range(2):
            qsel = jnp.where(j < qi, q_var[c][0], q_var[c][2])
            s = _dot_nt(ka_ref[c, j], qsel)
            s_ref[slot, c] = s
            sm_ref[slot, c] = jnp.max(s, axis=0, keepdims=True)

    def stage_softmax(n, slot):
        j = key_tile(n)
        cst = jnp.where(n < n_off, -slope2 * (jnp.abs(qi - j) * t).astype(F32), NEG_BIG)
        for c in range(2):
            alpha, p = softmax_update(c, s_ref[slot, c], sm_ref[slot, c], cst)
            p_ref[slot, c] = p
            al_ref[slot, c] = alpha

    def stage_values(n, slot):
        j = key_tile(n)
        for c in range(2):
            acc_ref[c] = al_ref[slot, c] * acc_ref[c] + _dot(vt_ref[j], p_ref[slot, c])

    s_diag = [_dot_nt(ka_ref[c, qi], q_var[c][1]) + bd_ref[...] for c in range(2)]
    stage_scores(0, 0)
    stage_scores(1, 1)
    for c in range(2):
        alpha, p = softmax_update(c, s_diag[c], jnp.max(s_diag[c], axis=0, keepdims=True), 0.0)
        acc_ref[c] = alpha * acc_ref[c] + _dot(vt_ref[qi], p)
    stage_softmax(0, 0)

    def pair_body(mi, carry):
        n0 = 2 + 2 * mi
        stage_scores(n0, 0)
        stage_values(n0 - 2, 0)
        stage_softmax(n0 - 1, 1)
        stage_scores(n0 + 1, 1)
        stage_values(n0 - 1, 1)
        stage_softmax(n0, 0)
        return carry

    lax.fori_loop(0, (n_off + 1) // 2, pair_body, 0)

    lp = lam_ref[...]
    lam = (jnp.exp(jnp.sum(lp[0:1] * lp[1:2], axis=-1, keepdims=True))
           - jnp.exp(jnp.sum(lp[2:3] * lp[3:4], axis=-1, keepdims=True)) + LAMBDA_INIT)
    o_t = acc_ref[0] / l_ref[0] - lam * (acc_ref[1] / l_ref[1])
    o = o_t.T
    ms = jnp.mean(o * o, axis=-1, keepdims=True)
    o = o * lax.rsqrt(ms + NORM_EPS) * sub_ref[...] * (1.0 - LAMBDA_INIT)
    o_ref[0] = o.astype(o_ref.dtype)


def _diff_attention(qkv, lam_params, subln, *, t):
    B, S, _ = qkv.shape
    nt = S // t
    slopes = jnp.asarray([2.0 ** (-8.0 * (h + 1.0) / ATTN_HEADS) for h in range(ATTN_HEADS)], F32)
    grid_spec = pltpu.PrefetchScalarGridSpec(
        num_scalar_prefetch=1,
        grid=(B, ATTN_HEADS, nt),
        in_specs=[pl.BlockSpec((4, ATTN_QKDIM), lambda b, h, i, s: (0, 0)),
                  pl.BlockSpec((1, ATTN_VDIM), lambda b, h, i, s: (0, 0)),
                  pl.BlockSpec((1, t, LANES), lambda b, h, i, s: (b, i, h)),
                  pl.BlockSpec((1, S, LANES), lambda b, h, i, s: (b, 0, ATTN_HEADS + h)),
                  pl.BlockSpec((1, S, LANES), lambda b, h, i, s: (b, 0, 2 * ATTN_HEADS + h))],
        out_specs=pl.BlockSpec((1, t, LANES), lambda b, h, i, s: (b, i, h)),
        scratch_shapes=[pltpu.VMEM((2, nt, t, LANES), BF16),
                        pltpu.VMEM((nt, ATTN_VDIM, t), BF16),
                        pltpu.VMEM((1, 1), F32),
                        pltpu.VMEM((2, 2, t, LANES), BF16),
                        pltpu.VMEM((t, t), F32),
                        pltpu.VMEM((2, 1, t), F32), pltpu.VMEM((2, 1, t), F32),
                        pltpu.VMEM((2, ATTN_VDIM, t), F32),
                        pltpu.VMEM((2, 2, t, t), F32),
                        pltpu.VMEM((2, 2, 1, t), F32),
                        pltpu.VMEM((2, 2, t, t), BF16),
                        pltpu.VMEM((2, 2, 1, t), F32)],
    )
    return pl.pallas_call(
        functools.partial(_attn_kernel, t=t, seq=S),
        out_shape=jax.ShapeDtypeStruct((B, S, ATTN_WIDTH), BF16),
        grid_spec=grid_spec,
        compiler_params=pltpu.CompilerParams(dimension_semantics=("parallel", "parallel", "arbitrary")),
        name="diff_attn",
    )(slopes, lam_params, subln, qkv, qkv, qkv)


def _prep_kernel(z_ref, zp_ref, zn_ref, tp_ref, tn_ref, w0_ref, w2_ref, a0_ref, a2_ref, g2_ref,
                 kk_ref, ka_ref, rk_ref,
                 r_o, v_o, kk_o, k0_o, k1_o, b0_o, b1_o, lw0_o, lw1_o, bonus_o, gate_o):
    i = pl.program_id(1)
    nt = pl.num_programs(1)
    z = z_ref[0]
    ts = z.shape[0]
    row = lax.broadcasted_iota(jnp.int32, (ts, 1), 0)
    prev_row = jnp.where(i > 0, zp_ref[0, 7:8, :], 0.0)
    next_row = jnp.where(i < nt - 1, zn_ref[0, 0:1, :], 0.0)
    z_prev = jnp.where(row == 0, prev_row, pltpu.roll(z, 1, 0))
    z_next = jnp.where(row == ts - 1, next_row, pltpu.roll(z, ts - 1, 0))
    zs = z + tp_ref[...] * (z_prev - z) + tn_ref[...] * (z_next - z)

    W = RWKV_WIDTH
    r = zs[:, 0:W]
    k = zs[:, W:2 * W]
    v = zs[:, 2 * W:3 * W]
    wd = jnp.tanh(zs[:, 3 * W:3 * W + LANES])
    ad = zs[:, 3 * W + LANES:3 * W + 2 * LANES]
    gd = zs[:, 3 * W + 2 * LANES:]

    w_log = w0_ref[...] + _dot3(wd, w2_ref[...])
    lw = -_sigmoid(w_log) * math.exp(-0.5)
    iclr = _sigmoid(a0_ref[...] + _dot3(ad, a2_ref[...]))
    gate = _dot3(_sigmoid(gd), g2_ref[...])

    ones = _head_ones(LANES)
    kk = k * kk_ref[...]
    kk = kk * lax.rsqrt(jnp.maximum(_head_sum(kk * kk, ones), 1e-24))
    ka = ka_ref[...]
    rk = rk_ref[...]
    bonus = jnp.zeros_like(r)
    k_outs = (k0_o, k1_o)
    b_outs = (b0_o, b1_o)
    lw_outs = (lw0_o, lw1_o)
    for d in range(N_DIR):
        a_d = iclr[:, d * W:(d + 1) * W]
        k_d = k * (1.0 + (a_d - 1.0) * ka)
        k_outs[d][0] = k_d.astype(k_outs[d].dtype)
        b_outs[d][0] = (kk * a_d).astype(b_outs[d].dtype)
        lw_outs[d][0] = lw[:, d * W:(d + 1) * W]
        bonus = bonus + _head_sum(r * k_d * rk, ones) * v
    r_o[0] = r.astype(r_o.dtype)
    v_o[0] = v.astype(v_o.dtype)
    kk_o[0] = kk.astype(kk_o.dtype)
    bonus_o[0] = bonus
    gate_o[0] = gate.astype(gate_o.dtype)


def _rwkv_prep(z, tp, tn, w0, w2bd, a0, a2bd, g2p, k_k, k_a, r_k, *, ts):
    B, S, ZC = z.shape
    nt = S // ts
    hb = ts // 8
    last8 = S // 8 - 1
    const = lambda shape: pl.BlockSpec(shape, lambda b, i: (0, 0))
    out_dtypes = (BF16,) * 7 + (F32, F32, F32, BF16)
    out_spec = pl.BlockSpec((1, ts, RWKV_WIDTH), lambda b, i: (b, i, 0))
    return pl.pallas_call(
        _prep_kernel,
        out_shape=tuple(jax.ShapeDtypeStruct((B, S, RWKV_WIDTH), dt) for dt in out_dtypes),
        grid=(B, nt),
        in_specs=[pl.BlockSpec((1, ts, ZC), lambda b, i: (b, i, 0)),
                  pl.BlockSpec((1, 8, ZC), lambda b, i: (b, jnp.maximum(i * hb - 1, 0), 0)),
                  pl.BlockSpec((1, 8, ZC), lambda b, i: (b, jnp.minimum((i + 1) * hb, last8), 0)),
                  const((1, ZC)), const((1, ZC)),
                  const((1, N_DIR * RWKV_WIDTH)), const((LANES, N_DIR * RWKV_WIDTH)),
                  const((1, N_DIR * RWKV_WIDTH)), const((LANES, N_DIR * RWKV_WIDTH)),
                  const((GATE_LORA_PAD, RWKV_WIDTH)),
                  const((1, RWKV_WIDTH)), const((1, RWKV_WIDTH)), const((1, RWKV_WIDTH))],
        out_specs=(out_spec,) * 11,
        compiler_params=pltpu.CompilerParams(dimension_semantics=("parallel", "arbitrary")),
        name="rwkv_prep",
    )(z, z, z, tp, tn, w0, w2bd, a0, a2bd, g2p, k_k, k_a, r_k)


def _scan_kernel(r_ref, kk_ref, k_ref, b_ref, v_ref, lw_ref, y_ref, s_ref, rp_ref, y0_ref, pm_ref, qm_ref,
                 *, tb, reverse):
    C = CHUNK
    W = LANES
    nc = tb // C
    order = [(nc - 1 - jj) if reverse else jj for jj in range(nc)]
    sls = [slice(j * C, (j + 1) * C) for j in order]
    ends = [j * C if reverse else (j + 1) * C - 1 for j in order]

    @pl.when(pl.program_id(2) == 0)
    def _():
        s_ref[...] = jnp.zeros_like(s_ref)
        rp_ref[...] = jnp.zeros_like(rp_ref)
        y0_ref[...] = jnp.zeros_like(y0_ref)
        pm_ref[...] = jnp.zeros_like(pm_ref)
        qm_ref[...] = jnp.zeros_like(qm_ref)

    lw = lw_ref[0]
    kk32, r32, b32, k32, v_all = (x[0].astype(F32) for x in (kk_ref, r_ref, b_ref, k_ref, v_ref))
    chain = {"s": s_ref[...], "n": 0}

    def chain_step():
        n = chain["n"]
        if n < nc:
            s16 = chain["s"].astype(BF16)
            y_ref[0, sls[n], :] = _dot(rp_ref[n], s16) + y0_ref[n]
            chain["s"] = _dot(pm_ref[n], s16) + qm_ref[n]
            chain["n"] = n + 1

    pos = lax.broadcasted_iota(jnp.int32, (tb, 1), 0) % C
    c = lw
    shift = 1
    while shift < C:
        if reverse:
            c = c + jnp.where(pos < C - shift, pltpu.roll(c, tb - shift, 0), 0.0)
        else:
            c = c + jnp.where(pos >= shift, pltpu.roll(c, shift, 0), 0.0)
        shift *= 2
    d = jnp.concatenate([c[e:e + 1, :] - c[j * C:(j + 1) * C] for j, e in
                         sorted(zip(order, ends))], axis=0)

    e_c = jnp.exp(c)
    e_nc = jnp.exp(-c)
    e_d = jnp.exp(d)
    at_all = -kk32 * jnp.exp(c - lw)
    rt_all = r32 * e_c
    bt_all = b32 * e_nc
    kt_all = k32 * e_nc
    bh_all = b32 * e_d
    kh_all = k32 * e_d

    r2 = lax.broadcasted_iota(jnp.int32, (W, W), 0)
    c2 = lax.broadcasted_iota(jnp.int32, (W, W), 1)
    bd_mask = (r2 // C) == (c2 // RWKV_HEAD)
    eye = r2 == c2
    tr = lax.broadcasted_iota(jnp.int32, (C, W), 0)
    sc = lax.broadcasted_iota(jnp.int32, (C, W), 1) % C
    strict = (sc > tr) if reverse else (sc < tr)
    incl = (sc >= tr) if reverse else (sc <= tr)

    def bd(x):
        return jnp.where(bd_mask, jnp.concatenate([x] * PAIR, axis=0), 0.0).astype(BF16)

    def bd2(xa, xb):
        return jnp.concatenate([bd(xa), bd(xb)], axis=1)

    At = [at_all[sl] for sl in sls]
    Rt = [rt_all[sl] for sl in sls]
    V = [v_all[sl] for sl in sls]
    bdV = [bd(x) for x in V]
    lhs = [jnp.concatenate([a, r], axis=0).astype(BF16) for a, r in zip(At, Rt)]
    sb = [_dot_nt(l, bd(bt_all[sl])) for l, sl in zip(lhs, sls)]
    sk = [_dot_nt(l, bd(kt_all[sl])) for l, sl in zip(lhs, sls)]
    lp = [jnp.where(strict, x[:C], 0.0) for x in sb]
    Lak = [jnp.where(strict, x[:C], 0.0).astype(BF16) for x in sk]
    Mrb = [jnp.where(incl, x[C:], 0.0).astype(BF16) for x in sb]
    Mrk = [jnp.where(incl, x[C:], 0.0).astype(BF16) for x in sk]
    chain_step()
    xa = list(At)
    xu = [_dot(l, b) for l, b in zip(Lak, bdV)]
    chain_step()
    n_dbl = C.bit_length() - 1
    for it in range(n_dbl):
        lp16 = [x.astype(BF16) for x in lp]
        upd = [_dot(l, bd2(a, u)) for l, a, u in zip(lp16, xa, xu)]
        xa = [a + x[:, :W] for a, x in zip(xa, upd)]
        xu = [u + x[:, W:] for u, x in zip(xu, upd)]
        chain_step()
        if it + 1 < n_dbl:
            lp = [_dot(l16, bd(l)) for l16, l in zip(lp16, lp)]
    rb = [_dot(m, bd2(a, u)) for m, a, u in zip(Mrb, xa, xu)]
    while chain["n"] < nc:
        chain_step()
    s_ref[...] = chain["s"]
    au = [jnp.concatenate([a, u], axis=1).astype(BF16) for a, u in zip(xa, xu)]
    pb = [_dot_tn(bh_all[sl].astype(BF16), x) for sl, x in zip(sls, au)]
    kv = [_dot_tn(kh_all[sl].astype(BF16), x.astype(BF16)) for sl, x in zip(sls, V)]
    for n in range(nc):
        rp_ref[n] = (Rt[n] + rb[n][:, :W]).astype(BF16)
        y0_ref[n] = rb[n][:, W:] + _dot(Mrk[n], bdV[n])
        pm_ref[n] = (jnp.where(bd_mask, pb[n][:, :W], 0.0)
                     + jnp.where(eye, e_c[ends[n]:ends[n] + 1, :], 0.0)).astype(BF16)
        qm_ref[n] = jnp.where(bd_mask, pb[n][:, W:] + kv[n], 0.0)


def _rwkv_scan(r, kk, k_d, b_d, v, lw_d, *, tb, reverse):
    B, S, Wd = r.shape
    nblk = S // tb
    nc = tb // CHUNK
    if reverse:
        in_idx = lambda b, p, i: (b, nblk - 1 - jnp.minimum(i, nblk - 1), p)
        out_idx = lambda b, p, i: (b, nblk - 1 - jnp.maximum(i - 1, 0), p)
    else:
        in_idx = lambda b, p, i: (b, jnp.minimum(i, nblk - 1), p)
        out_idx = lambda b, p, i: (b, jnp.maximum(i - 1, 0), p)
    spec = pl.BlockSpec((1, tb, LANES), in_idx)
    return pl.pallas_call(
        functools.partial(_scan_kernel, tb=tb, reverse=reverse),
        out_shape=jax.ShapeDtypeStruct((B, S, Wd), F32),
        grid=(B, Wd // LANES, nblk + 1),
        in_specs=[spec] * 6,
        out_specs=pl.BlockSpec((1, tb, LANES), out_idx),
        scratch_shapes=[pltpu.VMEM((LANES, LANES), F32),
                        pltpu.VMEM((nc, CHUNK, LANES), BF16), pltpu.VMEM((nc, CHUNK, LANES), F32),
                        pltpu.VMEM((nc, LANES, LANES), BF16), pltpu.VMEM((nc, LANES, LANES), F32)],
        compiler_params=pltpu.CompilerParams(dimension_semantics=("parallel", "parallel", "arbitrary")),
        name="rwkv_scan_bwd" if reverse else "rwkv_scan_fwd",
    )(r, kk, k_d, b_d, v, lw_d)


def _outproj_kernel(x_ref, attn_ref, yf_ref, yb_ref, bonus_ref, gate_ref, lw_ref, lb_ref, w_ref, g_ref, o_ref):
    y = yf_ref[...] + yb_ref[...]
    mean_mat = _head_ones(LANES)
    inv_n = 1.0 / RWKV_HEAD
    mu = _head_sum(y, mean_mat) * inv_n
    yc = y - mu
    var = _head_sum(yc * yc, mean_mat) * inv_n
    yn = yc * lax.rsqrt(var + LNX_EPS) * lw_ref[...] + lb_ref[...]
    rw = ((yn + bonus_ref[...]) * gate_ref[...]).astype(BF16)
    m = _dot(attn_ref[...], w_ref[0:ATTN_WIDTH, :]) + _dot(rw, w_ref[ATTN_WIDTH:, :])
    ms = jnp.mean(m * m, axis=-1, keepdims=True)
    o_ref[...] = x_ref[...] + m * lax.rsqrt(ms + NORM_EPS) * g_ref[...]


def _outproj(x2d, attn2d, yf, yb, bonus, gate, lnx_w, lnx_b, w_out_bf16, gain, *, tm):
    T = x2d.shape[0]
    row = lambda w: pl.BlockSpec((tm, w), lambda m: (m, 0))
    const = lambda shape: pl.BlockSpec(shape, lambda m: (0, 0))
    return pl.pallas_call(
        _outproj_kernel,
        out_shape=jax.ShapeDtypeStruct((T, D_MODEL), F32),
        grid=(T // tm,),
        in_specs=[row(D_MODEL), row(ATTN_WIDTH), row(RWKV_WIDTH), row(RWKV_WIDTH), row(RWKV_WIDTH), row(RWKV_WIDTH),
                  const((1, RWKV_WIDTH)), const((1, RWKV_WIDTH)),
                  const((D_MODEL, D_MODEL)), const((1, D_MODEL))],
        out_specs=row(D_MODEL),
        compiler_params=pltpu.CompilerParams(dimension_semantics=("parallel",)),
        name="outproj",
    )(x2d, attn2d, yf, yb, bonus, gate, lnx_w, lnx_b, w_out_bf16, gain)


def _ffn_kernel(x_ref, gpre_ref, wg_ref, wu_ref, wd_ref, gpost_ref, o_ref, h_ref, acc_ref):
    j = pl.program_id(1)

    @pl.when(j == 0)
    def _():
        x = x_ref[...]
        ms = jnp.mean(x * x, axis=-1, keepdims=True)
        h_ref[...] = (x * lax.rsqrt(ms + NORM_EPS) * gpre_ref[...]).astype(BF16)
        acc_ref[...] = jnp.zeros_like(acc_ref)

    h = h_ref[...]
    g = _dot(h, wg_ref[...])
    u = _dot(h, wu_ref[...])
    a = (g * _sigmoid(g) * u).astype(BF16)
    acc_ref[...] += _dot(a, wd_ref[...])

    @pl.when(j == pl.num_programs(1) - 1)
    def _():
        f = acc_ref[...]
        ms = jnp.mean(f * f, axis=-1, keepdims=True)
        o_ref[...] = x_ref[...] + f * lax.rsqrt(ms + NORM_EPS) * gpost_ref[...]


def _ffn(x2d, gpre, wg, wu, wd, gpost, *, tm, tf):
    T = x2d.shape[0]
    return pl.pallas_call(
        _ffn_kernel,
        out_shape=jax.ShapeDtypeStruct((T, D_MODEL), F32),
        grid=(T // tm, D_FF // tf),
        in_specs=[pl.BlockSpec((tm, D_MODEL), lambda m, j: (m, 0)),
                  pl.BlockSpec((1, D_MODEL), lambda m, j: (0, 0)),
                  pl.BlockSpec((D_MODEL, tf), lambda m, j: (0, j)),
                  pl.BlockSpec((D_MODEL, tf), lambda m, j: (0, j)),
                  pl.BlockSpec((tf, D_MODEL), lambda m, j: (j, 0)),
                  pl.BlockSpec((1, D_MODEL), lambda m, j: (0, 0))],
        out_specs=pl.BlockSpec((tm, D_MODEL), lambda m, j: (m, 0)),
        scratch_shapes=[pltpu.VMEM((tm, D_MODEL), BF16), pltpu.VMEM((tm, D_MODEL), F32)],
        compiler_params=pltpu.CompilerParams(dimension_semantics=("parallel", "arbitrary")),
        name="ffn",
    )(x2d, gpre, wg, wu, wd, gpost)


def _tiles(S):
    return dict(tmi=1024, tm=512, tn=512, ta=512, ts=256, tb=512, tf=512)


def _lora_blockdiag(w):
    zero = jnp.zeros_like(w[0])
    return jnp.concatenate([jnp.concatenate([w[0], zero], axis=1),
                            jnp.concatenate([zero, w[1]], axis=1)], axis=0)


def _encoder_layer(x, norm_mix_pre, norm_mix_post, w_in, w_out, lambda_q1, lambda_k1, lambda_q2, lambda_k2,
                   attn_subln, tshift_prev, tshift_next, w0, w2, a0, a2, g2, k_k, k_a, r_k, lnx_w, lnx_b,
                   norm_ffn_pre, norm_ffn_post, w_gate, w_up, w_down):
    B, S, D = x.shape
    T = B * S
    t = _tiles(S)
    row = lambda a: a.reshape(1, -1).astype(F32)
    pad_cols = RWKV_COLS_PAD - RWKV_COLS

    col_scale = jnp.where(jnp.arange(w_in.shape[1]) < ATTN_WIDTH, Q_PRESCALE, 1.0).astype(F32)
    w_in_p = jnp.pad(w_in * col_scale, ((0, 0), (0, pad_cols))).astype(BF16)
    x2d = x.reshape(T, D)
    qkv, z = _inproj(x2d, row(norm_mix_pre), w_in_p, tm=t["tmi"], tn=t["tn"])

    lam_params = jnp.stack([lambda_q1, lambda_k1, lambda_q2, lambda_k2]).astype(F32)
    attn = _diff_attention(qkv.reshape(B, S, QKV_COLS), lam_params, row(attn_subln), t=t["ta"])

    tp = jnp.pad(tshift_prev, (0, pad_cols)).reshape(1, -1)
    tn_ = jnp.pad(tshift_next, (0, pad_cols)).reshape(1, -1)
    g2p = jnp.pad(g2, ((0, GATE_LORA_PAD - GATE_LORA), (0, 0)))
    (r, v, kk, k0, k1, b0, b1, lw0, lw1, bonus, gate) = _rwkv_prep(
        z.reshape(B, S, RWKV_COLS_PAD), tp, tn_, row(w0), _lora_blockdiag(w2), row(a0), _lora_blockdiag(a2), g2p,
        row(k_k), row(k_a), row(r_k), ts=t["ts"])
    yf = _rwkv_scan(r, kk, k0, b0, v, lw0, tb=t["tb"], reverse=False)
    yb = _rwkv_scan(r, kk, k1, b1, v, lw1, tb=t["tb"], reverse=True)

    flat = lambda a: a.reshape(T, -1)
    x1 = _outproj(x2d, flat(attn), flat(yf), flat(yb), flat(bonus), flat(gate), row(lnx_w), row(lnx_b),
                  w_out.astype(BF16), row(norm_mix_post), tm=t["tm"])
    out = _ffn(x1, row(norm_ffn_pre), w_gate.astype(BF16), w_up.astype(BF16), w_down.astype(BF16),
               row(norm_ffn_post), tm=t["tm"], tf=t["tf"])
    return out.reshape(B, S, D)


def kernel(x_prompt, x_sample, norm_mix_pre, norm_mix_post, w_in, w_out, lambda_q1, lambda_k1, lambda_q2,
           lambda_k2, attn_subln, tshift_prev, tshift_next, w0, w2, a0, a2, g2, k_k, k_a, r_k, lnx_w, lnx_b,
           norm_ffn_pre, norm_ffn_post, w_gate, w_up, w_down):
    assert x_prompt.shape[1:] == x_sample.shape[1:], "both trunks share the sequence length"
    assert norm_mix_pre.shape[0] == 1, "single layer"
    nb = x_prompt.shape[0]
    x = jnp.concatenate([x_prompt, x_sample], axis=0)
    y = _encoder_layer(x, norm_mix_pre[0], norm_mix_post[0], w_in[0], w_out[0], lambda_q1[0], lambda_k1[0],
                       lambda_q2[0], lambda_k2[0], attn_subln[0], tshift_prev[0], tshift_next[0], w0[0], w2[0],
                       a0[0], a2[0], g2[0], k_k[0], k_a[0], r_k[0], lnx_w[0], lnx_b[0], norm_ffn_pre[0],
                       norm_ffn_post[0], w_gate[0], w_up[0], w_down[0])
    return (y[:nb], y[nb:])
```

```python
import functools
import math

import jax
import jax.numpy as jnp
from jax import lax
from jax.experimental import pallas as pl
from jax.experimental.pallas import tpu as pltpu

F32 = jnp.float32
BF16 = jnp.bfloat16

D_MODEL = 2048
ATTN_HEADS = 8
ATTN_VDIM = 128
ATTN_QKDIM = 64
ATTN_WIDTH = ATTN_HEADS * ATTN_VDIM
RWKV_HEAD = 64
RWKV_HEADS = 16
RWKV_WIDTH = RWKV_HEAD * RWKV_HEADS
N_DIR = 2
LORA = 64
GATE_LORA = 160
GATE_LORA_PAD = 256
QKV_COLS = 3 * ATTN_WIDTH
RWKV_COLS = 3 * RWKV_WIDTH + 2 * N_DIR * LORA + GATE_LORA
RWKV_COLS_PAD = 3 * RWKV_WIDTH + 2 * N_DIR * LORA + GATE_LORA_PAD
D_FF = 5632
NORM_EPS = 1e-6
LNX_EPS = 64e-5
LAMBDA_INIT = 0.8 - 0.6 * math.exp(-0.3 * 0)

LANES = 128
CHUNK = 64
PAIR = LANES // RWKV_HEAD


def _dot(a, b):
    return jnp.dot(a, b, preferred_element_type=F32)


def _dot_nt(a, b):
    return lax.dot_general(a, b, (((1,), (1,)), ((), ())), preferred_element_type=F32)


def _dot_tn(a, b):
    return lax.dot_general(a, b, (((0,), (0,)), ((), ())), preferred_element_type=F32)


def _split(x):
    hi = x.astype(BF16)
    lo = (x - hi.astype(F32)).astype(BF16)
    return hi, lo


def _dot3(a, b):
    ah, al = _split(a)
    bh, bl = _split(b)
    return _dot(ah, bh) + _dot(al, bh) + _dot(ah, bl)


def _sigmoid(x):
    return 1.0 / (1.0 + jnp.exp(-x))


def _head_ones(width):
    r = lax.broadcasted_iota(jnp.int32, (width, width), 0) // RWKV_HEAD
    c = lax.broadcasted_iota(jnp.int32, (width, width), 1) // RWKV_HEAD
    return jnp.where(r == c, 1.0, 0.0).astype(BF16)


def _head_sum(x, ones, two_pass=True):
    outs = []
    for g in range(x.shape[1] // LANES):
        xg = x[:, g * LANES:(g + 1) * LANES]
        if two_pass:
            hi, lo = _split(xg)
            outs.append(_dot(hi, ones) + _dot(lo, ones))
        else:
            outs.append(_dot(xg.astype(BF16), ones))
    return jnp.concatenate(outs, axis=1)


def _row_block_specs(tm, width, mp):
    return (pl.BlockSpec((tm, width), lambda m, *_: (jnp.minimum(m, mp - 1), 0)),
            pl.BlockSpec((tm, width), lambda m, *_: (jnp.maximum(m - mp, 0), 0)))


def _inproj_kernel(xp_ref, xs_ref, g_ref, w_ref, qkv_ref, z_ref, h_ref, *, n_qkv_tiles, mp):
    m = pl.program_id(0)
    n = pl.program_id(1)

    def norm(x_ref):
        x = x_ref[...]
        ms = jnp.mean(x * x, axis=-1, keepdims=True)
        h_ref[...] = (x * lax.rsqrt(ms + NORM_EPS) * g_ref[...]).astype(BF16)

    @pl.when((n == 0) & (m < mp))
    def _():
        norm(xp_ref)

    @pl.when((n == 0) & (m >= mp))
    def _():
        norm(xs_ref)

    acc = _dot(h_ref[...], w_ref[...])

    @pl.when(n < n_qkv_tiles)
    def _():
        qkv_ref[...] = acc.astype(BF16)

    @pl.when(n >= n_qkv_tiles)
    def _():
        z_ref[...] = acc


def _inproj(xp2d, xs2d, gain, w_in_bf16, *, tm, tn):
    mp = xp2d.shape[0] // tm
    T = xp2d.shape[0] + xs2d.shape[0]
    n_cols = w_in_bf16.shape[1]
    nq = QKV_COLS // tn
    grid = (T // tm, n_cols // tn)
    return pl.pallas_call(
        functools.partial(_inproj_kernel, n_qkv_tiles=nq, mp=mp),
        out_shape=(jax.ShapeDtypeStruct((T, QKV_COLS), BF16),
                   jax.ShapeDtypeStruct((T, RWKV_COLS_PAD), F32)),
        grid=grid,
        in_specs=[*_row_block_specs(tm, D_MODEL, mp),
                  pl.BlockSpec((1, D_MODEL), lambda m, n: (0, 0)),
                  pl.BlockSpec((D_MODEL, tn), lambda m, n: (0, n))],
        out_specs=(pl.BlockSpec((tm, tn), lambda m, n: (m, jnp.minimum(n, nq - 1))),
                   pl.BlockSpec((tm, tn), lambda m, n: (m, jnp.maximum(n - nq, 0)))),
        scratch_shapes=[pltpu.VMEM((tm, D_MODEL), BF16)],
        compiler_params=pltpu.CompilerParams(dimension_semantics=("parallel", "arbitrary")),
        name="inproj",
    )(xp2d, xs2d, gain, w_in_bf16)


LOG2E = math.log2(math.e)
Q_PRESCALE = ATTN_QKDIM ** -0.5 * LOG2E
AUX = 6
NEG_BIG = -1e30
SKIP_BITS = 80.0
NORM_SLACK = 1.01


def _pos_aux(slope2, n, first, sign_pos, sign_one):
    row = lax.broadcasted_iota(jnp.int32, (n, LANES), 0).astype(F32)
    lane = lax.broadcasted_iota(jnp.int32, (n, LANES), 1)
    val = slope2 * row * sign_pos
    hi = val.astype(BF16).astype(F32)
    mid = (val - hi).astype(BF16).astype(F32)
    lo = (val - hi - mid).astype(BF16).astype(F32)
    out = jnp.where(lane == first, hi, 0.0)
    out = jnp.where(lane == first + 1, mid, out)
    out = jnp.where(lane == first + 2, lo, out)
    out = jnp.where((lane >= first + 3) & (lane < first + AUX), sign_one, out)
    return out.astype(BF16)


def _attn_kernel(slopes_ref, lam_ref, sub_ref, q_ref, k_ref, v_ref, o_ref,
                 ka_ref, vt_ref, kn_ref, qa_ref, bd_ref, m_ref, l_ref, acc_ref, s_ref, sm_ref, p_ref, al_ref,
                 *, t, seq):
    h = pl.program_id(1)
    qi = pl.program_id(2)
    nt = seq // t
    slope2 = slopes_ref[h] * LOG2E
    lane = lax.broadcasted_iota(jnp.int32, (t, LANES), 1)
    own = (lane < ATTN_QKDIM, lane >= ATTN_QKDIM)
    aux0 = (ATTN_QKDIM, 0)

    @pl.when(qi == 0)
    def _():
        ak = [_pos_aux(slope2, t, aux0[c], 1.0, 1.0) for c in range(2)]

        def build(j, kn):
            off = pl.multiple_of(j * t, t)
            kt = k_ref[0, pl.ds(off, t), :]
            for c in range(2):
                ka_ref[c, j] = jnp.where(own[c], kt, ak[c])
            vt_ref[j] = v_ref[0, pl.ds(off, t), :].astype(F32).T.astype(BF16)
            k32 = kt.astype(F32)
            ksq = k32 * k32
            rows = jnp.maximum(jnp.sum(jnp.where(own[0], ksq, 0.0), axis=1, keepdims=True),
                               jnp.sum(jnp.where(own[1], ksq, 0.0), axis=1, keepdims=True))
            return jnp.maximum(kn, jnp.max(rows, axis=0, keepdims=True))

        kn_ref[...] = lax.fori_loop(0, nt, build, jnp.zeros((1, 1), F32))

        for c in range(2):
            a = aux0[c]
            lane_pos = (lane >= a + 3) & (lane < a + AUX)
            lane_one = (lane >= a) & (lane < a + 3)
            pos = _pos_aux(slope2, t, a + 3, 1.0, 0.0).astype(F32)
            qa_ref[c, 0] = jnp.where(lane_one, 1.0, jnp.where(lane_pos, -pos, 0.0)).astype(BF16)
            qa_ref[c, 1] = jnp.where(lane_one, -1.0, jnp.where(lane_pos, pos, 0.0)).astype(BF16)
        rel = (lax.broadcasted_iota(jnp.int32, (t, t), 0)
               - lax.broadcasted_iota(jnp.int32, (t, t), 1)).astype(F32)
        bd_ref[...] = -slope2 * jnp.abs(rel)

    q = q_ref[0]
    zero = jnp.zeros((t, LANES), BF16)
    q_var = {c: (jnp.where(own[c], q, qa_ref[c, 0]), jnp.where(own[c], q, zero), jnp.where(own[c], q, qa_ref[c, 1]))
             for c in range(2)}

    m_ref[...] = jnp.full(m_ref.shape, NEG_BIG, F32)
    l_ref[...] = jnp.zeros(l_ref.shape, F32)
    acc_ref[...] = jnp.zeros(acc_ref.shape, F32)

    def softmax_update(c, s, smax, cst):
        m_old = m_ref[c]
        m_new = jnp.maximum(m_old, smax + cst)
        alpha = jnp.exp2(m_old - m_new)
        p = jnp.exp2(s - (m_new - cst))
        l_ref[c] = alpha * l_ref[c] + jnp.sum(p, axis=0, keepdims=True)
        m_ref[c] = m_new
        return alpha, p.astype(BF16)

    q32 = q.astype(F32)
    qsq = q32 * q32
    qn = jnp.maximum(jnp.sum(jnp.where(own[0], qsq, 0.0), axis=1, keepdims=True),
                     jnp.sum(jnp.where(own[1], qsq, 0.0), axis=1, keepdims=True))
    qk_bound = 2.0 * NORM_SLACK * jnp.sqrt(jnp.max(qn, axis=0, keepdims=True) * kn_ref[...])
    dist_needed = (qk_bound + SKIP_BITS) / slope2
    reach = jnp.clip(jnp.ceil((dist_needed - 1.0) / t), 0.0, nt - 1.0)[0, 0].astype(jnp.int32)
    n_left = jnp.minimum(qi, reach)
    n_off = n_left + jnp.minimum(nt - 1 - qi, reach)

    def key_tile(n):
        return jnp.clip(jnp.where(n < n_left, qi - 1 - n, qi + 1 + n - n_left), 0, nt - 1)

    def stage_scores(n, slot):
        j = key_tile(n)
        for c in range(2):
            qsel = jnp.where(j < qi, q_var[c][0], q_var[c][2])
            s = _dot_nt(ka_ref[c, j], qsel)
            s_ref[slot, c] = s
            sm_ref[slot, c] = jnp.max(s, axis=0, keepdims=True)

    def stage_softmax(n, slot):
        j = key_tile(n)
        cst = jnp.where(n < n_off, -slope2 * (jnp.abs(qi - j) * t).astype(F32), NEG_BIG)
        for c in range(2):
            alpha, p = softmax_update(c, s_ref[slot, c], sm_ref[slot, c], cst)
            p_ref[slot, c] = p
            al_ref[slot, c] = alpha

    def stage_values(n, slot):
        j = key_tile(n)
        for c in range(2):
            acc_ref[c] = al_ref[slot, c] * acc_ref[c] + _dot(vt_ref[j], p_ref[slot, c])

    s_diag = [_dot_nt(ka_ref[c, qi], q_var[c][1]) + bd_ref[...] for c in range(2)]
    stage_scores(0, 0)
    stage_scores(1, 1)
    for c in range(2):
        alpha, p = softmax_update(c, s_diag[c], jnp.max(s_diag[c], axis=0, keepdims=True), 0.0)
        acc_ref[c] = alpha * acc_ref[c] + _dot(vt_ref[qi], p)
    stage_softmax(0, 0)

    def pair_body(mi, carry):
        n0 = 2 + 2 * mi
        stage_scores(n0, 0)
        stage_values(n0 - 2, 0)
        stage_softmax(n0 - 1, 1)
        stage_scores(n0 + 1, 1)
        stage_values(n0 - 1, 1)
        stage_softmax(n0, 0)
        return carry

    lax.fori_loop(0, (n_off + 1) // 2, pair_body, 0)

    lp = lam_ref[...]
    lam = (jnp.exp(jnp.sum(lp[0:1] * lp[1:2], axis=-1, keepdims=True))
           - jnp.exp(jnp.sum(lp[2:3] * lp[3:4], axis=-1, keepdims=True)) + LAMBDA_INIT)
    o_t = acc_ref[0] / l_ref[0] - lam * (acc_ref[1] / l_ref[1])
    o = o_t.T
    ms = jnp.mean(o * o, axis=-1, keepdims=True)
    o = o * lax.rsqrt(ms + NORM_EPS) * sub_ref[...] * (1.0 - LAMBDA_INIT)
    o_ref[0] = o.astype(o_ref.dtype)


def _diff_attention(qkv, lam_params, subln, *, t):
    B, S, _ = qkv.shape
    nt = S // t
    slopes = jnp.asarray([2.0 ** (-8.0 * (h + 1.0) / ATTN_HEADS) for h in range(ATTN_HEADS)], F32)
    grid_spec = pltpu.PrefetchScalarGridSpec(
        num_scalar_prefetch=1,
        grid=(B, ATTN_HEADS, nt),
        in_specs=[pl.BlockSpec((4, ATTN_QKDIM), lambda b, h, i, s: (0, 0)),
                  pl.BlockSpec((1, ATTN_VDIM), lambda b, h, i, s: (0, 0)),
                  pl.BlockSpec((1, t, LANES), lambda b, h, i, s: (b, i, h)),
                  pl.BlockSpec((1, S, LANES), lambda b, h, i, s: (b, 0, ATTN_HEADS + h)),
                  pl.BlockSpec((1, S, LANES), lambda b, h, i, s: (b, 0, 2 * ATTN_HEADS + h))],
        out_specs=pl.BlockSpec((1, t, LANES), lambda b, h, i, s: (b, i, h)),
        scratch_shapes=[pltpu.VMEM((2, nt, t, LANES), BF16),
                        pltpu.VMEM((nt, ATTN_VDIM, t), BF16),
                        pltpu.VMEM((1, 1), F32),
                        pltpu.VMEM((2, 2, t, LANES), BF16),
                        pltpu.VMEM((t, t), F32),
                        pltpu.VMEM((2, 1, t), F32), pltpu.VMEM((2, 1, t), F32),
                        pltpu.VMEM((2, ATTN_VDIM, t), F32),
                        pltpu.VMEM((2, 2, t, t), F32),
                        pltpu.VMEM((2, 2, 1, t), F32),
                        pltpu.VMEM((2, 2, t, t), BF16),
                        pltpu.VMEM((2, 2, 1, t), F32)],
    )
    return pl.pallas_call(
        functools.partial(_attn_kernel, t=t, seq=S),
        out_shape=jax.ShapeDtypeStruct((B, S, ATTN_WIDTH), BF16),
        grid_spec=grid_spec,
        compiler_params=pltpu.CompilerParams(dimension_semantics=("parallel", "parallel", "arbitrary")),
        name="diff_attn",
    )(slopes, lam_params, subln, qkv, qkv, qkv)


def _prep_kernel(z_ref, zp_ref, zn_ref, tp_ref, tn_ref, w0_ref, w2_ref, a0_ref, a2_ref, g2_ref,
                 kk_ref, ka_ref, rk_ref,
                 r_o, v_o, kk_o, k0_o, k1_o, b0_o, b1_o, lw0_o, lw1_o, bonus_o, gate_o):
    i = pl.program_id(1)
    nt = pl.num_programs(1)
    z = z_ref[0]
    ts = z.shape[0]
    row = lax.broadcasted_iota(jnp.int32, (ts, 1), 0)
    prev_row = jnp.where(i > 0, zp_ref[0, 7:8, :], 0.0)
    next_row = jnp.where(i < nt - 1, zn_ref[0, 0:1, :], 0.0)
    z_prev = jnp.where(row == 0, prev_row, pltpu.roll(z, 1, 0))
    z_next = jnp.where(row == ts - 1, next_row, pltpu.roll(z, ts - 1, 0))
    zs = z + tp_ref[...] * (z_prev - z) + tn_ref[...] * (z_next - z)

    W = RWKV_WIDTH
    r = zs[:, 0:W]
    k = zs[:, W:2 * W]
    v = zs[:, 2 * W:3 * W]
    wd = jnp.tanh(zs[:, 3 * W:3 * W + LANES])
    ad = zs[:, 3 * W + LANES:3 * W + 2 * LANES]
    gd = zs[:, 3 * W + 2 * LANES:]

    w_log = w0_ref[...] + _dot3(wd, w2_ref[...])
    lw = -_sigmoid(w_log) * math.exp(-0.5)
    iclr = _sigmoid(a0_ref[...] + _dot(ad.astype(BF16), a2_ref[...].astype(BF16)))
    gate = _dot(_sigmoid(gd).astype(BF16), g2_ref[...].astype(BF16))

    ones = _head_ones(LANES)
    kk = k * kk_ref[...]
    kk = kk * lax.rsqrt(jnp.maximum(_head_sum(kk * kk, ones), 1e-24))
    ka = ka_ref[...]
    rk = rk_ref[...]
    bonus = jnp.zeros_like(r)
    k_outs = (k0_o, k1_o)
    b_outs = (b0_o, b1_o)
    lw_outs = (lw0_o, lw1_o)
    for d in range(N_DIR):
        a_d = iclr[:, d * W:(d + 1) * W]
        k_d = k * (1.0 + (a_d - 1.0) * ka)
        k_outs[d][0] = k_d.astype(k_outs[d].dtype)
        b_outs[d][0] = (kk * a_d).astype(b_outs[d].dtype)
        lw_outs[d][0] = lw[:, d * W:(d + 1) * W]
        bonus = bonus + _head_sum(r * k_d * rk, ones, two_pass=False) * v
    r_o[0] = r.astype(r_o.dtype)
    v_o[0] = v.astype(v_o.dtype)
    kk_o[0] = kk.astype(kk_o.dtype)
    bonus_o[0] = bonus
    gate_o[0] = gate.astype(gate_o.dtype)


def _rwkv_prep(z, tp, tn, w0, w2bd, a0, a2bd, g2p, k_k, k_a, r_k, *, ts):
    B, S, ZC = z.shape
    nt = S // ts
    hb = ts // 8
    last8 = S // 8 - 1
    const = lambda shape: pl.BlockSpec(shape, lambda b, i: (0, 0))
    out_dtypes = (BF16,) * 7 + (F32, F32, F32, BF16)
    out_spec = pl.BlockSpec((1, ts, RWKV_WIDTH), lambda b, i: (b, i, 0))
    return pl.pallas_call(
        _prep_kernel,
        out_shape=tuple(jax.ShapeDtypeStruct((B, S, RWKV_WIDTH), dt) for dt in out_dtypes),
        grid=(B, nt),
        in_specs=[pl.BlockSpec((1, ts, ZC), lambda b, i: (b, i, 0)),
                  pl.BlockSpec((1, 8, ZC), lambda b, i: (b, jnp.maximum(i * hb - 1, 0), 0)),
                  pl.BlockSpec((1, 8, ZC), lambda b, i: (b, jnp.minimum((i + 1) * hb, last8), 0)),
                  const((1, ZC)), const((1, ZC)),
                  const((1, N_DIR * RWKV_WIDTH)), const((LANES, N_DIR * RWKV_WIDTH)),
                  const((1, N_DIR * RWKV_WIDTH)), const((LANES, N_DIR * RWKV_WIDTH)),
                  const((GATE_LORA_PAD, RWKV_WIDTH)),
                  const((1, RWKV_WIDTH)), const((1, RWKV_WIDTH)), const((1, RWKV_WIDTH))],
        out_specs=(out_spec,) * 11,
        compiler_params=pltpu.CompilerParams(dimension_semantics=("parallel", "arbitrary")),
        name="rwkv_prep",
    )(z, z, z, tp, tn, w0, w2bd, a0, a2bd, g2p, k_k, k_a, r_k)


def _scan_kernel(r_ref, kk_ref, k_ref, b_ref, v_ref, lw_ref, y_ref, s_ref, rp_ref, y0_ref, pm_ref, qm_ref,
                 *, tb, reverse):
    C = CHUNK
    W = LANES
    nc = tb // C
    order = [(nc - 1 - jj) if reverse else jj for jj in range(nc)]
    sls = [slice(j * C, (j + 1) * C) for j in order]
    ends = [j * C if reverse else (j + 1) * C - 1 for j in order]

    @pl.when(pl.program_id(2) == 0)
    def _():
        s_ref[...] = jnp.zeros_like(s_ref)
        rp_ref[...] = jnp.zeros_like(rp_ref)
        y0_ref[...] = jnp.zeros_like(y0_ref)
        pm_ref[...] = jnp.zeros_like(pm_ref)
        qm_ref[...] = jnp.zeros_like(qm_ref)

    lw = lw_ref[0]
    kk32, r32, b32, k32, v_all = (x[0].astype(F32) for x in (kk_ref, r_ref, b_ref, k_ref, v_ref))
    chain = {"s": s_ref[...], "n": 0}

    def chain_step():
        n = chain["n"]
        if n < nc:
            s16 = chain["s"].astype(BF16)
            y_ref[0, sls[n], :] = _dot(rp_ref[n], s16) + y0_ref[n]
            chain["s"] = _dot(pm_ref[n], s16) + qm_ref[n]
            chain["n"] = n + 1

    pos = lax.broadcasted_iota(jnp.int32, (tb, 1), 0) % C
    c = lw
    shift = 1
    while shift < C:
        if reverse:
            c = c + jnp.where(pos < C - shift, pltpu.roll(c, tb - shift, 0), 0.0)
        else:
            c = c + jnp.where(pos >= shift, pltpu.roll(c, shift, 0), 0.0)
        shift *= 2
    d = jnp.concatenate([c[e:e + 1, :] - c[j * C:(j + 1) * C] for j, e in
                         sorted(zip(order, ends))], axis=0)

    e_c = jnp.exp(c)
    e_nc = jnp.exp(-c)
    e_d = jnp.exp(d)
    at_all = -kk32 * jnp.exp(c - lw)
    rt_all = r32 * e_c
    bt_all = b32 * e_nc
    kt_all = k32 * e_nc
    bh_all = b32 * e_d
    kh_all = k32 * e_d

    r2 = lax.broadcasted_iota(jnp.int32, (W, W), 0)
    c2 = lax.broadcasted_iota(jnp.int32, (W, W), 1)
    bd_mask = (r2 // C) == (c2 // RWKV_HEAD)
    eye = r2 == c2
    tr = lax.broadcasted_iota(jnp.int32, (C, W), 0)
    sc = lax.broadcasted_iota(jnp.int32, (C, W), 1) % C
    strict = (sc > tr) if reverse else (sc < tr)
    incl = (sc >= tr) if reverse else (sc <= tr)

    def bd(x):
        return jnp.where(bd_mask, jnp.concatenate([x] * PAIR, axis=0), 0.0).astype(BF16)

    def bd2(xa, xb):
        return jnp.concatenate([bd(xa), bd(xb)], axis=1)

    At = [at_all[sl] for sl in sls]
    Rt = [rt_all[sl] for sl in sls]
    V = [v_all[sl] for sl in sls]
    bdV = [bd(x) for x in V]
    lhs = [jnp.concatenate([a, r], axis=0).astype(BF16) for a, r in zip(At, Rt)]
    sb = [_dot_nt(l, bd(bt_all[sl])) for l, sl in zip(lhs, sls)]
    sk = [_dot_nt(l, bd(kt_all[sl])) for l, sl in zip(lhs, sls)]
    lp = [jnp.where(strict, x[:C], 0.0) for x in sb]
    Lak = [jnp.where(strict, x[:C], 0.0).astype(BF16) for x in sk]
    Mrb = [jnp.where(incl, x[C:], 0.0).astype(BF16) for x in sb]
    Mrk = [jnp.where(incl, x[C:], 0.0).astype(BF16) for x in sk]
    chain_step()
    xa = list(At)
    xu = [_dot(l, b) for l, b in zip(Lak, bdV)]
    chain_step()
    n_dbl = C.bit_length() - 1
    for it in range(n_dbl):
        lp16 = [x.astype(BF16) for x in lp]
        upd = [_dot(l, bd2(a, u)) for l, a, u in zip(lp16, xa, xu)]
        xa = [a + x[:, :W] for a, x in zip(xa, upd)]
        xu = [u + x[:, W:] for u, x in zip(xu, upd)]
        chain_step()
        if it + 1 < n_dbl:
            lp = [_dot(l16, bd(l)) for l16, l in zip(lp16, lp)]
    rb = [_dot(m, bd2(a, u)) for m, a, u in zip(Mrb, xa, xu)]
    while chain["n"] < nc:
        chain_step()
    s_ref[...] = chain["s"]
    au = [jnp.concatenate([a, u], axis=1).astype(BF16) for a, u in zip(xa, xu)]
    pb = [_dot_tn(bh_all[sl].astype(BF16), x) for sl, x in zip(sls, au)]
    kv = [_dot_tn(kh_all[sl].astype(BF16), x.astype(BF16)) for sl, x in zip(sls, V)]
    for n in range(nc):
        rp_ref[n] = (Rt[n] + rb[n][:, :W]).astype(BF16)
        y0_ref[n] = rb[n][:, W:] + _dot(Mrk[n], bdV[n])
        pm_ref[n] = (jnp.where(bd_mask, pb[n][:, :W], 0.0)
                     + jnp.where(eye, e_c[ends[n]:ends[n] + 1, :], 0.0)).astype(BF16)
        qm_ref[n] = jnp.where(bd_mask, pb[n][:, W:] + kv[n], 0.0)


def _rwkv_scan(r, kk, k_d, b_d, v, lw_d, *, tb, reverse):
    B, S, Wd = r.shape
    nblk = S // tb
    nc = tb // CHUNK
    if reverse:
        in_idx = lambda b, p, i: (b, nblk - 1 - jnp.minimum(i, nblk - 1), p)
        out_idx = lambda b, p, i: (b, nblk - 1 - jnp.maximum(i - 1, 0), p)
    else:
        in_idx = lambda b, p, i: (b, jnp.minimum(i, nblk - 1), p)
        out_idx = lambda b, p, i: (b, jnp.maximum(i - 1, 0), p)
    spec = pl.BlockSpec((1, tb, LANES), in_idx)
    return pl.pallas_call(
        functools.partial(_scan_kernel, tb=tb, reverse=reverse),
        out_shape=jax.ShapeDtypeStruct((B, S, Wd), F32),
        grid=(B, Wd // LANES, nblk + 1),
        in_specs=[spec] * 6,
        out_specs=pl.BlockSpec((1, tb, LANES), out_idx),
        scratch_shapes=[pltpu.VMEM((LANES, LANES), F32),
                        pltpu.VMEM((nc, CHUNK, LANES), BF16), pltpu.VMEM((nc, CHUNK, LANES), F32),
                        pltpu.VMEM((nc, LANES, LANES), BF16), pltpu.VMEM((nc, LANES, LANES), F32)],
        compiler_params=pltpu.CompilerParams(dimension_semantics=("parallel", "parallel", "arbitrary")),
        name="rwkv_scan_bwd" if reverse else "rwkv_scan_fwd",
    )(r, kk, k_d, b_d, v, lw_d)


def _outproj_kernel(xp_ref, xs_ref, attn_ref, yf_ref, yb_ref, bonus_ref, gate_ref, lw_ref, lb_ref, w_ref, g_ref,
                    o_ref, *, mp):
    y = yf_ref[...] + yb_ref[...]
    mean_mat = _head_ones(LANES)
    inv_n = 1.0 / RWKV_HEAD
    mu = _head_sum(y, mean_mat) * inv_n
    yc = y - mu
    var = _head_sum(yc * yc, mean_mat) * inv_n
    yn = yc * lax.rsqrt(var + LNX_EPS) * lw_ref[...] + lb_ref[...]
    rw = ((yn + bonus_ref[...]) * gate_ref[...]).astype(BF16)
    m = _dot(attn_ref[...], w_ref[0:ATTN_WIDTH, :]) + _dot(rw, w_ref[ATTN_WIDTH:, :])
    ms = jnp.mean(m * m, axis=-1, keepdims=True)
    upd = m * lax.rsqrt(ms + NORM_EPS) * g_ref[...]

    @pl.when(pl.program_id(0) < mp)
    def _():
        o_ref[...] = xp_ref[...] + upd

    @pl.when(pl.program_id(0) >= mp)
    def _():
        o_ref[...] = xs_ref[...] + upd


def _outproj(xp2d, xs2d, attn2d, yf, yb, bonus, gate, lnx_w, lnx_b, w_out_bf16, gain, *, tm):
    mp = xp2d.shape[0] // tm
    T = xp2d.shape[0] + xs2d.shape[0]
    row = lambda w: pl.BlockSpec((tm, w), lambda m: (m, 0))
    const = lambda shape: pl.BlockSpec(shape, lambda m: (0, 0))
    return pl.pallas_call(
        functools.partial(_outproj_kernel, mp=mp),
        out_shape=jax.ShapeDtypeStruct((T, D_MODEL), F32),
        grid=(T // tm,),
        in_specs=[*_row_block_specs(tm, D_MODEL, mp),
                  row(ATTN_WIDTH), row(RWKV_WIDTH), row(RWKV_WIDTH), row(RWKV_WIDTH), row(RWKV_WIDTH),
                  const((1, RWKV_WIDTH)), const((1, RWKV_WIDTH)),
                  const((D_MODEL, D_MODEL)), const((1, D_MODEL))],
        out_specs=row(D_MODEL),
        compiler_params=pltpu.CompilerParams(dimension_semantics=("parallel",)),
        name="outproj",
    )(xp2d, xs2d, attn2d, yf, yb, bonus, gate, lnx_w, lnx_b, w_out_bf16, gain)


def _ffn_kernel(x_ref, gpre_ref, wg_ref, wu_ref, wd_ref, gpost_ref, op_ref, os_ref, h_ref, acc_ref, *, mp):
    j = pl.program_id(1)

    @pl.when(j == 0)
    def _():
        x = x_ref[...]
        ms = jnp.mean(x * x, axis=-1, keepdims=True)
        h_ref[...] = (x * lax.rsqrt(ms + NORM_EPS) * gpre_ref[...]).astype(BF16)
        acc_ref[...] = jnp.zeros_like(acc_ref)

    h = h_ref[...]
    g = _dot(h, wg_ref[...])
    u = _dot(h, wu_ref[...])
    a = (g * _sigmoid(g) * u).astype(BF16)
    acc_ref[...] += _dot(a, wd_ref[...])

    def finish(o_ref):
        f = acc_ref[...]
        ms = jnp.mean(f * f, axis=-1, keepdims=True)
        o_ref[...] = x_ref[...] + f * lax.rsqrt(ms + NORM_EPS) * gpost_ref[...]

    last = j == pl.num_programs(1) - 1

    @pl.when(last & (pl.program_id(0) < mp))
    def _():
        finish(op_ref)

    @pl.when(last & (pl.program_id(0) >= mp))
    def _():
        finish(os_ref)


def _ffn(x2d, gpre, wg, wu, wd, gpost, *, tm, tf, rows_first):
    T = x2d.shape[0]
    mp = rows_first // tm
    return pl.pallas_call(
        functools.partial(_ffn_kernel, mp=mp),
        out_shape=(jax.ShapeDtypeStruct((rows_first, D_MODEL), F32),
                   jax.ShapeDtypeStruct((T - rows_first, D_MODEL), F32)),
        grid=(T // tm, D_FF // tf),
        in_specs=[pl.BlockSpec((tm, D_MODEL), lambda m, j: (m, 0)),
                  pl.BlockSpec((1, D_MODEL), lambda m, j: (0, 0)),
                  pl.BlockSpec((D_MODEL, tf), lambda m, j: (0, j)),
                  pl.BlockSpec((D_MODEL, tf), lambda m, j: (0, j)),
                  pl.BlockSpec((tf, D_MODEL), lambda m, j: (j, 0)),
                  pl.BlockSpec((1, D_MODEL), lambda m, j: (0, 0))],
        out_specs=_row_block_specs(tm, D_MODEL, mp),
        scratch_shapes=[pltpu.VMEM((tm, D_MODEL), BF16), pltpu.VMEM((tm, D_MODEL), F32)],
        compiler_params=pltpu.CompilerParams(dimension_semantics=("arbitrary", "arbitrary")),
        name="ffn",
    )(x2d, gpre, wg, wu, wd, gpost)


def _tiles(S):
    return dict(tmi=1024, tm=512, tn=512, ta=512, ts=256, tb=512, tf=512)


def _lora_blockdiag(w):
    zero = jnp.zeros_like(w[0])
    return jnp.concatenate([jnp.concatenate([w[0], zero], axis=1),
                            jnp.concatenate([zero, w[1]], axis=1)], axis=0)


def _encoder_layer(xp, xs, norm_mix_pre, norm_mix_post, w_in, w_out, lambda_q1, lambda_k1, lambda_q2, lambda_k2,
                   attn_subln, tshift_prev, tshift_next, w0, w2, a0, a2, g2, k_k, k_a, r_k, lnx_w, lnx_b,
                   norm_ffn_pre, norm_ffn_post, w_gate, w_up, w_down):
    (Bp, S, D), Bs = xp.shape, xs.shape[0]
    B = Bp + Bs
    T = B * S
    t = _tiles(S)
    row = lambda a: a.reshape(1, -1).astype(F32)
    pad_cols = RWKV_COLS_PAD - RWKV_COLS

    col_scale = jnp.where(jnp.arange(w_in.shape[1]) < ATTN_WIDTH, Q_PRESCALE, 1.0).astype(F32)
    w_in_p = jnp.pad(w_in * col_scale, ((0, 0), (0, pad_cols))).astype(BF16)
    xp2d, xs2d = xp.reshape(Bp * S, D), xs.reshape(Bs * S, D)
    qkv, z = _inproj(xp2d, xs2d, row(norm_mix_pre), w_in_p, tm=t["tmi"], tn=t["tn"])

    lam_params = jnp.stack([lambda_q1, lambda_k1, lambda_q2, lambda_k2]).astype(F32)
    attn = _diff_attention(qkv.reshape(B, S, QKV_COLS), lam_params, row(attn_subln), t=t["ta"])

    tp = jnp.pad(tshift_prev, (0, pad_cols)).reshape(1, -1)
    tn_ = jnp.pad(tshift_next, (0, pad_cols)).reshape(1, -1)
    g2p = jnp.pad(g2, ((0, GATE_LORA_PAD - GATE_LORA), (0, 0))).astype(BF16)
    (r, v, kk, k0, k1, b0, b1, lw0, lw1, bonus, gate) = _rwkv_prep(
        z.reshape(B, S, RWKV_COLS_PAD), tp, tn_, row(w0), _lora_blockdiag(w2), row(a0),
        _lora_blockdiag(a2).astype(BF16), g2p, row(k_k), row(k_a), row(r_k), ts=t["ts"])
    yf = _rwkv_scan(r, kk, k0, b0, v, lw0, tb=t["tb"], reverse=False)
    yb = _rwkv_scan(r, kk, k1, b1, v, lw1, tb=t["tb"], reverse=True)

    flat = lambda a: a.reshape(T, -1)
    x1 = _outproj(xp2d, xs2d, flat(attn), flat(yf), flat(yb), flat(bonus), flat(gate), row(lnx_w), row(lnx_b),
                  w_out.astype(BF16), row(norm_mix_post), tm=t["tm"])
    yp, ys = _ffn(x1, row(norm_ffn_pre), w_gate.astype(BF16), w_up.astype(BF16), w_down.astype(BF16),
                  row(norm_ffn_post), tm=t["tm"], tf=t["tf"], rows_first=Bp * S)
    return yp.reshape(Bp, S, D), ys.reshape(Bs, S, D)


def kernel(x_prompt, x_sample, norm_mix_pre, norm_mix_post, w_in, w_out, lambda_q1, lambda_k1, lambda_q2,
           lambda_k2, attn_subln, tshift_prev, tshift_next, w0, w2, a0, a2, g2, k_k, k_a, r_k, lnx_w, lnx_b,
           norm_ffn_pre, norm_ffn_post, w_gate, w_up, w_down):
    assert x_prompt.shape[1:] == x_sample.shape[1:], "both trunks share the sequence length"
    assert norm_mix_pre.shape[0] == 1, "single layer"
    return _encoder_layer(x_prompt, x_sample, norm_mix_pre[0], norm_mix_post[0], w_in[0], w_out[0], lambda_q1[0],
                          lambda_k1[0], lambda_q2[0], lambda_k2[0], attn_subln[0], tshift_prev[0], tshift_next[0],
                          w0[0], w2[0], a0[0], a2[0], g2[0], k_k[0], k_a[0], r_k[0], lnx_w[0], lnx_b[0],
                          norm_ffn_pre[0], norm_ffn_post[0], w_gate[0], w_up[0], w_down[0])
```

```python
import functools
import math

import jax
import jax.numpy as jnp
from jax import lax
from jax.experimental import pallas as pl
from jax.experimental.pallas import tpu as pltpu

F32 = jnp.float32
BF16 = jnp.bfloat16

D_MODEL = 2048
ATTN_HEADS = 8
ATTN_VDIM = 128
ATTN_QKDIM = 64
ATTN_WIDTH = ATTN_HEADS * ATTN_VDIM
RWKV_HEAD = 64
RWKV_HEADS = 16
RWKV_WIDTH = RWKV_HEAD * RWKV_HEADS
N_DIR = 2
LORA = 64
GATE_LORA = 160
GATE_LORA_PAD = 256
QKV_COLS = 3 * ATTN_WIDTH
RWKV_COLS = 3 * RWKV_WIDTH + 2 * N_DIR * LORA + GATE_LORA
RWKV_COLS_PAD = 3 * RWKV_WIDTH + 2 * N_DIR * LORA + GATE_LORA_PAD
D_FF = 5632
NORM_EPS = 1e-6
LNX_EPS = 64e-5
LAMBDA_INIT = 0.8 - 0.6 * math.exp(-0.3 * 0)

LANES = 128
CHUNK = 64
PAIR = LANES // RWKV_HEAD


def _dot(a, b):
    return jnp.dot(a, b, preferred_element_type=F32)


def _dot_nt(a, b):
    return lax.dot_general(a, b, (((1,), (1,)), ((), ())), preferred_element_type=F32)


def _dot_tn(a, b):
    return lax.dot_general(a, b, (((0,), (0,)), ((), ())), preferred_element_type=F32)


def _split(x):
    hi = x.astype(BF16)
    lo = (x - hi.astype(F32)).astype(BF16)
    return hi, lo


def _dot3(a, b):
    ah, al = _split(a)
    bh, bl = _split(b)
    return _dot(ah, bh) + _dot(al, bh) + _dot(ah, bl)


def _sigmoid(x):
    return 1.0 / (1.0 + jnp.exp(-x))


def _head_ones(width):
    r = lax.broadcasted_iota(jnp.int32, (width, width), 0) // RWKV_HEAD
    c = lax.broadcasted_iota(jnp.int32, (width, width), 1) // RWKV_HEAD
    return jnp.where(r == c, 1.0, 0.0).astype(BF16)


def _head_sum(x, ones, two_pass=True):
    outs = []
    for g in range(x.shape[1] // LANES):
        xg = x[:, g * LANES:(g + 1) * LANES]
        if two_pass:
            hi, lo = _split(xg)
            outs.append(_dot(hi, ones) + _dot(lo, ones))
        else:
            outs.append(_dot(xg.astype(BF16), ones))
    return jnp.concatenate(outs, axis=1)


def _row_block_specs(tm, width, mp):
    return (pl.BlockSpec((tm, width), lambda m, *_: (jnp.minimum(m, mp - 1), 0)),
            pl.BlockSpec((tm, width), lambda m, *_: (jnp.maximum(m - mp, 0), 0)))


def _inproj_kernel(xp_ref, xs_ref, g_ref, w_ref, qkv_ref, z_ref, h_ref, *, n_qkv_tiles, mp):
    m = pl.program_id(0)
    n = pl.program_id(1)

    def norm(x_ref):
        x = x_ref[...]
        ms = jnp.mean(x * x, axis=-1, keepdims=True)
        h_ref[...] = (x * lax.rsqrt(ms + NORM_EPS) * g_ref[...]).astype(BF16)

    @pl.when((n == 0) & (m < mp))
    def _():
        norm(xp_ref)

    @pl.when((n == 0) & (m >= mp))
    def _():
        norm(xs_ref)

    acc = _dot(h_ref[...], w_ref[...])

    @pl.when(n < n_qkv_tiles)
    def _():
        qkv_ref[...] = acc.astype(BF16)

    @pl.when(n >= n_qkv_tiles)
    def _():
        z_ref[...] = acc


def _inproj(xp2d, xs2d, gain, w_in_bf16, *, tm, tn):
    mp = xp2d.shape[0] // tm
    T = xp2d.shape[0] + xs2d.shape[0]
    n_cols = w_in_bf16.shape[1]
    nq = QKV_COLS // tn
    grid = (T // tm, n_cols // tn)
    return pl.pallas_call(
        functools.partial(_inproj_kernel, n_qkv_tiles=nq, mp=mp),
        out_shape=(jax.ShapeDtypeStruct((T, QKV_COLS), BF16),
                   jax.ShapeDtypeStruct((T, RWKV_COLS_PAD), F32)),
        grid=grid,
        in_specs=[*_row_block_specs(tm, D_MODEL, mp),
                  pl.BlockSpec((1, D_MODEL), lambda m, n: (0, 0)),
                  pl.BlockSpec((D_MODEL, tn), lambda m, n: (0, n))],
        out_specs=(pl.BlockSpec((tm, tn), lambda m, n: (m, jnp.minimum(n, nq - 1))),
                   pl.BlockSpec((tm, tn), lambda m, n: (m, jnp.maximum(n - nq, 0)))),
        scratch_shapes=[pltpu.VMEM((tm, D_MODEL), BF16)],
        compiler_params=pltpu.CompilerParams(dimension_semantics=("parallel", "arbitrary")),
        name="inproj",
    )(xp2d, xs2d, gain, w_in_bf16)


LOG2E = math.log2(math.e)
Q_PRESCALE = ATTN_QKDIM ** -0.5 * LOG2E
AUX = 6
NEG_BIG = -1e30
SKIP_BITS = 80.0
NORM_SLACK = 1.01
BF16_ROWS = 16
V_ROWS = ATTN_VDIM + BF16_ROWS


def _pos_aux(slope2, n, first, sign_pos, sign_one):
    row = lax.broadcasted_iota(jnp.int32, (n, LANES), 0).astype(F32)
    lane = lax.broadcasted_iota(jnp.int32, (n, LANES), 1)
    val = slope2 * row * sign_pos
    hi = val.astype(BF16).astype(F32)
    mid = (val - hi).astype(BF16).astype(F32)
    lo = (val - hi - mid).astype(BF16).astype(F32)
    out = jnp.where(lane == first, hi, 0.0)
    out = jnp.where(lane == first + 1, mid, out)
    out = jnp.where(lane == first + 2, lo, out)
    out = jnp.where((lane >= first + 3) & (lane < first + AUX), sign_one, out)
    return out.astype(BF16)


def _attn_kernel(slopes_ref, lam_ref, sub_ref, q_ref, k_ref, v_ref, o_ref,
                 ka_ref, vt_ref, kn_ref, qa_ref, bd_ref, m_ref, acc_ref, s_ref, sm_ref, p_ref, al_ref,
                 *, t, seq):
    h = pl.program_id(1)
    qi = pl.program_id(2)
    nt = seq // t
    slope2 = slopes_ref[h] * LOG2E
    lane = lax.broadcasted_iota(jnp.int32, (t, LANES), 1)
    own = (lane < ATTN_QKDIM, lane >= ATTN_QKDIM)
    aux0 = (ATTN_QKDIM, 0)

    @pl.when(qi == 0)
    def _():
        ak = [_pos_aux(slope2, t, aux0[c], 1.0, 1.0) for c in range(2)]
        ones_rows = jnp.where(lax.broadcasted_iota(jnp.int32, (BF16_ROWS, t), 0) == 0, 1.0, 0.0).astype(BF16)

        def build(j, kn):
            off = pl.multiple_of(j * t, t)
            kt = k_ref[0, pl.ds(off, t), :]
            for c in range(2):
                ka_ref[c, j] = jnp.where(own[c], kt, ak[c])
            vt_ref[j, 0:ATTN_VDIM, :] = v_ref[0, pl.ds(off, t), :].astype(F32).T.astype(BF16)
            vt_ref[j, ATTN_VDIM:, :] = ones_rows
            k32 = kt.astype(F32)
            ksq = k32 * k32
            rows = jnp.maximum(jnp.sum(jnp.where(own[0], ksq, 0.0), axis=1, keepdims=True),
                               jnp.sum(jnp.where(own[1], ksq, 0.0), axis=1, keepdims=True))
            return jnp.maximum(kn, jnp.max(rows, axis=0, keepdims=True))

        kn_ref[...] = lax.fori_loop(0, nt, build, jnp.zeros((1, 1), F32))

        for c in range(2):
            a = aux0[c]
            lane_pos = (lane >= a + 3) & (lane < a + AUX)
            lane_one = (lane >= a) & (lane < a + 3)
            pos = _pos_aux(slope2, t, a + 3, 1.0, 0.0).astype(F32)
            qa_ref[c, 0] = jnp.where(lane_one, 1.0, jnp.where(lane_pos, -pos, 0.0)).astype(BF16)
            qa_ref[c, 1] = jnp.where(lane_one, -1.0, jnp.where(lane_pos, pos, 0.0)).astype(BF16)
        rel = (lax.broadcasted_iota(jnp.int32, (t, t), 0)
               - lax.broadcasted_iota(jnp.int32, (t, t), 1)).astype(F32)
        bd_ref[...] = -slope2 * jnp.abs(rel)

    q = q_ref[0]
    zero = jnp.zeros((t, LANES), BF16)
    q_var = {c: (jnp.where(own[c], q, qa_ref[c, 0]), jnp.where(own[c], q, zero), jnp.where(own[c], q, qa_ref[c, 1]))
             for c in range(2)}

    m_ref[...] = jnp.full(m_ref.shape, NEG_BIG, F32)
    acc_ref[...] = jnp.zeros(acc_ref.shape, F32)

    def softmax_update(c, s, smax, cst):
        m_old = m_ref[c]
        m_new = jnp.maximum(m_old, smax + cst)
        alpha = jnp.exp2(m_old - m_new)
        p = jnp.exp2(s - (m_new - cst))
        m_ref[c] = m_new
        return alpha, p.astype(BF16)

    q32 = q.astype(F32)
    qsq = q32 * q32
    qn = jnp.maximum(jnp.sum(jnp.where(own[0], qsq, 0.0), axis=1, keepdims=True),
                     jnp.sum(jnp.where(own[1], qsq, 0.0), axis=1, keepdims=True))
    qk_bound = 2.0 * NORM_SLACK * jnp.sqrt(jnp.max(qn, axis=0, keepdims=True) * kn_ref[...])
    dist_needed = (qk_bound + SKIP_BITS) / slope2
    reach = jnp.clip(jnp.ceil((dist_needed - 1.0) / t), 0.0, nt - 1.0)[0, 0].astype(jnp.int32)
    n_left = jnp.minimum(qi, reach)
    n_off = n_left + jnp.minimum(nt - 1 - qi, reach)

    def key_tile(n):
        return jnp.clip(jnp.where(n < n_left, qi - 1 - n, qi + 1 + n - n_left), 0, nt - 1)

    def stage_scores(n, slot):
        j = key_tile(n)
        for c in range(2):
            qsel = jnp.where(j < qi, q_var[c][0], q_var[c][2])
            s = _dot_nt(ka_ref[c, j], qsel)
            s_ref[slot, c] = s
            sm_ref[slot, c] = jnp.max(s, axis=0, keepdims=True)

    def stage_softmax(n, slot):
        j = key_tile(n)
        cst = jnp.where(n < n_off, -slope2 * (jnp.abs(qi - j) * t).astype(F32), NEG_BIG)
        for c in range(2):
            alpha, p = softmax_update(c, s_ref[slot, c], sm_ref[slot, c], cst)
            p_ref[slot, c] = p
            al_ref[slot, c] = alpha

    def stage_values(n, slot):
        j = key_tile(n)
        for c in range(2):
            acc_ref[c] = al_ref[slot, c] * acc_ref[c] + _dot(vt_ref[j], p_ref[slot, c])

    s_diag = [_dot_nt(ka_ref[c, qi], q_var[c][1]) + bd_ref[...] for c in range(2)]
    stage_scores(0, 0)
    stage_scores(1, 1)
    for c in range(2):
        alpha, p = softmax_update(c, s_diag[c], jnp.max(s_diag[c], axis=0, keepdims=True), 0.0)
        acc_ref[c] = alpha * acc_ref[c] + _dot(vt_ref[qi], p)
    stage_softmax(0, 0)

    def pair_body(mi, carry):
        n0 = 2 + 2 * mi
        stage_scores(n0, 0)
        stage_values(n0 - 2, 0)
        stage_softmax(n0 - 1, 1)
        stage_scores(n0 + 1, 1)
        stage_values(n0 - 1, 1)
        stage_softmax(n0, 0)
        return carry

    lax.fori_loop(0, (n_off + 1) // 2, pair_body, 0)

    lp = lam_ref[...]
    lam = (jnp.exp(jnp.sum(lp[0:1] * lp[1:2], axis=-1, keepdims=True))
           - jnp.exp(jnp.sum(lp[2:3] * lp[3:4], axis=-1, keepdims=True)) + LAMBDA_INIT)
    num = [acc_ref[c, 0:ATTN_VDIM, :] for c in range(2)]
    den = [acc_ref[c, ATTN_VDIM:ATTN_VDIM + 1, :] for c in range(2)]
    o_t = num[0] / den[0] - lam * (num[1] / den[1])
    o = o_t.T
    ms = jnp.mean(o * o, axis=-1, keepdims=True)
    o = o * lax.rsqrt(ms + NORM_EPS) * sub_ref[...] * (1.0 - LAMBDA_INIT)
    o_ref[0] = o.astype(o_ref.dtype)


def _diff_attention(qkv, lam_params, subln, *, t):
    B, S, _ = qkv.shape
    nt = S // t
    slopes = jnp.asarray([2.0 ** (-8.0 * (h + 1.0) / ATTN_HEADS) for h in range(ATTN_HEADS)], F32)
    grid_spec = pltpu.PrefetchScalarGridSpec(
        num_scalar_prefetch=1,
        grid=(B, ATTN_HEADS, nt),
        in_specs=[pl.BlockSpec((4, ATTN_QKDIM), lambda b, h, i, s: (0, 0)),
                  pl.BlockSpec((1, ATTN_VDIM), lambda b, h, i, s: (0, 0)),
                  pl.BlockSpec((1, t, LANES), lambda b, h, i, s: (b, i, h)),
                  pl.BlockSpec((1, S, LANES), lambda b, h, i, s: (b, 0, ATTN_HEADS + h)),
                  pl.BlockSpec((1, S, LANES), lambda b, h, i, s: (b, 0, 2 * ATTN_HEADS + h))],
        out_specs=pl.BlockSpec((1, t, LANES), lambda b, h, i, s: (b, i, h)),
        scratch_shapes=[pltpu.VMEM((2, nt, t, LANES), BF16),
                        pltpu.VMEM((nt, V_ROWS, t), BF16),
                        pltpu.VMEM((1, 1), F32),
                        pltpu.VMEM((2, 2, t, LANES), BF16),
                        pltpu.VMEM((t, t), F32),
                        pltpu.VMEM((2, 1, t), F32),
                        pltpu.VMEM((2, V_ROWS, t), F32),
                        pltpu.VMEM((2, 2, t, t), F32),
                        pltpu.VMEM((2, 2, 1, t), F32),
                        pltpu.VMEM((2, 2, t, t), BF16),
                        pltpu.VMEM((2, 2, 1, t), F32)],
    )
    return pl.pallas_call(
        functools.partial(_attn_kernel, t=t, seq=S),
        out_shape=jax.ShapeDtypeStruct((B, S, ATTN_WIDTH), BF16),
        grid_spec=grid_spec,
        compiler_params=pltpu.CompilerParams(dimension_semantics=("parallel", "parallel", "arbitrary")),
        name="diff_attn",
    )(slopes, lam_params, subln, qkv, qkv, qkv)


def _prep_kernel(z_ref, zp_ref, zn_ref, tp_ref, tn_ref, w0_ref, w2_ref, a0_ref, a2_ref, g2_ref,
                 kk_ref, ka_ref, rk_ref,
                 r_o, v_o, kk_o, k0_o, k1_o, b0_o, b1_o, lw0_o, lw1_o, bonus_o, gate_o):
    i = pl.program_id(1)
    nt = pl.num_programs(1)
    z = z_ref[0]
    ts = z.shape[0]
    row = lax.broadcasted_iota(jnp.int32, (ts, 1), 0)
    prev_row = jnp.where(i > 0, zp_ref[0, 7:8, :], 0.0)
    next_row = jnp.where(i < nt - 1, zn_ref[0, 0:1, :], 0.0)
    z_prev = jnp.where(row == 0, prev_row, pltpu.roll(z, 1, 0))
    z_next = jnp.where(row == ts - 1, next_row, pltpu.roll(z, ts - 1, 0))
    zs = z + tp_ref[...] * (z_prev - z) + tn_ref[...] * (z_next - z)

    W = RWKV_WIDTH
    r = zs[:, 0:W]
    k = zs[:, W:2 * W]
    v = zs[:, 2 * W:3 * W]
    wd = jnp.tanh(zs[:, 3 * W:3 * W + LANES])
    ad = zs[:, 3 * W + LANES:3 * W + 2 * LANES]
    gd = zs[:, 3 * W + 2 * LANES:]

    w_log = w0_ref[...] + _dot3(wd, w2_ref[...])
    lw = -_sigmoid(w_log) * math.exp(-0.5)
    iclr = _sigmoid(a0_ref[...] + _dot(ad.astype(BF16), a2_ref[...].astype(BF16)))
    gate = _dot(_sigmoid(gd).astype(BF16), g2_ref[...].astype(BF16))

    ones = _head_ones(LANES)
    kk = k * kk_ref[...]
    kk = kk * lax.rsqrt(jnp.maximum(_head_sum(kk * kk, ones), 1e-24))
    ka = ka_ref[...]
    rk = rk_ref[...]
    bonus = jnp.zeros_like(r)
    k_outs = (k0_o, k1_o)
    b_outs = (b0_o, b1_o)
    lw_outs = (lw0_o, lw1_o)
    for d in range(N_DIR):
        a_d = iclr[:, d * W:(d + 1) * W]
        k_d = k * (1.0 + (a_d - 1.0) * ka)
        k_outs[d][0] = k_d.astype(k_outs[d].dtype)
        b_outs[d][0] = (kk * a_d).astype(b_outs[d].dtype)
        lw_outs[d][0] = lw[:, d * W:(d + 1) * W]
        bonus = bonus + _head_sum(r * k_d * rk, ones, two_pass=False) * v
    r_o[0] = r.astype(r_o.dtype)
    v_o[0] = v.astype(v_o.dtype)
    kk_o[0] = kk.astype(kk_o.dtype)
    bonus_o[0] = bonus
    gate_o[0] = gate.astype(gate_o.dtype)


def _rwkv_prep(z, tp, tn, w0, w2bd, a0, a2bd, g2p, k_k, k_a, r_k, *, ts):
    B, S, ZC = z.shape
    nt = S // ts
    hb = ts // 8
    last8 = S // 8 - 1
    const = lambda shape: pl.BlockSpec(shape, lambda b, i: (0, 0))
    out_dtypes = (BF16,) * 7 + (F32, F32, F32, BF16)
    out_spec = pl.BlockSpec((1, ts, RWKV_WIDTH), lambda b, i: (b, i, 0))
    return pl.pallas_call(
        _prep_kernel,
        out_shape=tuple(jax.ShapeDtypeStruct((B, S, RWKV_WIDTH), dt) for dt in out_dtypes),
        grid=(B, nt),
        in_specs=[pl.BlockSpec((1, ts, ZC), lambda b, i: (b, i, 0)),
                  pl.BlockSpec((1, 8, ZC), lambda b, i: (b, jnp.maximum(i * hb - 1, 0), 0)),
                  pl.BlockSpec((1, 8, ZC), lambda b, i: (b, jnp.minimum((i + 1) * hb, last8), 0)),
                  const((1, ZC)), const((1, ZC)),
                  const((1, N_DIR * RWKV_WIDTH)), const((LANES, N_DIR * RWKV_WIDTH)),
                  const((1, N_DIR * RWKV_WIDTH)), const((LANES, N_DIR * RWKV_WIDTH)),
                  const((GATE_LORA_PAD, RWKV_WIDTH)),
                  const((1, RWKV_WIDTH)), const((1, RWKV_WIDTH)), const((1, RWKV_WIDTH))],
        out_specs=(out_spec,) * 11,
        compiler_params=pltpu.CompilerParams(dimension_semantics=("parallel", "arbitrary")),
        name="rwkv_prep",
    )(z, z, z, tp, tn, w0, w2bd, a0, a2bd, g2p, k_k, k_a, r_k)


def _scan_kernel(r_ref, kk_ref, k_ref, b_ref, v_ref, lw_ref, y_ref, s_ref, rp_ref, y0_ref, pm_ref, qm_ref,
                 *, tb, reverse):
    C = CHUNK
    W = LANES
    nc = tb // C
    order = [(nc - 1 - jj) if reverse else jj for jj in range(nc)]
    sls = [slice(j * C, (j + 1) * C) for j in order]
    ends = [j * C if reverse else (j + 1) * C - 1 for j in order]

    @pl.when(pl.program_id(2) == 0)
    def _():
        s_ref[...] = jnp.zeros_like(s_ref)
        rp_ref[...] = jnp.zeros_like(rp_ref)
        y0_ref[...] = jnp.zeros_like(y0_ref)
        pm_ref[...] = jnp.zeros_like(pm_ref)
        qm_ref[...] = jnp.zeros_like(qm_ref)

    lw = lw_ref[0]
    kk32, r32, b32, k32, v_all = (x[0].astype(F32) for x in (kk_ref, r_ref, b_ref, k_ref, v_ref))
    chain = {"s": s_ref[...], "n": 0}

    def chain_step():
        n = chain["n"]
        if n < nc:
            s16 = chain["s"].astype(BF16)
            y_ref[0, sls[n], :] = _dot(rp_ref[n], s16) + y0_ref[n]
            chain["s"] = _dot(pm_ref[n], s16) + qm_ref[n]
            chain["n"] = n + 1

    pos = lax.broadcasted_iota(jnp.int32, (tb, 1), 0) % C
    c = lw
    shift = 1
    while shift < C:
        if reverse:
            c = c + jnp.where(pos < C - shift, pltpu.roll(c, tb - shift, 0), 0.0)
        else:
            c = c + jnp.where(pos >= shift, pltpu.roll(c, shift, 0), 0.0)
        shift *= 2
    d = jnp.concatenate([c[e:e + 1, :] - c[j * C:(j + 1) * C] for j, e in
                         sorted(zip(order, ends))], axis=0)

    e_c = jnp.exp(c)
    e_nc = jnp.exp(-c)
    e_d = jnp.exp(d)
    at_all = -kk32 * jnp.exp(c - lw)
    rt_all = r32 * e_c
    bt_all = b32 * e_nc
    kt_all = k32 * e_nc
    bh_all = b32 * e_d
    kh_all = k32 * e_d

    r2 = lax.broadcasted_iota(jnp.int32, (W, W), 0)
    c2 = lax.broadcasted_iota(jnp.int32, (W, W), 1)
    bd_mask = (r2 // C) == (c2 // RWKV_HEAD)
    eye = r2 == c2
    tr = lax.broadcasted_iota(jnp.int32, (C, W), 0)
    sc = lax.broadcasted_iota(jnp.int32, (C, W), 1) % C
    strict = (sc > tr) if reverse else (sc < tr)
    incl = (sc >= tr) if reverse else (sc <= tr)

    def bd(x):
        return jnp.where(bd_mask, jnp.concatenate([x] * PAIR, axis=0), 0.0).astype(BF16)

    def bd2(xa, xb):
        return jnp.concatenate([bd(xa), bd(xb)], axis=1)

    At = [at_all[sl] for sl in sls]
    Rt = [rt_all[sl] for sl in sls]
    V = [v_all[sl] for sl in sls]
    bdV = [bd(x) for x in V]
    lhs = [jnp.concatenate([a, r], axis=0).astype(BF16) for a, r in zip(At, Rt)]
    sbk = [_dot_nt(l, jnp.concatenate([bd(bt_all[sl]), bd(kt_all[sl])], axis=0)) for l, sl in zip(lhs, sls)]
    sb = [x[:, :W] for x in sbk]
    sk = [x[:, W:] for x in sbk]
    lp = [jnp.where(strict, x[:C], 0.0) for x in sb]
    Lak = [jnp.where(strict, x[:C], 0.0).astype(BF16) for x in sk]
    Mrb = [jnp.where(incl, x[C:], 0.0).astype(BF16) for x in sb]
    Mrk = [jnp.where(incl, x[C:], 0.0).astype(BF16) for x in sk]
    chain_step()
    xa = list(At)
    xu = [_dot(l, b) for l, b in zip(Lak, bdV)]
    chain_step()
    n_dbl = C.bit_length() - 1
    for it in range(n_dbl):
        lp16 = [x.astype(BF16) for x in lp]
        if it + 1 < n_dbl:
            upd = [_dot(l16, jnp.concatenate([bd2(a, u), bd(l)], axis=1))
                   for l16, l, a, u in zip(lp16, lp, xa, xu)]
            lp = [x[:, 2 * W:] for x in upd]
        else:
            upd = [_dot(l, bd2(a, u)) for l, a, u in zip(lp16, xa, xu)]
        xa = [a + x[:, :W] for a, x in zip(xa, upd)]
        xu = [u + x[:, W:2 * W] for u, x in zip(xu, upd)]
        chain_step()
    rb = [_dot(m, bd2(a, u)) for m, a, u in zip(Mrb, xa, xu)]
    while chain["n"] < nc:
        chain_step()
    s_ref[...] = chain["s"]
    au = [jnp.concatenate([a, u], axis=1).astype(BF16) for a, u in zip(xa, xu)]
    pb = [_dot_tn(bh_all[sl].astype(BF16), x) for sl, x in zip(sls, au)]
    kv = [_dot_tn(kh_all[sl].astype(BF16), x.astype(BF16)) for sl, x in zip(sls, V)]
    for n in range(nc):
        rp_ref[n] = (Rt[n] + rb[n][:, :W]).astype(BF16)
        y0_ref[n] = rb[n][:, W:] + _dot(Mrk[n], bdV[n])
        pm_ref[n] = (jnp.where(bd_mask, pb[n][:, :W], 0.0)
                     + jnp.where(eye, e_c[ends[n]:ends[n] + 1, :], 0.0)).astype(BF16)
        qm_ref[n] = jnp.where(bd_mask, pb[n][:, W:] + kv[n], 0.0)


def _rwkv_scan(r, kk, k_d, b_d, v, lw_d, *, tb, reverse):
    B, S, Wd = r.shape
    nblk = S // tb
    nc = tb // CHUNK
    if reverse:
        in_idx = lambda b, p, i: (b, nblk - 1 - jnp.minimum(i, nblk - 1), p)
        out_idx = lambda b, p, i: (b, nblk - 1 - jnp.maximum(i - 1, 0), p)
    else:
        in_idx = lambda b, p, i: (b, jnp.minimum(i, nblk - 1), p)
        out_idx = lambda b, p, i: (b, jnp.maximum(i - 1, 0), p)
    spec = pl.BlockSpec((1, tb, LANES), in_idx)
    return pl.pallas_call(
        functools.partial(_scan_kernel, tb=tb, reverse=reverse),
        out_shape=jax.ShapeDtypeStruct((B, S, Wd), F32),
        grid=(B, Wd // LANES, nblk + 1),
        in_specs=[spec] * 6,
        out_specs=pl.BlockSpec((1, tb, LANES), out_idx),
        scratch_shapes=[pltpu.VMEM((LANES, LANES), F32),
                        pltpu.VMEM((nc, CHUNK, LANES), BF16), pltpu.VMEM((nc, CHUNK, LANES), F32),
                        pltpu.VMEM((nc, LANES, LANES), BF16), pltpu.VMEM((nc, LANES, LANES), F32)],
        compiler_params=pltpu.CompilerParams(dimension_semantics=("parallel", "parallel", "arbitrary")),
        name="rwkv_scan_bwd" if reverse else "rwkv_scan_fwd",
    )(r, kk, k_d, b_d, v, lw_d)


def _outproj_kernel(xp_ref, xs_ref, attn_ref, yf_ref, yb_ref, bonus_ref, gate_ref, lw_ref, lb_ref, w_ref, g_ref,
                    o_ref, *, mp):
    y = yf_ref[...] + yb_ref[...]
    mean_mat = _head_ones(LANES)
    inv_n = 1.0 / RWKV_HEAD
    mu = _head_sum(y, mean_mat) * inv_n
    yc = y - mu
    var = _head_sum(yc * yc, mean_mat) * inv_n
    yn = yc * lax.rsqrt(var + LNX_EPS) * lw_ref[...] + lb_ref[...]
    rw = ((yn + bonus_ref[...]) * gate_ref[...]).astype(BF16)
    m = _dot(attn_ref[...], w_ref[0:ATTN_WIDTH, :]) + _dot(rw, w_ref[ATTN_WIDTH:, :])
    ms = jnp.mean(m * m, axis=-1, keepdims=True)
    upd = m * lax.rsqrt(ms + NORM_EPS) * g_ref[...]

    @pl.when(pl.program_id(0) < mp)
    def _():
        o_ref[...] = xp_ref[...] + upd

    @pl.when(pl.program_id(0) >= mp)
    def _():
        o_ref[...] = xs_ref[...] + upd


def _outproj(xp2d, xs2d, attn2d, yf, yb, bonus, gate, lnx_w, lnx_b, w_out_bf16, gain, *, tm):
    mp = xp2d.shape[0] // tm
    T = xp2d.shape[0] + xs2d.shape[0]
    row = lambda w: pl.BlockSpec((tm, w), lambda m: (m, 0))
    const = lambda shape: pl.BlockSpec(shape, lambda m: (0, 0))
    return pl.pallas_call(
        functools.partial(_outproj_kernel, mp=mp),
        out_shape=jax.ShapeDtypeStruct((T, D_MODEL), F32),
        grid=(T // tm,),
        in_specs=[*_row_block_specs(tm, D_MODEL, mp),
                  row(ATTN_WIDTH), row(RWKV_WIDTH), row(RWKV_WIDTH), row(RWKV_WIDTH), row(RWKV_WIDTH),
                  const((1, RWKV_WIDTH)), const((1, RWKV_WIDTH)),
                  const((D_MODEL, D_MODEL)), const((1, D_MODEL))],
        out_specs=row(D_MODEL),
        compiler_params=pltpu.CompilerParams(dimension_semantics=("parallel",)),
        name="outproj",
    )(xp2d, xs2d, attn2d, yf, yb, bonus, gate, lnx_w, lnx_b, w_out_bf16, gain)


def _ffn_kernel(x_ref, gpre_ref, wg_ref, wu_ref, wd_ref, gpost_ref, op_ref, os_ref, h_ref, acc_ref, *, mp):
    j = pl.program_id(1)

    @pl.when(j == 0)
    def _():
        x = x_ref[...]
        ms = jnp.mean(x * x, axis=-1, keepdims=True)
        h_ref[...] = (x * lax.rsqrt(ms + NORM_EPS) * gpre_ref[...]).astype(BF16)
        acc_ref[...] = jnp.zeros_like(acc_ref)

    h = h_ref[...]
    g = _dot(h, wg_ref[...])
    u = _dot(h, wu_ref[...])
    a = (g * _sigmoid(g) * u).astype(BF16)
    acc_ref[...] += _dot(a, wd_ref[...])

    def finish(o_ref):
        f = acc_ref[...]
        ms = jnp.mean(f * f, axis=-1, keepdims=True)
        o_ref[...] = x_ref[...] + f * lax.rsqrt(ms + NORM_EPS) * gpost_ref[...]

    last = j == pl.num_programs(1) - 1

    @pl.when(last & (pl.program_id(0) < mp))
    def _():
        finish(op_ref)

    @pl.when(last & (pl.program_id(0) >= mp))
    def _():
        finish(os_ref)


def _ffn(x2d, gpre, wg, wu, wd, gpost, *, tm, tf, rows_first):
    T = x2d.shape[0]
    mp = rows_first // tm
    return pl.pallas_call(
        functools.partial(_ffn_kernel, mp=mp),
        out_shape=(jax.ShapeDtypeStruct((rows_first, D_MODEL), F32),
                   jax.ShapeDtypeStruct((T - rows_first, D_MODEL), F32)),
        grid=(T // tm, D_FF // tf),
        in_specs=[pl.BlockSpec((tm, D_MODEL), lambda m, j: (m, 0)),
                  pl.BlockSpec((1, D_MODEL), lambda m, j: (0, 0)),
                  pl.BlockSpec((D_MODEL, tf), lambda m, j: (0, j)),
                  pl.BlockSpec((D_MODEL, tf), lambda m, j: (0, j)),
                  pl.BlockSpec((tf, D_MODEL), lambda m, j: (j, 0)),
                  pl.BlockSpec((1, D_MODEL), lambda m, j: (0, 0))],
        out_specs=_row_block_specs(tm, D_MODEL, mp),
        scratch_shapes=[pltpu.VMEM((tm, D_MODEL), BF16), pltpu.VMEM((tm, D_MODEL), F32)],
        compiler_params=pltpu.CompilerParams(dimension_semantics=("arbitrary", "arbitrary")),
        name="ffn",
    )(x2d, gpre, wg, wu, wd, gpost)


def _tiles(S):
    return dict(tmi=1024, tm=512, tn=512, ta=512, ts=256, tb=512, tf=512)


def _lora_blockdiag(w):
    zero = jnp.zeros_like(w[0])
    return jnp.concatenate([jnp.concatenate([w[0], zero], axis=1),
                            jnp.concatenate([zero, w[1]], axis=1)], axis=0)


def _encoder_layer(xp, xs, norm_mix_pre, norm_mix_post, w_in, w_out, lambda_q1, lambda_k1, lambda_q2, lambda_k2,
                   attn_subln, tshift_prev, tshift_next, w0, w2, a0, a2, g2, k_k, k_a, r_k, lnx_w, lnx_b,
                   norm_ffn_pre, norm_ffn_post, w_gate, w_up, w_down):
    (Bp, S, D), Bs = xp.shape, xs.shape[0]
    B = Bp + Bs
    T = B * S
    t = _tiles(S)
    row = lambda a: a.reshape(1, -1).astype(F32)
    pad_cols = RWKV_COLS_PAD - RWKV_COLS

    col_scale = jnp.where(jnp.arange(w_in.shape[1]) < ATTN_WIDTH, Q_PRESCALE, 1.0).astype(F32)
    w_in_p = jnp.pad(w_in * col_scale, ((0, 0), (0, pad_cols))).astype(BF16)
    xp2d, xs2d = xp.reshape(Bp * S, D), xs.reshape(Bs * S, D)
    qkv, z = _inproj(xp2d, xs2d, row(norm_mix_pre), w_in_p, tm=t["tmi"], tn=t["tn"])

    lam_params = jnp.stack([lambda_q1, lambda_k1, lambda_q2, lambda_k2]).astype(F32)
    attn = _diff_attention(qkv.reshape(B, S, QKV_COLS), lam_params, row(attn_subln), t=t["ta"])

    tp = jnp.pad(tshift_prev, (0, pad_cols)).reshape(1, -1)
    tn_ = jnp.pad(tshift_next, (0, pad_cols)).reshape(1, -1)
    g2p = jnp.pad(g2, ((0, GATE_LORA_PAD - GATE_LORA), (0, 0))).astype(BF16)
    (r, v, kk, k0, k1, b0, b1, lw0, lw1, bonus, gate) = _rwkv_prep(
        z.reshape(B, S, RWKV_COLS_PAD), tp, tn_, row(w0), _lora_blockdiag(w2), row(a0),
        _lora_blockdiag(a2).astype(BF16), g2p, row(k_k), row(k_a), row(r_k), ts=t["ts"])
    yf = _rwkv_scan(r, kk, k0, b0, v, lw0, tb=t["tb"], reverse=False)
    yb = _rwkv_scan(r, kk, k1, b1, v, lw1, tb=t["tb"], reverse=True)

    flat = lambda a: a.reshape(T, -1)
    x1 = _outproj(xp2d, xs2d, flat(attn), flat(yf), flat(yb), flat(bonus), flat(gate), row(lnx_w), row(lnx_b),
                  w_out.astype(BF16), row(norm_mix_post), tm=t["tm"])
    yp, ys = _ffn(x1, row(norm_ffn_pre), w_gate.astype(BF16), w_up.astype(BF16), w_down.astype(BF16),
                  row(norm_ffn_post), tm=t["tm"], tf=t["tf"], rows_first=Bp * S)
    return yp.reshape(Bp, S, D), ys.reshape(Bs, S, D)


def kernel(x_prompt, x_sample, norm_mix_pre, norm_mix_post, w_in, w_out, lambda_q1, lambda_k1, lambda_q2,
           lambda_k2, attn_subln, tshift_prev, tshift_next, w0, w2, a0, a2, g2, k_k, k_a, r_k, lnx_w, lnx_b,
           norm_ffn_pre, norm_ffn_post, w_gate, w_up, w_down):
    assert x_prompt.shape[1:] == x_sample.shape[1:], "both trunks share the sequence length"
    assert norm_mix_pre.shape[0] == 1, "single layer"
    return _encoder_layer(x_prompt, x_sample, norm_mix_pre[0], norm_mix_post[0], w_in[0], w_out[0], lambda_q1[0],
                          lambda_k1[0], lambda_q2[0], lambda_k2[0], attn_subln[0], tshift_prev[0], tshift_next[0],
                          w0[0], w2[0], a0[0], a2[0], g2[0], k_k[0], k_a[0], r_k[0], lnx_w[0], lnx_b[0],
                          norm_ffn_pre[0], norm_ffn_post[0], w_gate[0], w_up[0], w_down[0])
```

```python
import functools
import math

import jax
import jax.numpy as jnp
from jax import lax
from jax.experimental import pallas as pl
from jax.experimental.pallas import tpu as pltpu

F32 = jnp.float32
BF16 = jnp.bfloat16

D_MODEL = 2048
ATTN_HEADS = 8
ATTN_VDIM = 128
ATTN_QKDIM = 64
ATTN_WIDTH = ATTN_HEADS * ATTN_VDIM
RWKV_HEAD = 64
RWKV_HEADS = 16
RWKV_WIDTH = RWKV_HEAD * RWKV_HEADS
N_DIR = 2
LORA = 64
GATE_LORA = 160
GATE_LORA_PAD = 256
QKV_COLS = 3 * ATTN_WIDTH
RWKV_COLS = 3 * RWKV_WIDTH + 2 * N_DIR * LORA + GATE_LORA
RWKV_COLS_PAD = 3 * RWKV_WIDTH + 2 * N_DIR * LORA + GATE_LORA_PAD
D_FF = 5632
NORM_EPS = 1e-6
LNX_EPS = 64e-5
LAMBDA_INIT = 0.8 - 0.6 * math.exp(-0.3 * 0)

LANES = 128
CHUNK = 64
PAIR = LANES // RWKV_HEAD

def _dot(a, b):
    return jnp.dot(a, b, preferred_element_type=F32)


def _dot_nt(a, b):
    return lax.dot_general(a, b, (((1,), (1,)), ((), ())), preferred_element_type=F32)


def _dot_tn(a, b):
    return lax.dot_general(a, b, (((0,), (0,)), ((), ())), preferred_element_type=F32)


def _split(x):
    hi = x.astype(BF16)
    lo = (x - hi.astype(F32)).astype(BF16)
    return hi, lo


def _dot3(a, b):
    ah, al = _split(a)
    bh, bl = _split(b)
    return _dot(ah, bh) + _dot(al, bh) + _dot(ah, bl)


def _sigmoid(x):
    return 1.0 / (1.0 + jnp.exp(-x))


def _head_ones(width):
    r = lax.broadcasted_iota(jnp.int32, (width, width), 0) // RWKV_HEAD
    c = lax.broadcasted_iota(jnp.int32, (width, width), 1) // RWKV_HEAD
    return jnp.where(r == c, 1.0, 0.0).astype(BF16)


def _head_sum(x, ones, two_pass=True):
    outs = []
    for g in range(x.shape[1] // LANES):
        xg = x[:, g * LANES:(g + 1) * LANES]
        if two_pass:
            hi, lo = _split(xg)
            outs.append(_dot(hi, ones) + _dot(lo, ones))
        else:
            outs.append(_dot(xg.astype(BF16), ones))
    return jnp.concatenate(outs, axis=1)


def _row_block_specs(tm, width, mp):
    return (pl.BlockSpec((tm, width), lambda m, *_: (jnp.minimum(m, mp - 1), 0)),
            pl.BlockSpec((tm, width), lambda m, *_: (jnp.maximum(m - mp, 0), 0)))


def _inproj_kernel(xp_ref, xs_ref, g_ref, w_ref, qkv_ref, z_ref, h_ref, *, n_qkv_tiles, mp):
    m = pl.program_id(0)
    n = pl.program_id(1)

    def norm(x_ref):
        x = x_ref[...]
        ms = jnp.mean(x * x, axis=-1, keepdims=True)
        h_ref[...] = (x * lax.rsqrt(ms + NORM_EPS) * g_ref[...]).astype(BF16)

    @pl.when((n == 0) & (m < mp))
    def _():
        norm(xp_ref)

    @pl.when((n == 0) & (m >= mp))
    def _():
        norm(xs_ref)

    acc = _dot(h_ref[...], w_ref[...])

    @pl.when(n < n_qkv_tiles)
    def _():
        qkv_ref[...] = acc.astype(BF16)

    @pl.when(n >= n_qkv_tiles)
    def _():
        z_ref[...] = acc


def _inproj(xp2d, xs2d, gain, w_in_bf16, *, tm, tn):
    mp = xp2d.shape[0] // tm
    T = xp2d.shape[0] + xs2d.shape[0]
    n_cols = w_in_bf16.shape[1]
    nq = QKV_COLS // tn
    grid = (T // tm, n_cols // tn)
    return pl.pallas_call(
        functools.partial(_inproj_kernel, n_qkv_tiles=nq, mp=mp),
        out_shape=(jax.ShapeDtypeStruct((T, QKV_COLS), BF16),
                   jax.ShapeDtypeStruct((T, RWKV_COLS_PAD), F32)),
        grid=grid,
        in_specs=[*_row_block_specs(tm, D_MODEL, mp),
                  pl.BlockSpec((1, D_MODEL), lambda m, n: (0, 0)),
                  pl.BlockSpec((D_MODEL, tn), lambda m, n: (0, n))],
        out_specs=(pl.BlockSpec((tm, tn), lambda m, n: (m, jnp.minimum(n, nq - 1))),
                   pl.BlockSpec((tm, tn), lambda m, n: (m, jnp.maximum(n - nq, 0)))),
        scratch_shapes=[pltpu.VMEM((tm, D_MODEL), BF16)],
        compiler_params=pltpu.CompilerParams(dimension_semantics=("parallel", "arbitrary")),
        name="inproj",
    )(xp2d, xs2d, gain, w_in_bf16)


LOG2E = math.log2(math.e)
Q_PRESCALE = ATTN_QKDIM ** -0.5 * LOG2E
AUX = 6
NEG_BIG = -1e30
SKIP_BITS = 80.0
NORM_SLACK = 1.01
BF16_ROWS = 16
V_ROWS = ATTN_VDIM + BF16_ROWS


def _pos_aux(slope2, n, first, sign_pos, sign_one):
    row = lax.broadcasted_iota(jnp.int32, (n, LANES), 0).astype(F32)
    lane = lax.broadcasted_iota(jnp.int32, (n, LANES), 1)
    val = slope2 * row * sign_pos
    hi = val.astype(BF16).astype(F32)
    mid = (val - hi).astype(BF16).astype(F32)
    lo = (val - hi - mid).astype(BF16).astype(F32)
    out = jnp.where(lane == first, hi, 0.0)
    out = jnp.where(lane == first + 1, mid, out)
    out = jnp.where(lane == first + 2, lo, out)
    out = jnp.where((lane >= first + 3) & (lane < first + AUX), sign_one, out)
    return out.astype(BF16)


def _attn_kernel(slopes_ref, lam_ref, sub_ref, q_ref, k_ref, v_ref, o_ref,
                 ka_ref, vt_ref, kn_ref, qa_ref, bd_ref, m_ref, acc_ref, s_ref, sm_ref, p_ref, al_ref,
                 *, t, seq):
    h = pl.program_id(1)
    qi = pl.program_id(2)
    nt = seq // t
    slope2 = slopes_ref[h] * LOG2E
    lane = lax.broadcasted_iota(jnp.int32, (t, LANES), 1)
    own = (lane < ATTN_QKDIM, lane >= ATTN_QKDIM)
    aux0 = (ATTN_QKDIM, 0)

    @pl.when(qi == 0)
    def _():
        ak = [_pos_aux(slope2, t, aux0[c], 1.0, 1.0) for c in range(2)]
        ones_rows = jnp.where(lax.broadcasted_iota(jnp.int32, (BF16_ROWS, t), 0) == 0, 1.0, 0.0).astype(BF16)

        def build(j, kn):
            off = pl.multiple_of(j * t, t)
            kt = k_ref[0, pl.ds(off, t), :]
            for c in range(2):
                ka_ref[c, j] = jnp.where(own[c], kt, ak[c])
            vt_ref[j, 0:ATTN_VDIM, :] = v_ref[0, pl.ds(off, t), :].astype(F32).T.astype(BF16)
            vt_ref[j, ATTN_VDIM:, :] = ones_rows
            k32 = kt.astype(F32)
            ksq = k32 * k32
            rows = jnp.maximum(jnp.sum(jnp.where(own[0], ksq, 0.0), axis=1, keepdims=True),
                               jnp.sum(jnp.where(own[1], ksq, 0.0), axis=1, keepdims=True))
            return jnp.maximum(kn, jnp.max(rows, axis=0, keepdims=True))

        kn_ref[...] = lax.fori_loop(0, nt, build, jnp.zeros((1, 1), F32))

        for c in range(2):
            a = aux0[c]
            lane_pos = (lane >= a + 3) & (lane < a + AUX)
            lane_one = (lane >= a) & (lane < a + 3)
            pos = _pos_aux(slope2, t, a + 3, 1.0, 0.0).astype(F32)
            qa_ref[c, 0] = jnp.where(lane_one, 1.0, jnp.where(lane_pos, -pos, 0.0)).astype(BF16)
            qa_ref[c, 1] = jnp.where(lane_one, -1.0, jnp.where(lane_pos, pos, 0.0)).astype(BF16)
        rel = (lax.broadcasted_iota(jnp.int32, (t, t), 0)
               - lax.broadcasted_iota(jnp.int32, (t, t), 1)).astype(F32)
        bd_ref[...] = -slope2 * jnp.abs(rel)

    q = q_ref[0]
    zero = jnp.zeros((t, LANES), BF16)
    q_var = {c: (jnp.where(own[c], q, qa_ref[c, 0]), jnp.where(own[c], q, zero), jnp.where(own[c], q, qa_ref[c, 1]))
             for c in range(2)}

    m_ref[...] = jnp.full(m_ref.shape, NEG_BIG, F32)
    acc_ref[...] = jnp.zeros(acc_ref.shape, F32)

    def softmax_update(c, s, smax, cst):
        m_old = m_ref[c]
        m_new = jnp.maximum(m_old, smax + cst)
        alpha = jnp.exp2(m_old - m_new)
        p = jnp.exp2(s - (m_new - cst))
        m_ref[c] = m_new
        return alpha, p.astype(BF16)

    q32 = q.astype(F32)
    qsq = q32 * q32
    qn = jnp.maximum(jnp.sum(jnp.where(own[0], qsq, 0.0), axis=1, keepdims=True),
                     jnp.sum(jnp.where(own[1], qsq, 0.0), axis=1, keepdims=True))
    qk_bound = 2.0 * NORM_SLACK * jnp.sqrt(jnp.max(qn, axis=0, keepdims=True) * kn_ref[...])
    dist_needed = (qk_bound + SKIP_BITS) / slope2
    reach = jnp.clip(jnp.ceil((dist_needed - 1.0) / t), 0.0, nt - 1.0)[0, 0].astype(jnp.int32)
    n_left = jnp.minimum(qi, reach)
    n_off = n_left + jnp.minimum(nt - 1 - qi, reach)

    def key_tile(n):
        return jnp.clip(jnp.where(n < n_left, qi - 1 - n, qi + 1 + n - n_left), 0, nt - 1)

    def stage_scores(n, slot):
        j = key_tile(n)
        for c in range(2):
            qsel = jnp.where(j < qi, q_var[c][0], q_var[c][2])
            s = _dot_nt(ka_ref[c, j], qsel)
            s_ref[slot, c] = s
            sm_ref[slot, c] = jnp.max(s, axis=0, keepdims=True)

    def stage_softmax(n, slot):
        j = key_tile(n)
        cst = jnp.where(n < n_off, -slope2 * (jnp.abs(qi - j) * t).astype(F32), NEG_BIG)
        for c in range(2):
            alpha, p = softmax_update(c, s_ref[slot, c], sm_ref[slot, c], cst)
            p_ref[slot, c] = p
            al_ref[slot, c] = alpha

    def stage_values(n, slot):
        j = key_tile(n)
        for c in range(2):
            acc_ref[c] = al_ref[slot, c] * acc_ref[c] + _dot(vt_ref[j], p_ref[slot, c])

    s_diag = [_dot_nt(ka_ref[c, qi], q_var[c][1]) + bd_ref[...] for c in range(2)]
    stage_scores(0, 0)
    stage_scores(1, 1)
    for c in range(2):
        alpha, p = softmax_update(c, s_diag[c], jnp.max(s_diag[c], axis=0, keepdims=True), 0.0)
        acc_ref[c] = alpha * acc_ref[c] + _dot(vt_ref[qi], p)
    stage_softmax(0, 0)

    def pair_body(mi, carry):
        n0 = 2 + 2 * mi
        stage_scores(n0, 0)
        stage_values(n0 - 2, 0)
        stage_softmax(n0 - 1, 1)
        stage_scores(n0 + 1, 1)
        stage_values(n0 - 1, 1)
        stage_softmax(n0, 0)
        return carry

    lax.fori_loop(0, (n_off + 1) // 2, pair_body, 0)

    lp = lam_ref[...]
    lam = (jnp.exp(jnp.sum(lp[0:1] * lp[1:2], axis=-1, keepdims=True))
           - jnp.exp(jnp.sum(lp[2:3] * lp[3:4], axis=-1, keepdims=True)) + LAMBDA_INIT)
    num = [acc_ref[c, 0:ATTN_VDIM, :] for c in range(2)]
    den = [acc_ref[c, ATTN_VDIM:ATTN_VDIM + 1, :] for c in range(2)]
    o_t = num[0] / den[0] - lam * (num[1] / den[1])
    o = o_t.T
    ms = jnp.mean(o * o, axis=-1, keepdims=True)
    o = o * lax.rsqrt(ms + NORM_EPS) * sub_ref[...] * (1.0 - LAMBDA_INIT)
    o_ref[0] = o.astype(o_ref.dtype)


def _diff_attention(qkv, lam_params, subln, *, t):
    B, S, _ = qkv.shape
    nt = S // t
    slopes = jnp.asarray([2.0 ** (-8.0 * (h + 1.0) / ATTN_HEADS) for h in range(ATTN_HEADS)], F32)
    grid_spec = pltpu.PrefetchScalarGridSpec(
        num_scalar_prefetch=1,
        grid=(B, ATTN_HEADS, nt),
        in_specs=[pl.BlockSpec((4, ATTN_QKDIM), lambda b, h, i, s: (0, 0)),
                  pl.BlockSpec((1, ATTN_VDIM), lambda b, h, i, s: (0, 0)),
                  pl.BlockSpec((1, t, LANES), lambda b, h, i, s: (b, i, h)),
                  pl.BlockSpec((1, S, LANES), lambda b, h, i, s: (b, 0, ATTN_HEADS + h)),
                  pl.BlockSpec((1, S, LANES), lambda b, h, i, s: (b, 0, 2 * ATTN_HEADS + h))],
        out_specs=pl.BlockSpec((1, t, LANES), lambda b, h, i, s: (b, i, h)),
        scratch_shapes=[pltpu.VMEM((2, nt, t, LANES), BF16),
                        pltpu.VMEM((nt, V_ROWS, t), BF16),
                        pltpu.VMEM((1, 1), F32),
                        pltpu.VMEM((2, 2, t, LANES), BF16),
                        pltpu.VMEM((t, t), F32),
                        pltpu.VMEM((2, 1, t), F32),
                        pltpu.VMEM((2, V_ROWS, t), F32),
                        pltpu.VMEM((2, 2, t, t), F32),
                        pltpu.VMEM((2, 2, 1, t), F32),
                        pltpu.VMEM((2, 2, t, t), BF16),
                        pltpu.VMEM((2, 2, 1, t), F32)],
    )
    return pl.pallas_call(
        functools.partial(_attn_kernel, t=t, seq=S),
        out_shape=jax.ShapeDtypeStruct((B, S, ATTN_WIDTH), BF16),
        grid_spec=grid_spec,
        compiler_params=pltpu.CompilerParams(dimension_semantics=("parallel", "parallel", "arbitrary")),
        name="diff_attn",
    )(slopes, lam_params, subln, qkv, qkv, qkv)


def _prep_kernel(z_ref, zp_ref, zn_ref, tp_ref, tn_ref, w0_ref, w2_ref, a0_ref, a2_ref, g2_ref,
                 kk_ref, ka_ref, rk_ref,
                 r_o, v_o, kk_o, k0_o, k1_o, b0_o, b1_o, lw0_o, lw1_o, bonus_o, gate_o):
    i = pl.program_id(1)
    nt = pl.num_programs(1)
    z = z_ref[0]
    ts = z.shape[0]
    row = lax.broadcasted_iota(jnp.int32, (ts, 1), 0)
    prev_row = jnp.where(i > 0, zp_ref[0, 7:8, :], 0.0)
    next_row = jnp.where(i < nt - 1, zn_ref[0, 0:1, :], 0.0)
    z_prev = jnp.where(row == 0, prev_row, pltpu.roll(z, 1, 0))
    z_next = jnp.where(row == ts - 1, next_row, pltpu.roll(z, ts - 1, 0))
    zs = z + tp_ref[...] * (z_prev - z) + tn_ref[...] * (z_next - z)

    W = RWKV_WIDTH
    r = zs[:, 0:W]
    k = zs[:, W:2 * W]
    v = zs[:, 2 * W:3 * W]
    wd = jnp.tanh(zs[:, 3 * W:3 * W + LANES])
    ad = zs[:, 3 * W + LANES:3 * W + 2 * LANES]
    gd = zs[:, 3 * W + 2 * LANES:]

    w_log = w0_ref[...] + _dot3(wd, w2_ref[...])
    lw = -_sigmoid(w_log) * math.exp(-0.5)
    iclr = _sigmoid(a0_ref[...] + _dot(ad.astype(BF16), a2_ref[...].astype(BF16)))
    gate = _dot(_sigmoid(gd).astype(BF16), g2_ref[...].astype(BF16))

    ones = _head_ones(LANES)
    kk = k * kk_ref[...]
    kk = kk * lax.rsqrt(jnp.maximum(_head_sum(kk * kk, ones), 1e-24))
    ka = ka_ref[...]
    rk = rk_ref[...]
    bonus = jnp.zeros_like(r)
    k_outs = (k0_o, k1_o)
    b_outs = (b0_o, b1_o)
    lw_outs = (lw0_o, lw1_o)
    for d in range(N_DIR):
        a_d = iclr[:, d * W:(d + 1) * W]
        k_d = k * (1.0 + (a_d - 1.0) * ka)
        k_outs[d][0] = k_d.astype(k_outs[d].dtype)
        b_outs[d][0] = (kk * a_d).astype(b_outs[d].dtype)
        lw_outs[d][0] = lw[:, d * W:(d + 1) * W]
        bonus = bonus + _head_sum(r * k_d * rk, ones, two_pass=False) * v
    r_o[0] = r.astype(r_o.dtype)
    v_o[0] = v.astype(v_o.dtype)
    kk_o[0] = kk.astype(kk_o.dtype)
    bonus_o[0] = bonus
    gate_o[0] = gate.astype(gate_o.dtype)


def _rwkv_prep(z, tp, tn, w0, w2bd, a0, a2bd, g2p, k_k, k_a, r_k, *, ts):
    B, S, ZC = z.shape
    nt = S // ts
    hb = ts // 8
    last8 = S // 8 - 1
    const = lambda shape: pl.BlockSpec(shape, lambda b, i: (0, 0))
    out_dtypes = (BF16,) * 7 + (F32, F32, F32, BF16)
    out_spec = pl.BlockSpec((1, ts, RWKV_WIDTH), lambda b, i: (b, i, 0))
    return pl.pallas_call(
        _prep_kernel,
        out_shape=tuple(jax.ShapeDtypeStruct((B, S, RWKV_WIDTH), dt) for dt in out_dtypes),
        grid=(B, nt),
        in_specs=[pl.BlockSpec((1, ts, ZC), lambda b, i: (b, i, 0)),
                  pl.BlockSpec((1, 8, ZC), lambda b, i: (b, jnp.maximum(i * hb - 1, 0), 0)),
                  pl.BlockSpec((1, 8, ZC), lambda b, i: (b, jnp.minimum((i + 1) * hb, last8), 0)),
                  const((1, ZC)), const((1, ZC)),
                  const((1, N_DIR * RWKV_WIDTH)), const((LANES, N_DIR * RWKV_WIDTH)),
                  const((1, N_DIR * RWKV_WIDTH)), const((LANES, N_DIR * RWKV_WIDTH)),
                  const((GATE_LORA_PAD, RWKV_WIDTH)),
                  const((1, RWKV_WIDTH)), const((1, RWKV_WIDTH)), const((1, RWKV_WIDTH))],
        out_specs=(out_spec,) * 11,
        compiler_params=pltpu.CompilerParams(dimension_semantics=("parallel", "arbitrary")),
        name="rwkv_prep",
    )(z, z, z, tp, tn, w0, w2bd, a0, a2bd, g2p, k_k, k_a, r_k)


def _scan_kernel(*refs, tb):
    n_in, n_scr = 6, 5
    ins = (refs[0:n_in], refs[n_in:2 * n_in])
    outs = refs[2 * n_in:2 * n_in + 2]
    scr = (refs[2 * n_in + 2:2 * n_in + 2 + n_scr], refs[2 * n_in + 2 + n_scr:])

    @pl.when(pl.program_id(2) == 0)
    def _():
        for ref in scr[0] + scr[1]:
            ref[...] = jnp.zeros_like(ref)

    live = [_scan_stages(*ins[d], outs[d], *scr[d], tb=tb, reverse=bool(d)) for d in range(N_DIR)]
    while live:
        for g in list(live):
            if next(g, "done") == "done":
                live.remove(g)


def _scan_stages(r_ref, kk_ref, k_ref, b_ref, v_ref, lw_ref, y_ref, s_ref, rp_ref, y0_ref, pm_ref, qm_ref,
                 *, tb, reverse):
    C = CHUNK
    W = LANES
    nc = tb // C
    order = [(nc - 1 - jj) if reverse else jj for jj in range(nc)]
    sls = [slice(j * C, (j + 1) * C) for j in order]
    ends = [j * C if reverse else (j + 1) * C - 1 for j in order]

    lw = lw_ref[0]
    kk32, r32, b32, k32, v_all = (x[0].astype(F32) for x in (kk_ref, r_ref, b_ref, k_ref, v_ref))
    chain = {"s": s_ref[...], "n": 0}

    def chain_step():
        n = chain["n"]
        if n < nc:
            s16 = chain["s"].astype(BF16)
            y_ref[0, sls[n], :] = _dot(rp_ref[n], s16) + y0_ref[n]
            chain["s"] = _dot(pm_ref[n], s16) + qm_ref[n]
            chain["n"] = n + 1

    pos = lax.broadcasted_iota(jnp.int32, (tb, 1), 0) % C
    c = lw
    shift = 1
    while shift < C:
        if reverse:
            c = c + jnp.where(pos < C - shift, pltpu.roll(c, tb - shift, 0), 0.0)
        else:
            c = c + jnp.where(pos >= shift, pltpu.roll(c, shift, 0), 0.0)
        shift *= 2
    d = jnp.concatenate([c[e:e + 1, :] - c[j * C:(j + 1) * C] for j, e in
                         sorted(zip(order, ends))], axis=0)

    e_c = jnp.exp(c)
    e_nc = jnp.exp(-c)
    e_d = jnp.exp(d)
    at_all = -kk32 * jnp.exp(c - lw)
    rt_all = r32 * e_c
    bt_all = b32 * e_nc
    kt_all = k32 * e_nc
    bh_all = b32 * e_d
    kh_all = k32 * e_d
    yield

    r2 = lax.broadcasted_iota(jnp.int32, (W, W), 0)
    c2 = lax.broadcasted_iota(jnp.int32, (W, W), 1)
    bd_mask = (r2 // C) == (c2 // RWKV_HEAD)
    eye = r2 == c2
    tr = lax.broadcasted_iota(jnp.int32, (C, W), 0)
    sc = lax.broadcasted_iota(jnp.int32, (C, W), 1) % C
    strict = (sc > tr) if reverse else (sc < tr)
    incl = (sc >= tr) if reverse else (sc <= tr)

    def bd(x):
        return jnp.where(bd_mask, jnp.concatenate([x] * PAIR, axis=0), 0.0).astype(BF16)

    def bd2(xa, xb):
        return jnp.concatenate([bd(xa), bd(xb)], axis=1)

    At = [at_all[sl] for sl in sls]
    Rt = [rt_all[sl] for sl in sls]
    V = [v_all[sl] for sl in sls]
    bdV = [bd(x) for x in V]
    lhs = [jnp.concatenate([a, r], axis=0).astype(BF16) for a, r in zip(At, Rt)]
    sbk = [_dot_nt(l, jnp.concatenate([bd(bt_all[sl]), bd(kt_all[sl])], axis=0)) for l, sl in zip(lhs, sls)]
    sb = [x[:, :W] for x in sbk]
    sk = [x[:, W:] for x in sbk]
    lp = [jnp.where(strict, x[:C], 0.0) for x in sb]
    Lak = [jnp.where(strict, x[:C], 0.0).astype(BF16) for x in sk]
    Mrb = [jnp.where(incl, x[C:], 0.0).astype(BF16) for x in sb]
    Mrk = [jnp.where(incl, x[C:], 0.0).astype(BF16) for x in sk]
    chain_step()
    yield
    xa = list(At)
    xu = [_dot(l, b) for l, b in zip(Lak, bdV)]
    chain_step()
    yield
    n_dbl = C.bit_length() - 1
    for it in range(n_dbl):
        lp16 = [x.astype(BF16) for x in lp]
        if it + 1 < n_dbl:
            upd = [_dot(l16, jnp.concatenate([bd2(a, u), bd(l)], axis=1))
                   for l16, l, a, u in zip(lp16, lp, xa, xu)]
            lp = [x[:, 2 * W:] for x in upd]
        else:
            upd = [_dot(l, bd2(a, u)) for l, a, u in zip(lp16, xa, xu)]
        xa = [a + x[:, :W] for a, x in zip(xa, upd)]
        xu = [u + x[:, W:2 * W] for u, x in zip(xu, upd)]
        chain_step()
        yield
    rb = [_dot(m, bd2(a, u)) for m, a, u in zip(Mrb, xa, xu)]
    while chain["n"] < nc:
        chain_step()
    s_ref[...] = chain["s"]
    yield
    au = [jnp.concatenate([a, u], axis=1).astype(BF16) for a, u in zip(xa, xu)]
    pb = [_dot_tn(bh_all[sl].astype(BF16), x) for sl, x in zip(sls, au)]
    kv = [_dot_tn(kh_all[sl].astype(BF16), x.astype(BF16)) for sl, x in zip(sls, V)]
    yield
    for n in range(nc):
        rp_ref[n] = (Rt[n] + rb[n][:, :W]).astype(BF16)
        y0_ref[n] = rb[n][:, W:] + _dot(Mrk[n], bdV[n])
        pm_ref[n] = (jnp.where(bd_mask, pb[n][:, :W], 0.0)
                     + jnp.where(eye, e_c[ends[n]:ends[n] + 1, :], 0.0)).astype(BF16)
        qm_ref[n] = jnp.where(bd_mask, pb[n][:, W:] + kv[n], 0.0)


def _rwkv_scan(r, kk, k_dirs, b_dirs, v, lw_dirs, *, tb):
    B, S, Wd = r.shape
    nblk = S // tb
    nc = tb // CHUNK
    in_specs, out_specs = [], []
    for rev in (False, True):
        flip = (lambda n: nblk - 1 - n) if rev else (lambda n: n)
        in_idx = lambda b, p, i, flip=flip: (b, flip(jnp.minimum(i, nblk - 1)), p)
        out_idx = lambda b, p, i, flip=flip: (b, flip(jnp.maximum(i - 1, 0)), p)
        in_specs += [pl.BlockSpec((1, tb, LANES), in_idx)] * 6
        out_specs.append(pl.BlockSpec((1, tb, LANES), out_idx))
    scratch = [pltpu.VMEM((LANES, LANES), F32),
               pltpu.VMEM((nc, CHUNK, LANES), BF16), pltpu.VMEM((nc, CHUNK, LANES), F32),
               pltpu.VMEM((nc, LANES, LANES), BF16), pltpu.VMEM((nc, LANES, LANES), F32)]
    operands = [x for d in range(N_DIR) for x in (r, kk, k_dirs[d], b_dirs[d], v, lw_dirs[d])]
    return pl.pallas_call(
        functools.partial(_scan_kernel, tb=tb),
        out_shape=(jax.ShapeDtypeStruct((B, S, Wd), F32),) * N_DIR,
        grid=(B, Wd // LANES, nblk + 1),
        in_specs=in_specs,
        out_specs=tuple(out_specs),
        scratch_shapes=scratch * N_DIR,
        compiler_params=pltpu.CompilerParams(dimension_semantics=("parallel", "parallel", "arbitrary")),
        name="rwkv_scan",
    )(*operands)


def _outproj_kernel(xp_ref, xs_ref, attn_ref, yf_ref, yb_ref, bonus_ref, gate_ref, lw_ref, lb_ref, w_ref, g_ref,
                    o_ref, *, mp):
    y = yf_ref[...] + yb_ref[...]
    mean_mat = _head_ones(LANES)
    inv_n = 1.0 / RWKV_HEAD
    mu = _head_sum(y, mean_mat) * inv_n
    yc = y - mu
    var = _head_sum(yc * yc, mean_mat) * inv_n
    yn = yc * lax.rsqrt(var + LNX_EPS) * lw_ref[...] + lb_ref[...]
    rw = ((yn + bonus_ref[...]) * gate_ref[...]).astype(BF16)
    m = _dot(attn_ref[...], w_ref[0:ATTN_WIDTH, :]) + _dot(rw, w_ref[ATTN_WIDTH:, :])
    ms = jnp.mean(m * m, axis=-1, keepdims=True)
    upd = m * lax.rsqrt(ms + NORM_EPS) * g_ref[...]

    @pl.when(pl.program_id(0) < mp)
    def _():
        o_ref[...] = xp_ref[...] + upd

    @pl.when(pl.program_id(0) >= mp)
    def _():
        o_ref[...] = xs_ref[...] + upd


def _outproj(xp2d, xs2d, attn2d, yf, yb, bonus, gate, lnx_w, lnx_b, w_out_bf16, gain, *, tm):
    mp = xp2d.shape[0] // tm
    T = xp2d.shape[0] + xs2d.shape[0]
    row = lambda w: pl.BlockSpec((tm, w), lambda m: (m, 0))
    const = lambda shape: pl.BlockSpec(shape, lambda m: (0, 0))
    return pl.pallas_call(
        functools.partial(_outproj_kernel, mp=mp),
        out_shape=jax.ShapeDtypeStruct((T, D_MODEL), F32),
        grid=(T // tm,),
        in_specs=[*_row_block_specs(tm, D_MODEL, mp),
                  row(ATTN_WIDTH), row(RWKV_WIDTH), row(RWKV_WIDTH), row(RWKV_WIDTH), row(RWKV_WIDTH),
                  const((1, RWKV_WIDTH)), const((1, RWKV_WIDTH)),
                  const((D_MODEL, D_MODEL)), const((1, D_MODEL))],
        out_specs=row(D_MODEL),
        compiler_params=pltpu.CompilerParams(dimension_semantics=("parallel",)),
        name="outproj",
    )(xp2d, xs2d, attn2d, yf, yb, bonus, gate, lnx_w, lnx_b, w_out_bf16, gain)


def _ffn_kernel(x_ref, gpre_ref, wg_ref, wu_ref, wd_ref, gpost_ref, op_ref, os_ref, h_ref, acc_ref, *, mp):
    j = pl.program_id(1)

    @pl.when(j == 0)
    def _():
        x = x_ref[...]
        ms = jnp.mean(x * x, axis=-1, keepdims=True)
        h_ref[...] = (x * lax.rsqrt(ms + NORM_EPS) * gpre_ref[...]).astype(BF16)
        acc_ref[...] = jnp.zeros_like(acc_ref)

    h = h_ref[...]
    g = _dot(h, wg_ref[...])
    u = _dot(h, wu_ref[...])
    a = (g * _sigmoid(g) * u).astype(BF16)
    acc_ref[...] += _dot(a, wd_ref[...])

    def finish(o_ref):
        f = acc_ref[...]
        ms = jnp.mean(f * f, axis=-1, keepdims=True)
        o_ref[...] = x_ref[...] + f * lax.rsqrt(ms + NORM_EPS) * gpost_ref[...]

    last = j == pl.num_programs(1) - 1

    @pl.when(last & (pl.program_id(0) < mp))
    def _():
        finish(op_ref)

    @pl.when(last & (pl.program_id(0) >= mp))
    def _():
        finish(os_ref)


def _ffn(x2d, gpre, wg, wu, wd, gpost, *, tm, tf, rows_first):
    T = x2d.shape[0]
    mp = rows_first // tm
    return pl.pallas_call(
        functools.partial(_ffn_kernel, mp=mp),
        out_shape=(jax.ShapeDtypeStruct((rows_first, D_MODEL), F32),
                   jax.ShapeDtypeStruct((T - rows_first, D_MODEL), F32)),
        grid=(T // tm, D_FF // tf),
        in_specs=[pl.BlockSpec((tm, D_MODEL), lambda m, j: (m, 0)),
                  pl.BlockSpec((1, D_MODEL), lambda m, j: (0, 0)),
                  pl.BlockSpec((D_MODEL, tf), lambda m, j: (0, j)),
                  pl.BlockSpec((D_MODEL, tf), lambda m, j: (0, j)),
                  pl.BlockSpec((tf, D_MODEL), lambda m, j: (j, 0)),
                  pl.BlockSpec((1, D_MODEL), lambda m, j: (0, 0))],
        out_specs=_row_block_specs(tm, D_MODEL, mp),
        scratch_shapes=[pltpu.VMEM((tm, D_MODEL), BF16), pltpu.VMEM((tm, D_MODEL), F32)],
        compiler_params=pltpu.CompilerParams(dimension_semantics=("arbitrary", "arbitrary")),
        name="ffn",
    )(x2d, gpre, wg, wu, wd, gpost)


def _tiles(S):
    return dict(tmi=1024, tm=512, tn=512, ta=512, ts=256, tb=512, tf=512)


def _lora_blockdiag(w):
    zero = jnp.zeros_like(w[0])
    return jnp.concatenate([jnp.concatenate([w[0], zero], axis=1),
                            jnp.concatenate([zero, w[1]], axis=1)], axis=0)


def _encoder_layer(xp, xs, norm_mix_pre, norm_mix_post, w_in, w_out, lambda_q1, lambda_k1, lambda_q2, lambda_k2,
                   attn_subln, tshift_prev, tshift_next, w0, w2, a0, a2, g2, k_k, k_a, r_k, lnx_w, lnx_b,
                   norm_ffn_pre, norm_ffn_post, w_gate, w_up, w_down):
    (Bp, S, D), Bs = xp.shape, xs.shape[0]
    B = Bp + Bs
    T = B * S
    t = _tiles(S)
    row = lambda a: a.reshape(1, -1).astype(F32)
    pad_cols = RWKV_COLS_PAD - RWKV_COLS

    col_scale = jnp.where(jnp.arange(w_in.shape[1]) < ATTN_WIDTH, Q_PRESCALE, 1.0).astype(F32)
    w_in_p = jnp.pad(w_in * col_scale, ((0, 0), (0, pad_cols))).astype(BF16)
    xp2d, xs2d = xp.reshape(Bp * S, D), xs.reshape(Bs * S, D)
    qkv, z = _inproj(xp2d, xs2d, row(norm_mix_pre), w_in_p, tm=t["tmi"], tn=t["tn"])

    lam_params = jnp.stack([lambda_q1, lambda_k1, lambda_q2, lambda_k2]).astype(F32)
    attn = _diff_attention(qkv.reshape(B, S, QKV_COLS), lam_params, row(attn_subln), t=t["ta"])

    tp = jnp.pad(tshift_prev, (0, pad_cols)).reshape(1, -1)
    tn_ = jnp.pad(tshift_next, (0, pad_cols)).reshape(1, -1)
    g2p = jnp.pad(g2, ((0, GATE_LORA_PAD - GATE_LORA), (0, 0))).astype(BF16)
    (r, v, kk, k0, k1, b0, b1, lw0, lw1, bonus, gate) = _rwkv_prep(
        z.reshape(B, S, RWKV_COLS_PAD), tp, tn_, row(w0), _lora_blockdiag(w2), row(a0),
        _lora_blockdiag(a2).astype(BF16), g2p, row(k_k), row(k_a), row(r_k), ts=t["ts"])
    yf, yb = _rwkv_scan(r, kk, (k0, k1), (b0, b1), v, (lw0, lw1), tb=t["tb"])

    flat = lambda a: a.reshape(T, -1)
    x1 = _outproj(xp2d, xs2d, flat(attn), flat(yf), flat(yb), flat(bonus), flat(gate), row(lnx_w), row(lnx_b),
                  w_out.astype(BF16), row(norm_mix_post), tm=t["tm"])
    yp, ys = _ffn(x1, row(norm_ffn_pre), w_gate.astype(BF16), w_up.astype(BF16), w_down.astype(BF16),
                  row(norm_ffn_post), tm=t["tm"], tf=t["tf"], rows_first=Bp * S)
    return yp.reshape(Bp, S, D), ys.reshape(Bs, S, D)


def kernel(x_prompt, x_sample, norm_mix_pre, norm_mix_post, w_in, w_out, lambda_q1, lambda_k1, lambda_q2,
           lambda_k2, attn_subln, tshift_prev, tshift_next, w0, w2, a0, a2, g2, k_k, k_a, r_k, lnx_w, lnx_b,
           norm_ffn_pre, norm_ffn_post, w_gate, w_up, w_down):
    assert x_prompt.shape[1:] == x_sample.shape[1:], "both trunks share the sequence length"
    assert norm_mix_pre.shape[0] == 1, "single layer"
    return _encoder_layer(x_prompt, x_sample, norm_mix_pre[0], norm_mix_post[0], w_in[0], w_out[0], lambda_q1[0],
                          lambda_k1[0], lambda_q2[0], lambda_k2[0], attn_subln[0], tshift_prev[0], tshift_next[0],
                          w0[0], w2[0], a0[0], a2[0], g2[0], k_k[0], k_a[0], r_k[0], lnx_w[0], lnx_b[0],
                          norm_ffn_pre[0], norm_ffn_post[0], w_gate[0], w_up[0], w_down[0])
```

```python
import functools
import math

import jax
import jax.numpy as jnp
from jax import lax
from jax.experimental import pallas as pl
from jax.experimental.pallas import tpu as pltpu

F32 = jnp.float32
BF16 = jnp.bfloat16

D_MODEL = 2048
ATTN_HEADS = 8
ATTN_VDIM = 128
ATTN_QKDIM = 64
ATTN_WIDTH = ATTN_HEADS * ATTN_VDIM
RWKV_HEAD = 64
RWKV_HEADS = 16
RWKV_WIDTH = RWKV_HEAD * RWKV_HEADS
N_DIR = 2
LORA = 64
GATE_LORA = 160
GATE_LORA_PAD = 256
QKV_COLS = 3 * ATTN_WIDTH
RWKV_COLS = 3 * RWKV_WIDTH + 2 * N_DIR * LORA + GATE_LORA
RWKV_COLS_PAD = 3 * RWKV_WIDTH + 2 * N_DIR * LORA + GATE_LORA_PAD
D_FF = 5632
NORM_EPS = 1e-6
LNX_EPS = 64e-5
LAMBDA_INIT = 0.8 - 0.6 * math.exp(-0.3 * 0)

LANES = 128
CHUNK = 64
PAIR = LANES // RWKV_HEAD

def _dot(a, b):
    return jnp.dot(a, b, preferred_element_type=F32)


def _dot_nt(a, b):
    return lax.dot_general(a, b, (((1,), (1,)), ((), ())), preferred_element_type=F32)


def _dot_tn(a, b):
    return lax.dot_general(a, b, (((0,), (0,)), ((), ())), preferred_element_type=F32)


def _split(x):
    hi = x.astype(BF16)
    lo = (x - hi.astype(F32)).astype(BF16)
    return hi, lo


def _dot3(a, b):
    ah, al = _split(a)
    bh, bl = _split(b)
    return _dot(ah, bh) + _dot(al, bh) + _dot(ah, bl)


def _sigmoid(x):
    return 1.0 / (1.0 + jnp.exp(-x))


def _head_ones(width):
    r = lax.broadcasted_iota(jnp.int32, (width, width), 0) // RWKV_HEAD
    c = lax.broadcasted_iota(jnp.int32, (width, width), 1) // RWKV_HEAD
    return jnp.where(r == c, 1.0, 0.0).astype(BF16)


def _head_sum(x, ones, two_pass=True):
    outs = []
    for g in range(x.shape[1] // LANES):
        xg = x[:, g * LANES:(g + 1) * LANES]
        if two_pass:
            hi, lo = _split(xg)
            outs.append(_dot(hi, ones) + _dot(lo, ones))
        else:
            outs.append(_dot(xg.astype(BF16), ones))
    return jnp.concatenate(outs, axis=1)


def _row_block_specs(tm, width, mp):
    return (pl.BlockSpec((tm, width), lambda m, *_: (jnp.minimum(m, mp - 1), 0)),
            pl.BlockSpec((tm, width), lambda m, *_: (jnp.maximum(m - mp, 0), 0)))


def _inproj_kernel(xp_ref, xs_ref, g_ref, w_ref, qkv_ref, z_ref, h_ref, *, mp):
    m = pl.program_id(0)

    def norm(x_ref):
        x = x_ref[...]
        ms = jnp.mean(x * x, axis=-1, keepdims=True)
        h_ref[...] = (x * lax.rsqrt(ms + NORM_EPS) * g_ref[...]).astype(BF16)

    @pl.when(m < mp)
    def _():
        norm(xp_ref)

    @pl.when(m >= mp)
    def _():
        norm(xs_ref)

    h = h_ref[...]
    qkv_ref[...] = _dot(h, w_ref[:, 0:QKV_COLS]).astype(BF16)
    z_ref[...] = _dot(h, w_ref[:, QKV_COLS:])


def _inproj(xp2d, xs2d, gain, w_in_bf16, *, tm):
    mp = xp2d.shape[0] // tm
    T = xp2d.shape[0] + xs2d.shape[0]
    n_cols = w_in_bf16.shape[1]
    return pl.pallas_call(
        functools.partial(_inproj_kernel, mp=mp),
        out_shape=(jax.ShapeDtypeStruct((T, QKV_COLS), BF16),
                   jax.ShapeDtypeStruct((T, RWKV_COLS_PAD), F32)),
        grid=(T // tm,),
        in_specs=[*_row_block_specs(tm, D_MODEL, mp),
                  pl.BlockSpec((1, D_MODEL), lambda m: (0, 0)),
                  pl.BlockSpec((D_MODEL, n_cols), lambda m: (0, 0), pipeline_mode=pl.Buffered(1))],
        out_specs=(pl.BlockSpec((tm, QKV_COLS), lambda m: (m, 0)),
                   pl.BlockSpec((tm, RWKV_COLS_PAD), lambda m: (m, 0))),
        scratch_shapes=[pltpu.VMEM((tm, D_MODEL), BF16)],
        compiler_params=pltpu.CompilerParams(dimension_semantics=("parallel",)),
        name="inproj",
    )(xp2d, xs2d, gain, w_in_bf16)


LOG2E = math.log2(math.e)
Q_PRESCALE = ATTN_QKDIM ** -0.5 * LOG2E
AUX = 6
NEG_BIG = -1e30
SKIP_BITS = 80.0
NORM_SLACK = 1.01
FAST_BITS = -1.0
BF16_ROWS = 16
V_ROWS = ATTN_VDIM + BF16_ROWS


def _pos_aux(slope2, n, first, sign_pos, sign_one):
    row = lax.broadcasted_iota(jnp.int32, (n, LANES), 0).astype(F32)
    lane = lax.broadcasted_iota(jnp.int32, (n, LANES), 1)
    val = slope2 * row * sign_pos
    hi = val.astype(BF16).astype(F32)
    mid = (val - hi).astype(BF16).astype(F32)
    lo = (val - hi - mid).astype(BF16).astype(F32)
    out = jnp.where(lane == first, hi, 0.0)
    out = jnp.where(lane == first + 1, mid, out)
    out = jnp.where(lane == first + 2, lo, out)
    out = jnp.where((lane >= first + 3) & (lane < first + AUX), sign_one, out)
    return out.astype(BF16)


def _attn_kernel(slopes_ref, lam_ref, sub_ref, q_ref, k_ref, v_ref, o_ref,
                 ka_ref, vt_ref, kn_ref, qa_ref, bd_ref, m_ref, acc_ref, s_ref, sm_ref, p_ref, al_ref,
                 *, t, seq):
    h = pl.program_id(1)
    qi = pl.program_id(2)
    nt = seq // t
    slope2 = slopes_ref[h] * LOG2E
    lane = lax.broadcasted_iota(jnp.int32, (t, LANES), 1)
    own = (lane < ATTN_QKDIM, lane >= ATTN_QKDIM)
    aux0 = (ATTN_QKDIM, 0)

    @pl.when(qi == 0)
    def _():
        ak = [_pos_aux(slope2, t, aux0[c], 1.0, 1.0) for c in range(2)]
        ones_rows = jnp.where(lax.broadcasted_iota(jnp.int32, (BF16_ROWS, t), 0) == 0, 1.0, 0.0).astype(BF16)

        def build(j, kn):
            off = pl.multiple_of(j * t, t)
            kt = k_ref[0, pl.ds(off, t), :]
            for c in range(2):
                ka_ref[c, j] = jnp.where(own[c], kt, ak[c])
            vt_ref[j, 0:ATTN_VDIM, :] = v_ref[0, pl.ds(off, t), :].astype(F32).T.astype(BF16)
            vt_ref[j, ATTN_VDIM:, :] = ones_rows
            k32 = kt.astype(F32)
            ksq = k32 * k32
            rows = jnp.maximum(jnp.sum(jnp.where(own[0], ksq, 0.0), axis=1, keepdims=True),
                               jnp.sum(jnp.where(own[1], ksq, 0.0), axis=1, keepdims=True))
            return jnp.maximum(kn, jnp.max(rows, axis=0, keepdims=True))

        kn_ref[...] = lax.fori_loop(0, nt, build, jnp.zeros((1, 1), F32))

        for c in range(2):
            a = aux0[c]
            lane_pos = (lane >= a + 3) & (lane < a + AUX)
            lane_one = (lane >= a) & (lane < a + 3)
            pos = _pos_aux(slope2, t, a + 3, 1.0, 0.0).astype(F32)
            qa_ref[c, 0] = jnp.where(lane_one, 1.0, jnp.where(lane_pos, -pos, 0.0)).astype(BF16)
            qa_ref[c, 1] = jnp.where(lane_one, -1.0, jnp.where(lane_pos, pos, 0.0)).astype(BF16)
        rel = (lax.broadcasted_iota(jnp.int32, (t, t), 0)
               - lax.broadcasted_iota(jnp.int32, (t, t), 1)).astype(F32)
        bd_ref[...] = -slope2 * jnp.abs(rel)

    q = q_ref[0]
    zero = jnp.zeros((t, LANES), BF16)
    q_var = {c: (jnp.where(own[c], q, qa_ref[c, 0]), jnp.where(own[c], q, zero), jnp.where(own[c], q, qa_ref[c, 1]))
             for c in range(2)}

    acc_ref[...] = jnp.zeros(acc_ref.shape, F32)

    q32 = q.astype(F32)
    qsq = q32 * q32
    qn = jnp.maximum(jnp.sum(jnp.where(own[0], qsq, 0.0), axis=1, keepdims=True),
                     jnp.sum(jnp.where(own[1], qsq, 0.0), axis=1, keepdims=True))
    qk_bound = 2.0 * NORM_SLACK * jnp.sqrt(jnp.max(qn, axis=0, keepdims=True) * kn_ref[...])
    dist_needed = (qk_bound + SKIP_BITS) / slope2
    reach = jnp.clip(jnp.ceil((dist_needed - 1.0) / t), 0.0, nt - 1.0)[0, 0].astype(jnp.int32)
    n_left = jnp.minimum(qi, reach)
    n_off = n_left + jnp.minimum(nt - 1 - qi, reach)

    own_rows = [jnp.where(lax.broadcasted_iota(jnp.int32, (8, LANES), 1) // ATTN_QKDIM == c, 1.0, 0.0).astype(BF16)
                for c in range(2)]
    q_norm2 = [_dot_nt(own_rows[c], qsq.astype(BF16))[0:1, :] for c in range(2)]
    row_bound = [NORM_SLACK * jnp.sqrt(q_norm2[c] * kn_ref[...]) for c in range(2)]
    use_fixed = qk_bound[0, 0] <= FAST_BITS

    def key_tile(n):
        return jnp.clip(jnp.where(n < n_left, qi - 1 - n, qi + 1 + n - n_left), 0, nt - 1)

    def run(fixed):
        def probs(c, s, smax, cst):
            if fixed:
                return None, jnp.exp2(s - (row_bound[c] - cst)).astype(BF16)
            m_old = m_ref[c]
            m_new = jnp.maximum(m_old, smax + cst)
            m_ref[c] = m_new
            return jnp.exp2(m_old - m_new), jnp.exp2(s - (m_new - cst)).astype(BF16)

        def accumulate(c, alpha, vt, p):
            if fixed:
                acc_ref[c] += _dot(vt, p)
            else:
                acc_ref[c] = alpha * acc_ref[c] + _dot(vt, p)

        def stage_scores(n, slot):
            j = key_tile(n)
            for c in range(2):
                qsel = jnp.where(j < qi, q_var[c][0], q_var[c][2])
                s = _dot_nt(ka_ref[c, j], qsel)
                s_ref[slot, c] = s
                if not fixed:
                    sm_ref[slot, c] = jnp.max(s, axis=0, keepdims=True)

        def stage_softmax(n, slot):
            j = key_tile(n)
            cst = jnp.where(n < n_off, -slope2 * (jnp.abs(qi - j) * t).astype(F32), NEG_BIG)
            for c in range(2):
                alpha, p = probs(c, s_ref[slot, c], None if fixed else sm_ref[slot, c], cst)
                p_ref[slot, c] = p
                if not fixed:
                    al_ref[slot, c] = alpha

        def stage_values(n, slot):
            j = key_tile(n)
            for c in range(2):
                accumulate(c, None if fixed else al_ref[slot, c], vt_ref[j], p_ref[slot, c])

        if not fixed:
            m_ref[...] = jnp.full(m_ref.shape, NEG_BIG, F32)
        s_diag = [_dot_nt(ka_ref[c, qi], q_var[c][1]) + bd_ref[...] for c in range(2)]
        stage_scores(0, 0)
        stage_scores(1, 1)
        for c in range(2):
            smax = None if fixed else jnp.max(s_diag[c], axis=0, keepdims=True)
            alpha, p = probs(c, s_diag[c], smax, 0.0)
            accumulate(c, alpha, vt_ref[qi], p)
        stage_softmax(0, 0)

        def pair_body(mi, carry):
            n0 = 2 + 2 * mi
            stage_scores(n0, 0)
            stage_values(n0 - 2, 0)
            stage_softmax(n0 - 1, 1)
            stage_scores(n0 + 1, 1)
            stage_values(n0 - 1, 1)
            stage_softmax(n0, 0)
            return carry

        lax.fori_loop(0, (n_off + 1) // 2, pair_body, 0)

    @pl.when(use_fixed)
    def _():
        run(True)

    @pl.when(jnp.logical_not(use_fixed))
    def _():
        run(False)

    lp = lam_ref[...]
    lam = (jnp.exp(jnp.sum(lp[0:1] * lp[1:2], axis=-1, keepdims=True))
           - jnp.exp(jnp.sum(lp[2:3] * lp[3:4], axis=-1, keepdims=True)) + LAMBDA_INIT)
    num = [acc_ref[c, 0:ATTN_VDIM, :] for c in range(2)]
    den = [acc_ref[c, ATTN_VDIM:ATTN_VDIM + 1, :] for c in range(2)]
    o_t = num[0] / den[0] - lam * (num[1] / den[1])
    o = o_t.T
    ms = jnp.mean(o * o, axis=-1, keepdims=True)
    o = o * lax.rsqrt(ms + NORM_EPS) * sub_ref[...] * (1.0 - LAMBDA_INIT)
    o_ref[0] = o.astype(o_ref.dtype)


def _diff_attention(qkv, lam_params, subln, *, t):
    B, S, _ = qkv.shape
    nt = S // t
    slopes = jnp.asarray([2.0 ** (-8.0 * (h + 1.0) / ATTN_HEADS) for h in range(ATTN_HEADS)], F32)
    grid_spec = pltpu.PrefetchScalarGridSpec(
        num_scalar_prefetch=1,
        grid=(B, ATTN_HEADS, nt),
        in_specs=[pl.BlockSpec((4, ATTN_QKDIM), lambda b, h, i, s: (0, 0)),
                  pl.BlockSpec((1, ATTN_VDIM), lambda b, h, i, s: (0, 0)),
                  pl.BlockSpec((1, t, LANES), lambda b, h, i, s: (b, i, h)),
                  pl.BlockSpec((1, S, LANES), lambda b, h, i, s: (b, 0, ATTN_HEADS + h)),
                  pl.BlockSpec((1, S, LANES), lambda b, h, i, s: (b, 0, 2 * ATTN_HEADS + h))],
        out_specs=pl.BlockSpec((1, t, LANES), lambda b, h, i, s: (b, i, h)),
        scratch_shapes=[pltpu.VMEM((2, nt, t, LANES), BF16),
                        pltpu.VMEM((nt, V_ROWS, t), BF16),
                        pltpu.VMEM((1, 1), F32),
                        pltpu.VMEM((2, 2, t, LANES), BF16),
                        pltpu.VMEM((t, t), F32),
                        pltpu.VMEM((2, 1, t), F32),
                        pltpu.VMEM((2, V_ROWS, t), F32),
                        pltpu.VMEM((2, 2, t, t), F32),
                        pltpu.VMEM((2, 2, 1, t), F32),
                        pltpu.VMEM((2, 2, t, t), BF16),
                        pltpu.VMEM((2, 2, 1, t), F32)],
    )
    return pl.pallas_call(
        functools.partial(_attn_kernel, t=t, seq=S),
        out_shape=jax.ShapeDtypeStruct((B, S, ATTN_WIDTH), BF16),
        grid_spec=grid_spec,
        compiler_params=pltpu.CompilerParams(dimension_semantics=("parallel", "parallel", "arbitrary")),
        name="diff_attn",
    )(slopes, lam_params, subln, qkv, qkv, qkv)


def _prep_kernel(z_ref, zp_ref, zn_ref, tp_ref, tn_ref, w0_ref, w2_ref, a0_ref, a2_ref, g2_ref,
                 kk_ref, ka_ref, rk_ref,
                 r_o, v_o, kk_o, k0_o, k1_o, b0_o, b1_o, lw0_o, lw1_o, bonus_o, gate_o):
    i = pl.program_id(1)
    nt = pl.num_programs(1)
    z = z_ref[0]
    ts = z.shape[0]
    row = lax.broadcasted_iota(jnp.int32, (ts, 1), 0)
    prev_row = jnp.where(i > 0, zp_ref[0, 7:8, :], 0.0)
    next_row = jnp.where(i < nt - 1, zn_ref[0, 0:1, :], 0.0)
    z_prev = jnp.where(row == 0, prev_row, pltpu.roll(z, 1, 0))
    z_next = jnp.where(row == ts - 1, next_row, pltpu.roll(z, ts - 1, 0))
    zs = z + tp_ref[...] * (z_prev - z) + tn_ref[...] * (z_next - z)

    W = RWKV_WIDTH
    r = zs[:, 0:W]
    k = zs[:, W:2 * W]
    v = zs[:, 2 * W:3 * W]
    wd = jnp.tanh(zs[:, 3 * W:3 * W + LANES])
    ad = zs[:, 3 * W + LANES:3 * W + 2 * LANES]
    gd = zs[:, 3 * W + 2 * LANES:]

    w_log = w0_ref[...] + _dot3(wd, w2_ref[...])
    lw = -_sigmoid(w_log) * math.exp(-0.5)
    iclr = _sigmoid(a0_ref[...] + _dot(ad.astype(BF16), a2_ref[...].astype(BF16)))
    gate = _dot(_sigmoid(gd).astype(BF16), g2_ref[...].astype(BF16))

    ones = _head_ones(LANES)
    kk = k * kk_ref[...]
    kk = kk * lax.rsqrt(jnp.maximum(_head_sum(kk * kk, ones), 1e-24))
    ka = ka_ref[...]
    rk = rk_ref[...]
    bonus = jnp.zeros_like(r)
    k_outs = (k0_o, k1_o)
    b_outs = (b0_o, b1_o)
    lw_outs = (lw0_o, lw1_o)
    for d in range(N_DIR):
        a_d = iclr[:, d * W:(d + 1) * W]
        k_d = k * (1.0 + (a_d - 1.0) * ka)
        k_outs[d][0] = k_d.astype(k_outs[d].dtype)
        b_outs[d][0] = (kk * a_d).astype(b_outs[d].dtype)
        lw_outs[d][0] = lw[:, d * W:(d + 1) * W]
        bonus = bonus + _head_sum(r * k_d * rk, ones, two_pass=False) * v
    r_o[0] = r.astype(r_o.dtype)
    v_o[0] = v.astype(v_o.dtype)
    kk_o[0] = kk.astype(kk_o.dtype)
    bonus_o[0] = bonus
    gate_o[0] = gate.astype(gate_o.dtype)


def _rwkv_prep(z, tp, tn, w0, w2bd, a0, a2bd, g2p, k_k, k_a, r_k, *, ts):
    B, S, ZC = z.shape
    nt = S // ts
    hb = ts // 8
    last8 = S // 8 - 1
    const = lambda shape: pl.BlockSpec(shape, lambda b, i: (0, 0))
    out_dtypes = (BF16,) * 7 + (F32, F32, F32, BF16)
    out_spec = pl.BlockSpec((1, ts, RWKV_WIDTH), lambda b, i: (b, i, 0))
    return pl.pallas_call(
        _prep_kernel,
        out_shape=tuple(jax.ShapeDtypeStruct((B, S, RWKV_WIDTH), dt) for dt in out_dtypes),
        grid=(B, nt),
        in_specs=[pl.BlockSpec((1, ts, ZC), lambda b, i: (b, i, 0)),
                  pl.BlockSpec((1, 8, ZC), lambda b, i: (b, jnp.maximum(i * hb - 1, 0), 0)),
                  pl.BlockSpec((1, 8, ZC), lambda b, i: (b, jnp.minimum((i + 1) * hb, last8), 0)),
                  const((1, ZC)), const((1, ZC)),
                  const((1, N_DIR * RWKV_WIDTH)), const((LANES, N_DIR * RWKV_WIDTH)),
                  const((1, N_DIR * RWKV_WIDTH)), const((LANES, N_DIR * RWKV_WIDTH)),
                  const((GATE_LORA_PAD, RWKV_WIDTH)),
                  const((1, RWKV_WIDTH)), const((1, RWKV_WIDTH)), const((1, RWKV_WIDTH))],
        out_specs=(out_spec,) * 11,
        compiler_params=pltpu.CompilerParams(dimension_semantics=("parallel", "arbitrary")),
        name="rwkv_prep",
    )(z, z, z, tp, tn, w0, w2bd, a0, a2bd, g2p, k_k, k_a, r_k)


def _scan_kernel(*refs, tb):
    n_in, n_scr = 6, 5
    ins = (refs[0:n_in], refs[n_in:2 * n_in])
    outs = refs[2 * n_in:2 * n_in + 2]
    scr = (refs[2 * n_in + 2:2 * n_in + 2 + n_scr], refs[2 * n_in + 2 + n_scr:])

    @pl.when(pl.program_id(2) == 0)
    def _():
        for ref in scr[0] + scr[1]:
            ref[...] = jnp.zeros_like(ref)

    live = [_scan_stages(*ins[d], outs[d], *scr[d], tb=tb, reverse=bool(d)) for d in range(N_DIR)]
    while live:
        for g in list(live):
            if next(g, "done") == "done":
                live.remove(g)


def _scan_stages(r_ref, kk_ref, k_ref, b_ref, v_ref, lw_ref, y_ref, s_ref, rp_ref, y0_ref, pm_ref, qm_ref,
                 *, tb, reverse):
    C = CHUNK
    W = LANES
    nc = tb // C
    order = [(nc - 1 - jj) if reverse else jj for jj in range(nc)]
    sls = [slice(j * C, (j + 1) * C) for j in order]
    ends = [j * C if reverse else (j + 1) * C - 1 for j in order]

    lw = lw_ref[0]
    kk32, r32, b32, k32, v_all = (x[0].astype(F32) for x in (kk_ref, r_ref, b_ref, k_ref, v_ref))
    chain = {"s": s_ref[...], "n": 0}

    def chain_step():
        n = chain["n"]
        if n < nc:
            s16 = chain["s"].astype(BF16)
            y_ref[0, sls[n], :] = _dot(rp_ref[n], s16) + y0_ref[n]
            chain["s"] = _dot(pm_ref[n], s16) + qm_ref[n]
            chain["n"] = n + 1

    pos = lax.broadcasted_iota(jnp.int32, (tb, 1), 0) % C
    c = lw
    shift = 1
    while shift < C:
        if reverse:
            c = c + jnp.where(pos < C - shift, pltpu.roll(c, tb - shift, 0), 0.0)
        else:
            c = c + jnp.where(pos >= shift, pltpu.roll(c, shift, 0), 0.0)
        shift *= 2
    d = jnp.concatenate([c[e:e + 1, :] - c[j * C:(j + 1) * C] for j, e in
                         sorted(zip(order, ends))], axis=0)

    e_c = jnp.exp(c)
    e_nc = jnp.exp(-c)
    e_d = jnp.exp(d)
    at_all = -kk32 * jnp.exp(c - lw)
    rt_all = r32 * e_c
    bt_all = b32 * e_nc
    kt_all = k32 * e_nc
    bh_all = b32 * e_d
    kh_all = k32 * e_d
    yield

    r2 = lax.broadcasted_iota(jnp.int32, (W, W), 0)
    c2 = lax.broadcasted_iota(jnp.int32, (W, W), 1)
    bd_mask = (r2 // C) == (c2 // RWKV_HEAD)
    eye = r2 == c2
    tr = lax.broadcasted_iota(jnp.int32, (C, W), 0)
    sc = lax.broadcasted_iota(jnp.int32, (C, W), 1) % C
    strict = (sc > tr) if reverse else (sc < tr)
    incl = (sc >= tr) if reverse else (sc <= tr)

    def bd(x):
        return jnp.where(bd_mask, jnp.concatenate([x] * PAIR, axis=0), 0.0).astype(BF16)

    def bd2(xa, xb):
        return jnp.concatenate([bd(xa), bd(xb)], axis=1)

    At = [at_all[sl] for sl in sls]
    Rt = [rt_all[sl] for sl in sls]
    V = [v_all[sl] for sl in sls]
    bdV = [bd(x) for x in V]
    lhs = [jnp.concatenate([a, r], axis=0).astype(BF16) for a, r in zip(At, Rt)]
    sbk = [_dot_nt(l, jnp.concatenate([bd(bt_all[sl]), bd(kt_all[sl])], axis=0)) for l, sl in zip(lhs, sls)]
    sb = [x[:, :W] for x in sbk]
    sk = [x[:, W:] for x in sbk]
    lp = [jnp.where(strict, x[:C], 0.0) for x in sb]
    Lak = [jnp.where(strict, x[:C], 0.0).astype(BF16) for x in sk]
    Mrb = [jnp.where(incl, x[C:], 0.0).astype(BF16) for x in sb]
    Mrk = [jnp.where(incl, x[C:], 0.0).astype(BF16) for x in sk]
    chain_step()
    yield
    xa = list(At)
    xu = [_dot(l, b) for l, b in zip(Lak, bdV)]
    chain_step()
    yield
    n_dbl = C.bit_length() - 1
    for it in range(n_dbl):
        lp16 = [x.astype(BF16) for x in lp]
        if it + 1 < n_dbl:
            upd = [_dot(l16, jnp.concatenate([bd2(a, u), bd(l)], axis=1))
                   for l16, l, a, u in zip(lp16, lp, xa, xu)]
            lp = [x[:, 2 * W:] for x in upd]
        else:
            upd = [_dot(l, bd2(a, u)) for l, a, u in zip(lp16, xa, xu)]
        xa = [a + x[:, :W] for a, x in zip(xa, upd)]
        xu = [u + x[:, W:2 * W] for u, x in zip(xu, upd)]
        chain_step()
        yield
    rb = [_dot(m, bd2(a, u)) for m, a, u in zip(Mrb, xa, xu)]
    while chain["n"] < nc:
        chain_step()
    s_ref[...] = chain["s"]
    yield
    au = [jnp.concatenate([a, u], axis=1).astype(BF16) for a, u in zip(xa, xu)]
    pb = [_dot_tn(bh_all[sl].astype(BF16), x) for sl, x in zip(sls, au)]
    kv = [_dot_tn(kh_all[sl].astype(BF16), x.astype(BF16)) for sl, x in zip(sls, V)]
    yield
    for n in range(nc):
        rp_ref[n] = (Rt[n] + rb[n][:, :W]).astype(BF16)
        y0_ref[n] = rb[n][:, W:] + _dot(Mrk[n], bdV[n])
        pm_ref[n] = (jnp.where(bd_mask, pb[n][:, :W], 0.0)
                     + jnp.where(eye, e_c[ends[n]:ends[n] + 1, :], 0.0)).astype(BF16)
        qm_ref[n] = jnp.where(bd_mask, pb[n][:, W:] + kv[n], 0.0)


def _rwkv_scan(r, kk, k_dirs, b_dirs, v, lw_dirs, *, tb):
    B, S, Wd = r.shape
    nblk = S // tb
    nc = tb // CHUNK
    in_specs, out_specs = [], []
    for rev in (False, True):
        flip = (lambda n: nblk - 1 - n) if rev else (lambda n: n)
        in_idx = lambda b, p, i, flip=flip: (b, flip(jnp.minimum(i, nblk - 1)), p)
        out_idx = lambda b, p, i, flip=flip: (b, flip(jnp.maximum(i - 1, 0)), p)
        in_specs += [pl.BlockSpec((1, tb, LANES), in_idx)] * 6
        out_specs.append(pl.BlockSpec((1, tb, LANES), out_idx))
    scratch = [pltpu.VMEM((LANES, LANES), F32),
               pltpu.VMEM((nc, CHUNK, LANES), BF16), pltpu.VMEM((nc, CHUNK, LANES), F32),
               pltpu.VMEM((nc, LANES, LANES), BF16), pltpu.VMEM((nc, LANES, LANES), F32)]
    operands = [x for d in range(N_DIR) for x in (r, kk, k_dirs[d], b_dirs[d], v, lw_dirs[d])]
    return pl.pallas_call(
        functools.partial(_scan_kernel, tb=tb),
        out_shape=(jax.ShapeDtypeStruct((B, S, Wd), F32),) * N_DIR,
        grid=(B, Wd // LANES, nblk + 1),
        in_specs=in_specs,
        out_specs=tuple(out_specs),
        scratch_shapes=scratch * N_DIR,
        compiler_params=pltpu.CompilerParams(dimension_semantics=("parallel", "parallel", "arbitrary")),
        name="rwkv_scan",
    )(*operands)


def _outproj_kernel(xp_ref, xs_ref, attn_ref, yf_ref, yb_ref, bonus_ref, gate_ref, lw_ref, lb_ref, w_ref, g_ref,
                    o_ref, *, mp):
    y = yf_ref[...] + yb_ref[...]
    mean_mat = _head_ones(LANES)
    inv_n = 1.0 / RWKV_HEAD
    mu = _head_sum(y, mean_mat) * inv_n
    yc = y - mu
    var = _head_sum(yc * yc, mean_mat) * inv_n
    yn = yc * lax.rsqrt(var + LNX_EPS) * lw_ref[...] + lb_ref[...]
    rw = ((yn + bonus_ref[...]) * gate_ref[...]).astype(BF16)
    m = _dot(attn_ref[...], w_ref[0:ATTN_WIDTH, :]) + _dot(rw, w_ref[ATTN_WIDTH:, :])
    ms = jnp.mean(m * m, axis=-1, keepdims=True)
    upd = m * lax.rsqrt(ms + NORM_EPS) * g_ref[...]

    @pl.when(pl.program_id(0) < mp)
    def _():
        o_ref[...] = xp_ref[...] + upd

    @pl.when(pl.program_id(0) >= mp)
    def _():
        o_ref[...] = xs_ref[...] + upd


def _outproj(xp2d, xs2d, attn2d, yf, yb, bonus, gate, lnx_w, lnx_b, w_out_bf16, gain, *, tm):
    mp = xp2d.shape[0] // tm
    T = xp2d.shape[0] + xs2d.shape[0]
    row = lambda w: pl.BlockSpec((tm, w), lambda m: (m, 0))
    const = lambda shape: pl.BlockSpec(shape, lambda m: (0, 0))
    return pl.pallas_call(
        functools.partial(_outproj_kernel, mp=mp),
        out_shape=jax.ShapeDtypeStruct((T, D_MODEL), F32),
        grid=(T // tm,),
        in_specs=[*_row_block_specs(tm, D_MODEL, mp),
                  row(ATTN_WIDTH), row(RWKV_WIDTH), row(RWKV_WIDTH), row(RWKV_WIDTH), row(RWKV_WIDTH),
                  const((1, RWKV_WIDTH)), const((1, RWKV_WIDTH)),
                  const((D_MODEL, D_MODEL)), const((1, D_MODEL))],
        out_specs=row(D_MODEL),
        compiler_params=pltpu.CompilerParams(dimension_semantics=("parallel",)),
        name="outproj",
    )(xp2d, xs2d, attn2d, yf, yb, bonus, gate, lnx_w, lnx_b, w_out_bf16, gain)


def _ffn_kernel(x_ref, gpre_ref, wg_ref, wu_ref, wd_ref, gpost_ref, op_ref, os_ref, h_ref, acc_ref, *, mp):
    j = pl.program_id(1)

    @pl.when(j == 0)
    def _():
        x = x_ref[...]
        ms = jnp.mean(x * x, axis=-1, keepdims=True)
        h_ref[...] = (x * lax.rsqrt(ms + NORM_EPS) * gpre_ref[...]).astype(BF16)
        acc_ref[...] = jnp.zeros_like(acc_ref)

    h = h_ref[...]
    g = _dot(h, wg_ref[...])
    u = _dot(h, wu_ref[...])
    a = (g * _sigmoid(g) * u).astype(BF16)
    acc_ref[...] += _dot(a, wd_ref[...])

    def finish(o_ref):
        f = acc_ref[...]
        ms = jnp.mean(f * f, axis=-1, keepdims=True)
        o_ref[...] = x_ref[...] + f * lax.rsqrt(ms + NORM_EPS) * gpost_ref[...]

    last = j == pl.num_programs(1) - 1

    @pl.when(last & (pl.program_id(0) < mp))
    def _():
        finish(op_ref)

    @pl.when(last & (pl.program_id(0) >= mp))
    def _():
        finish(os_ref)


def _ffn(x2d, gpre, wg, wu, wd, gpost, *, tm, tf, rows_first):
    T = x2d.shape[0]
    mp = rows_first // tm
    return pl.pallas_call(
        functools.partial(_ffn_kernel, mp=mp),
        out_shape=(jax.ShapeDtypeStruct((rows_first, D_MODEL), F32),
                   jax.ShapeDtypeStruct((T - rows_first, D_MODEL), F32)),
        grid=(T // tm, D_FF // tf),
        in_specs=[pl.BlockSpec((tm, D_MODEL), lambda m, j: (m, 0)),
                  pl.BlockSpec((1, D_MODEL), lambda m, j: (0, 0)),
                  pl.BlockSpec((D_MODEL, tf), lambda m, j: (0, j)),
                  pl.BlockSpec((D_MODEL, tf), lambda m, j: (0, j)),
                  pl.BlockSpec((tf, D_MODEL), lambda m, j: (j, 0)),
                  pl.BlockSpec((1, D_MODEL), lambda m, j: (0, 0))],
        out_specs=_row_block_specs(tm, D_MODEL, mp),
        scratch_shapes=[pltpu.VMEM((tm, D_MODEL), BF16), pltpu.VMEM((tm, D_MODEL), F32)],
        compiler_params=pltpu.CompilerParams(dimension_semantics=("arbitrary", "arbitrary")),
        name="ffn",
    )(x2d, gpre, wg, wu, wd, gpost)


def _tiles(S):
    return dict(tmi=256, tm=512, ta=512, ts=256, tb=512, tf=512)


def _lora_blockdiag(w):
    zero = jnp.zeros_like(w[0])
    return jnp.concatenate([jnp.concatenate([w[0], zero], axis=1),
                            jnp.concatenate([zero, w[1]], axis=1)], axis=0)


def _encoder_layer(xp, xs, norm_mix_pre, norm_mix_post, w_in, w_out, lambda_q1, lambda_k1, lambda_q2, lambda_k2,
                   attn_subln, tshift_prev, tshift_next, w0, w2, a0, a2, g2, k_k, k_a, r_k, lnx_w, lnx_b,
                   norm_ffn_pre, norm_ffn_post, w_gate, w_up, w_down):
    (Bp, S, D), Bs = xp.shape, xs.shape[0]
    B = Bp + Bs
    T = B * S
    t = _tiles(S)
    row = lambda a: a.reshape(1, -1).astype(F32)
    pad_cols = RWKV_COLS_PAD - RWKV_COLS

    col_scale = jnp.where(jnp.arange(w_in.shape[1]) < ATTN_WIDTH, Q_PRESCALE, 1.0).astype(F32)
    w_in_p = jnp.pad(w_in * col_scale, ((0, 0), (0, pad_cols))).astype(BF16)
    xp2d, xs2d = xp.reshape(Bp * S, D), xs.reshape(Bs * S, D)
    qkv, z = _inproj(xp2d, xs2d, row(norm_mix_pre), w_in_p, tm=t["tmi"])

    lam_params = jnp.stack([lambda_q1, lambda_k1, lambda_q2, lambda_k2]).astype(F32)
    attn = _diff_attention(qkv.reshape(B, S, QKV_COLS), lam_params, row(attn_subln), t=t["ta"])

    tp = jnp.pad(tshift_prev, (0, pad_cols)).reshape(1, -1)
    tn_ = jnp.pad(tshift_next, (0, pad_cols)).reshape(1, -1)
    g2p = jnp.pad(g2, ((0, GATE_LORA_PAD - GATE_LORA), (0, 0))).astype(BF16)
    (r, v, kk, k0, k1, b0, b1, lw0, lw1, bonus, gate) = _rwkv_prep(
        z.reshape(B, S, RWKV_COLS_PAD), tp, tn_, row(w0), _lora_blockdiag(w2), row(a0),
        _lora_blockdiag(a2).astype(BF16), g2p, row(k_k), row(k_a), row(r_k), ts=t["ts"])
    yf, yb = _rwkv_scan(r, kk, (k0, k1), (b0, b1), v, (lw0, lw1), tb=t["tb"])

    flat = lambda a: a.reshape(T, -1)
    x1 = _outproj(xp2d, xs2d, flat(attn), flat(yf), flat(yb), flat(bonus), flat(gate), row(lnx_w), row(lnx_b),
                  w_out.astype(BF16), row(norm_mix_post), tm=t["tm"])
    yp, ys = _ffn(x1, row(norm_ffn_pre), w_gate.astype(BF16), w_up.astype(BF16), w_down.astype(BF16),
                  row(norm_ffn_post), tm=t["tm"], tf=t["tf"], rows_first=Bp * S)
    return yp.reshape(Bp, S, D), ys.reshape(Bs, S, D)


def kernel(x_prompt, x_sample, norm_mix_pre, norm_mix_post, w_in, w_out, lambda_q1, lambda_k1, lambda_q2,
           lambda_k2, attn_subln, tshift_prev, tshift_next, w0, w2, a0, a2, g2, k_k, k_a, r_k, lnx_w, lnx_b,
           norm_ffn_pre, norm_ffn_post, w_gate, w_up, w_down):
    assert x_prompt.shape[1:] == x_sample.shape[1:], "both trunks share the sequence length"
    assert norm_mix_pre.shape[0] == 1, "single layer"
    return _encoder_layer(x_prompt, x_sample, norm_mix_pre[0], norm_mix_post[0], w_in[0], w_out[0], lambda_q1[0],
                          lambda_k1[0], lambda_q2[0], lambda_k2[0], attn_subln[0], tshift_prev[0], tshift_next[0],
                          w0[0], w2[0], a0[0], a2[0], g2[0], k_k[0], k_a[0], r_k[0], lnx_w[0], lnx_b[0],
                          norm_ffn_pre[0], norm_ffn_post[0], w_gate[0], w_up[0], w_down[0])
```

```python
import functools
import math

import jax
import jax.numpy as jnp
from jax import lax
from jax.experimental import pallas as pl
from jax.experimental.pallas import tpu as pltpu

F32 = jnp.float32
BF16 = jnp.bfloat16

D_MODEL = 2048
ATTN_HEADS = 8
ATTN_VDIM = 128
ATTN_QKDIM = 64
ATTN_WIDTH = ATTN_HEADS * ATTN_VDIM
RWKV_HEAD = 64
RWKV_HEADS = 16
RWKV_WIDTH = RWKV_HEAD * RWKV_HEADS
N_DIR = 2
LORA = 64
GATE_LORA = 160
GATE_LORA_PAD = 256
QKV_COLS = 3 * ATTN_WIDTH
RWKV_COLS = 3 * RWKV_WIDTH + 2 * N_DIR * LORA + GATE_LORA
RWKV_COLS_PAD = 3 * RWKV_WIDTH + 2 * N_DIR * LORA + GATE_LORA_PAD
D_FF = 5632
NORM_EPS = 1e-6
LNX_EPS = 64e-5
LAMBDA_INIT = 0.8 - 0.6 * math.exp(-0.3 * 0)

LANES = 128
CHUNK = 64
PAIR = LANES // RWKV_HEAD

def _dot(a, b):
    return jnp.dot(a, b, preferred_element_type=F32)


def _dot_nt(a, b):
    return lax.dot_general(a, b, (((1,), (1,)), ((), ())), preferred_element_type=F32)


def _dot_tn(a, b):
    return lax.dot_general(a, b, (((0,), (0,)), ((), ())), preferred_element_type=F32)


def _split(x):
    hi = x.astype(BF16)
    lo = (x - hi.astype(F32)).astype(BF16)
    return hi, lo


def _dot3(a, b):
    ah, al = _split(a)
    bh, bl = _split(b)
    return _dot(ah, bh) + _dot(al, bh) + _dot(ah, bl)


def _sigmoid(x):
    return 1.0 / (1.0 + jnp.exp(-x))


def _head_ones(width):
    r = lax.broadcasted_iota(jnp.int32, (width, width), 0) // RWKV_HEAD
    c = lax.broadcasted_iota(jnp.int32, (width, width), 1) // RWKV_HEAD
    return jnp.where(r == c, 1.0, 0.0).astype(BF16)


def _head_sum(x, ones, two_pass=True):
    outs = []
    for g in range(x.shape[1] // LANES):
        xg = x[:, g * LANES:(g + 1) * LANES]
        if two_pass:
            hi, lo = _split(xg)
            outs.append(_dot(hi, ones) + _dot(lo, ones))
        else:
            outs.append(_dot(xg.astype(BF16), ones))
    return jnp.concatenate(outs, axis=1)


def _row_block_specs(tm, width, mp):
    return (pl.BlockSpec((tm, width), lambda m, *_: (jnp.minimum(m, mp - 1), 0)),
            pl.BlockSpec((tm, width), lambda m, *_: (jnp.maximum(m - mp, 0), 0)))


def _inproj_kernel(xp_ref, xs_ref, g_ref, w_ref, qkv_ref, z_ref, h_ref, *, mp):
    m = pl.program_id(0)

    def norm(x_ref):
        x = x_ref[...]
        ms = jnp.mean(x * x, axis=-1, keepdims=True)
        h_ref[...] = (x * lax.rsqrt(ms + NORM_EPS) * g_ref[...]).astype(BF16)

    @pl.when(m < mp)
    def _():
        norm(xp_ref)

    @pl.when(m >= mp)
    def _():
        norm(xs_ref)

    h = h_ref[...]
    qkv_ref[...] = _dot(h, w_ref[:, 0:QKV_COLS]).astype(BF16)
    z_ref[...] = _dot(h, w_ref[:, QKV_COLS:])


def _inproj(xp2d, xs2d, gain, w_in_bf16, *, tm):
    mp = xp2d.shape[0] // tm
    T = xp2d.shape[0] + xs2d.shape[0]
    n_cols = w_in_bf16.shape[1]
    return pl.pallas_call(
        functools.partial(_inproj_kernel, mp=mp),
        out_shape=(jax.ShapeDtypeStruct((T, QKV_COLS), BF16),
                   jax.ShapeDtypeStruct((T, RWKV_COLS_PAD), F32)),
        grid=(T // tm,),
        in_specs=[*_row_block_specs(tm, D_MODEL, mp),
                  pl.BlockSpec((1, D_MODEL), lambda m: (0, 0)),
                  pl.BlockSpec((D_MODEL, n_cols), lambda m: (0, 0), pipeline_mode=pl.Buffered(1))],
        out_specs=(pl.BlockSpec((tm, QKV_COLS), lambda m: (m, 0)),
                   pl.BlockSpec((tm, RWKV_COLS_PAD), lambda m: (m, 0))),
        scratch_shapes=[pltpu.VMEM((tm, D_MODEL), BF16)],
        compiler_params=pltpu.CompilerParams(dimension_semantics=("parallel",)),
        name="inproj",
    )(xp2d, xs2d, gain, w_in_bf16)


LOG2E = math.log2(math.e)
Q_PRESCALE = ATTN_QKDIM ** -0.5 * LOG2E
AUX = 6
NEG_BIG = -1e30
SKIP_BITS = 80.0
NORM_SLACK = 1.01
BF16_ROWS = 16
V_ROWS = ATTN_VDIM + BF16_ROWS


def _pos_aux(slope2, n, first, sign_pos, sign_one):
    row = lax.broadcasted_iota(jnp.int32, (n, LANES), 0).astype(F32)
    lane = lax.broadcasted_iota(jnp.int32, (n, LANES), 1)
    val = slope2 * row * sign_pos
    hi = val.astype(BF16).astype(F32)
    mid = (val - hi).astype(BF16).astype(F32)
    lo = (val - hi - mid).astype(BF16).astype(F32)
    out = jnp.where(lane == first, hi, 0.0)
    out = jnp.where(lane == first + 1, mid, out)
    out = jnp.where(lane == first + 2, lo, out)
    out = jnp.where((lane >= first + 3) & (lane < first + AUX), sign_one, out)
    return out.astype(BF16)


def _attn_kernel(slopes_ref, lam_ref, sub_ref, q_ref, k_ref, v_ref, o_ref,
                 ka_ref, vt_ref, kn_ref, qa_ref, bd_ref, m_ref, acc_ref, s_ref, sm_ref, p_ref, al_ref,
                 *, t, seq):
    h = pl.program_id(1)
    qi = pl.program_id(2)
    nt = seq // t
    slope2 = slopes_ref[h] * LOG2E
    lane = lax.broadcasted_iota(jnp.int32, (t, LANES), 1)
    own = (lane < ATTN_QKDIM, lane >= ATTN_QKDIM)
    aux0 = (ATTN_QKDIM, 0)

    @pl.when(qi == 0)
    def _():
        ak = [_pos_aux(slope2, t, aux0[c], 1.0, 1.0) for c in range(2)]
        ones_rows = jnp.where(lax.broadcasted_iota(jnp.int32, (BF16_ROWS, t), 0) == 0, 1.0, 0.0).astype(BF16)

        def build(j, kn):
            off = pl.multiple_of(j * t, t)
            kt = k_ref[0, pl.ds(off, t), :]
            for c in range(2):
                ka_ref[c, j] = jnp.where(own[c], kt, ak[c])
            vt_ref[j, 0:ATTN_VDIM, :] = v_ref[0, pl.ds(off, t), :].astype(F32).T.astype(BF16)
            vt_ref[j, ATTN_VDIM:, :] = ones_rows
            k32 = kt.astype(F32)
            ksq = k32 * k32
            rows = jnp.maximum(jnp.sum(jnp.where(own[0], ksq, 0.0), axis=1, keepdims=True),
                               jnp.sum(jnp.where(own[1], ksq, 0.0), axis=1, keepdims=True))
            return jnp.maximum(kn, jnp.max(rows, axis=0, keepdims=True))

        kn_ref[...] = lax.fori_loop(0, nt, build, jnp.zeros((1, 1), F32))

        for c in range(2):
            a = aux0[c]
            lane_pos = (lane >= a + 3) & (lane < a + AUX)
            lane_one = (lane >= a) & (lane < a + 3)
            pos = _pos_aux(slope2, t, a + 3, 1.0, 0.0).astype(F32)
            qa_ref[c, 0] = jnp.where(lane_one, 1.0, jnp.where(lane_pos, -pos, 0.0)).astype(BF16)
            qa_ref[c, 1] = jnp.where(lane_one, -1.0, jnp.where(lane_pos, pos, 0.0)).astype(BF16)
        rel = (lax.broadcasted_iota(jnp.int32, (t, t), 0)
               - lax.broadcasted_iota(jnp.int32, (t, t), 1)).astype(F32)
        bd_ref[...] = -slope2 * jnp.abs(rel)

    q = q_ref[0]
    zero = jnp.zeros((t, LANES), BF16)
    q_var = {c: (jnp.where(own[c], q, qa_ref[c, 0]), jnp.where(own[c], q, zero), jnp.where(own[c], q, qa_ref[c, 1]))
             for c in range(2)}

    acc_ref[...] = jnp.zeros(acc_ref.shape, F32)

    q32 = q.astype(F32)
    qsq = q32 * q32
    qn = jnp.maximum(jnp.sum(jnp.where(own[0], qsq, 0.0), axis=1, keepdims=True),
                     jnp.sum(jnp.where(own[1], qsq, 0.0), axis=1, keepdims=True))
    qk_bound = 2.0 * NORM_SLACK * jnp.sqrt(jnp.max(qn, axis=0, keepdims=True) * kn_ref[...])
    dist_needed = (qk_bound + SKIP_BITS) / slope2
    reach = jnp.clip(jnp.ceil((dist_needed - 1.0) / t), 0.0, nt - 1.0)[0, 0].astype(jnp.int32)
    n_left = jnp.minimum(qi, reach)
    n_off = n_left + jnp.minimum(nt - 1 - qi, reach)

    def key_tile(n):
        return jnp.clip(jnp.where(n < n_left, qi - 1 - n, qi + 1 + n - n_left), 0, nt - 1)

    def probs(c, s, smax, cst):
        m_old = m_ref[c]
        m_new = jnp.maximum(m_old, smax + cst)
        m_ref[c] = m_new
        return jnp.exp2(m_old - m_new), jnp.exp2(s - (m_new - cst)).astype(BF16)

    def accumulate(c, alpha, vt, p):
        acc_ref[c] = alpha * acc_ref[c] + _dot(vt, p)

    def stage_scores(n, slot):
        j = key_tile(n)
        for c in range(2):
            qsel = jnp.where(j < qi, q_var[c][0], q_var[c][2])
            s = _dot_nt(ka_ref[c, j], qsel)
            s_ref[slot, c] = s
            sm_ref[slot, c] = jnp.max(s, axis=0, keepdims=True)

    def stage_softmax(n, slot):
        j = key_tile(n)
        cst = jnp.where(n < n_off, -slope2 * (jnp.abs(qi - j) * t).astype(F32), NEG_BIG)
        for c in range(2):
            alpha, p = probs(c, s_ref[slot, c], sm_ref[slot, c], cst)
            p_ref[slot, c] = p
            al_ref[slot, c] = alpha

    def stage_values(n, slot):
        j = key_tile(n)
        for c in range(2):
            accumulate(c, al_ref[slot, c], vt_ref[j], p_ref[slot, c])

    m_ref[...] = jnp.full(m_ref.shape, NEG_BIG, F32)
    s_diag = [_dot_nt(ka_ref[c, qi], q_var[c][1]) + bd_ref[...] for c in range(2)]
    stage_scores(0, 0)
    stage_scores(1, 1)
    for c in range(2):
        alpha, p = probs(c, s_diag[c], jnp.max(s_diag[c], axis=0, keepdims=True), 0.0)
        accumulate(c, alpha, vt_ref[qi], p)
    stage_softmax(0, 0)

    def pair_body(mi, carry):
        n0 = 2 + 2 * mi
        stage_scores(n0, 0)
        stage_values(n0 - 2, 0)
        stage_softmax(n0 - 1, 1)
        stage_scores(n0 + 1, 1)
        stage_values(n0 - 1, 1)
        stage_softmax(n0, 0)
        return carry

    lax.fori_loop(0, (n_off + 1) // 2, pair_body, 0)

    lp = lam_ref[...]
    lam = (jnp.exp(jnp.sum(lp[0:1] * lp[1:2], axis=-1, keepdims=True))
           - jnp.exp(jnp.sum(lp[2:3] * lp[3:4], axis=-1, keepdims=True)) + LAMBDA_INIT)
    num = [acc_ref[c, 0:ATTN_VDIM, :] for c in range(2)]
    den = [acc_ref[c, ATTN_VDIM:ATTN_VDIM + 1, :] for c in range(2)]
    o_t = num[0] / den[0] - lam * (num[1] / den[1])
    o = o_t.T
    ms = jnp.mean(o * o, axis=-1, keepdims=True)
    o = o * lax.rsqrt(ms + NORM_EPS) * sub_ref[...] * (1.0 - LAMBDA_INIT)
    o_ref[0] = o.astype(o_ref.dtype)


def _diff_attention(qkv, lam_params, subln, *, t):
    B, S, _ = qkv.shape
    nt = S // t
    slopes = jnp.asarray([2.0 ** (-8.0 * (h + 1.0) / ATTN_HEADS) for h in range(ATTN_HEADS)], F32)
    grid_spec = pltpu.PrefetchScalarGridSpec(
        num_scalar_prefetch=1,
        grid=(B, ATTN_HEADS, nt),
        in_specs=[pl.BlockSpec((4, ATTN_QKDIM), lambda b, h, i, s: (0, 0)),
                  pl.BlockSpec((1, ATTN_VDIM), lambda b, h, i, s: (0, 0)),
                  pl.BlockSpec((1, t, LANES), lambda b, h, i, s: (b, i, h)),
                  pl.BlockSpec((1, S, LANES), lambda b, h, i, s: (b, 0, ATTN_HEADS + h)),
                  pl.BlockSpec((1, S, LANES), lambda b, h, i, s: (b, 0, 2 * ATTN_HEADS + h))],
        out_specs=pl.BlockSpec((1, t, LANES), lambda b, h, i, s: (b, i, h)),
        scratch_shapes=[pltpu.VMEM((2, nt, t, LANES), BF16),
                        pltpu.VMEM((nt, V_ROWS, t), BF16),
                        pltpu.VMEM((1, 1), F32),
                        pltpu.VMEM((2, 2, t, LANES), BF16),
                        pltpu.VMEM((t, t), F32),
                        pltpu.VMEM((2, 1, t), F32),
                        pltpu.VMEM((2, V_ROWS, t), F32),
                        pltpu.VMEM((2, 2, t, t), F32),
                        pltpu.VMEM((2, 2, 1, t), F32),
                        pltpu.VMEM((2, 2, t, t), BF16),
                        pltpu.VMEM((2, 2, 1, t), F32)],
    )
    return pl.pallas_call(
        functools.partial(_attn_kernel, t=t, seq=S),
        out_shape=jax.ShapeDtypeStruct((B, S, ATTN_WIDTH), BF16),
        grid_spec=grid_spec,
        compiler_params=pltpu.CompilerParams(dimension_semantics=("parallel", "parallel", "arbitrary")),
        name="diff_attn",
    )(slopes, lam_params, subln, qkv, qkv, qkv)


def _prep_kernel(z_ref, zp_ref, zn_ref, tp_ref, tn_ref, w0_ref, w2_ref, a0_ref, a2_ref, g2_ref,
                 kk_ref, ka_ref, rk_ref,
                 r_o, v_o, kk_o, k0_o, k1_o, b0_o, b1_o, lw0_o, lw1_o, bonus_o, gate_o):
    i = pl.program_id(1)
    nt = pl.num_programs(1)
    z = z_ref[0]
    ts = z.shape[0]
    row = lax.broadcasted_iota(jnp.int32, (ts, 1), 0)
    prev_row = jnp.where(i > 0, zp_ref[0, 7:8, :], 0.0)
    next_row = jnp.where(i < nt - 1, zn_ref[0, 0:1, :], 0.0)
    z_prev = jnp.where(row == 0, prev_row, pltpu.roll(z, 1, 0))
    z_next = jnp.where(row == ts - 1, next_row, pltpu.roll(z, ts - 1, 0))
    zs = z + tp_ref[...] * (z_prev - z) + tn_ref[...] * (z_next - z)

    W = RWKV_WIDTH
    r = zs[:, 0:W]
    k = zs[:, W:2 * W]
    v = zs[:, 2 * W:3 * W]
    wd = jnp.tanh(zs[:, 3 * W:3 * W + LANES])
    ad = zs[:, 3 * W + LANES:3 * W + 2 * LANES]
    gd = zs[:, 3 * W + 2 * LANES:]

    w_log = w0_ref[...] + _dot3(wd, w2_ref[...])
    lw = -_sigmoid(w_log) * math.exp(-0.5)
    iclr = _sigmoid(a0_ref[...] + _dot(ad.astype(BF16), a2_ref[...].astype(BF16)))
    gate = _dot(_sigmoid(gd).astype(BF16), g2_ref[...].astype(BF16))

    ones = _head_ones(LANES)
    kk = k * kk_ref[...]
    kk = kk * lax.rsqrt(jnp.maximum(_head_sum(kk * kk, ones), 1e-24))
    ka = ka_ref[...]
    rk = rk_ref[...]
    bonus = jnp.zeros_like(r)
    k_outs = (k0_o, k1_o)
    b_outs = (b0_o, b1_o)
    lw_outs = (lw0_o, lw1_o)
    for d in range(N_DIR):
        a_d = iclr[:, d * W:(d + 1) * W]
        k_d = k * (1.0 + (a_d - 1.0) * ka)
        k_outs[d][0] = k_d.astype(k_outs[d].dtype)
        b_outs[d][0] = (kk * a_d).astype(b_outs[d].dtype)
        lw_outs[d][0] = lw[:, d * W:(d + 1) * W]
        bonus = bonus + _head_sum(r * k_d * rk, ones, two_pass=False) * v
    r_o[0] = r.astype(r_o.dtype)
    v_o[0] = v.astype(v_o.dtype)
    kk_o[0] = kk.astype(kk_o.dtype)
    bonus_o[0] = bonus
    gate_o[0] = gate.astype(gate_o.dtype)


def _rwkv_prep(z, tp, tn, w0, w2bd, a0, a2bd, g2p, k_k, k_a, r_k, *, ts):
    B, S, ZC = z.shape
    nt = S // ts
    hb = ts // 8
    last8 = S // 8 - 1
    const = lambda shape: pl.BlockSpec(shape, lambda b, i: (0, 0))
    out_dtypes = (BF16,) * 7 + (F32, F32, F32, BF16)
    out_spec = pl.BlockSpec((1, ts, RWKV_WIDTH), lambda b, i: (b, i, 0))
    return pl.pallas_call(
        _prep_kernel,
        out_shape=tuple(jax.ShapeDtypeStruct((B, S, RWKV_WIDTH), dt) for dt in out_dtypes),
        grid=(B, nt),
        in_specs=[pl.BlockSpec((1, ts, ZC), lambda b, i: (b, i, 0)),
                  pl.BlockSpec((1, 8, ZC), lambda b, i: (b, jnp.maximum(i * hb - 1, 0), 0)),
                  pl.BlockSpec((1, 8, ZC), lambda b, i: (b, jnp.minimum((i + 1) * hb, last8), 0)),
                  const((1, ZC)), const((1, ZC)),
                  const((1, N_DIR * RWKV_WIDTH)), const((LANES, N_DIR * RWKV_WIDTH)),
                  const((1, N_DIR * RWKV_WIDTH)), const((LANES, N_DIR * RWKV_WIDTH)),
                  const((GATE_LORA_PAD, RWKV_WIDTH)),
                  const((1, RWKV_WIDTH)), const((1, RWKV_WIDTH)), const((1, RWKV_WIDTH))],
        out_specs=(out_spec,) * 11,
        compiler_params=pltpu.CompilerParams(dimension_semantics=("parallel", "arbitrary")),
        name="rwkv_prep",
    )(z, z, z, tp, tn, w0, w2bd, a0, a2bd, g2p, k_k, k_a, r_k)


def _scan_kernel(*refs, tb):
    n_in, n_scr = 6, 5
    ins = (refs[0:n_in], refs[n_in:2 * n_in])
    outs = refs[2 * n_in:2 * n_in + 2]
    scr = (refs[2 * n_in + 2:2 * n_in + 2 + n_scr], refs[2 * n_in + 2 + n_scr:])

    @pl.when(pl.program_id(2) == 0)
    def _():
        for ref in scr[0] + scr[1]:
            ref[...] = jnp.zeros_like(ref)

    live = [_scan_stages(*ins[d], outs[d], *scr[d], tb=tb, reverse=bool(d)) for d in range(N_DIR)]
    while live:
        for g in list(live):
            if next(g, "done") == "done":
                live.remove(g)


def _scan_stages(r_ref, kk_ref, k_ref, b_ref, v_ref, lw_ref, y_ref, s_ref, rp_ref, y0_ref, pm_ref, qm_ref,
                 *, tb, reverse):
    C = CHUNK
    W = LANES
    nc = tb // C
    order = [(nc - 1 - jj) if reverse else jj for jj in range(nc)]
    sls = [slice(j * C, (j + 1) * C) for j in order]
    ends = [j * C if reverse else (j + 1) * C - 1 for j in order]

    lw = lw_ref[0]
    kk32, r32, b32, k32, v_all = (x[0].astype(F32) for x in (kk_ref, r_ref, b_ref, k_ref, v_ref))
    chain = {"s": s_ref[...], "n": 0}

    def chain_step():
        n = chain["n"]
        if n < nc:
            s16 = chain["s"].astype(BF16)
            y_ref[0, sls[n], :] = _dot(rp_ref[n], s16) + y0_ref[n]
            chain["s"] = _dot(pm_ref[n], s16) + qm_ref[n]
            chain["n"] = n + 1

    pos = lax.broadcasted_iota(jnp.int32, (tb, 1), 0) % C
    c = lw
    shift = 1
    while shift < C:
        if reverse:
            c = c + jnp.where(pos < C - shift, pltpu.roll(c, tb - shift, 0), 0.0)
        else:
            c = c + jnp.where(pos >= shift, pltpu.roll(c, shift, 0), 0.0)
        shift *= 2
    d = jnp.concatenate([c[e:e + 1, :] - c[j * C:(j + 1) * C] for j, e in
                         sorted(zip(order, ends))], axis=0)

    e_c = jnp.exp(c)
    e_nc = jnp.exp(-c)
    e_d = jnp.exp(d)
    at_all = -kk32 * jnp.exp(c - lw)
    rt_all = r32 * e_c
    bt_all = b32 * e_nc
    kt_all = k32 * e_nc
    bh_all = b32 * e_d
    kh_all = k32 * e_d
    yield

    r2 = lax.broadcasted_iota(jnp.int32, (W, W), 0)
    c2 = lax.broadcasted_iota(jnp.int32, (W, W), 1)
    bd_mask = (r2 // C) == (c2 // RWKV_HEAD)
    eye = r2 == c2
    tr = lax.broadcasted_iota(jnp.int32, (C, W), 0)
    sc = lax.broadcasted_iota(jnp.int32, (C, W), 1) % C
    strict = (sc > tr) if reverse else (sc < tr)
    incl = (sc >= tr) if reverse else (sc <= tr)

    def bd(x):
        return jnp.where(bd_mask, jnp.concatenate([x] * PAIR, axis=0), 0.0).astype(BF16)

    def bd2(xa, xb):
        return jnp.concatenate([bd(xa), bd(xb)], axis=1)

    At = [at_all[sl] for sl in sls]
    Rt = [rt_all[sl] for sl in sls]
    V = [v_all[sl] for sl in sls]
    bdV = [bd(x) for x in V]
    lhs = [jnp.concatenate([a, r], axis=0).astype(BF16) for a, r in zip(At, Rt)]
    sbk = [_dot_nt(l, jnp.concatenate([bd(bt_all[sl]), bd(kt_all[sl])], axis=0)) for l, sl in zip(lhs, sls)]
    sb = [x[:, :W] for x in sbk]
    sk = [x[:, W:] for x in sbk]
    lp = [jnp.where(strict, x[:C], 0.0) for x in sb]
    Lak = [jnp.where(strict, x[:C], 0.0).astype(BF16) for x in sk]
    Mrb = [jnp.where(incl, x[C:], 0.0).astype(BF16) for x in sb]
    Mrk = [jnp.where(incl, x[C:], 0.0).astype(BF16) for x in sk]
    chain_step()
    yield
    xa = list(At)
    xu = [_dot(l, b) for l, b in zip(Lak, bdV)]
    chain_step()
    yield
    n_dbl = C.bit_length() - 1
    for it in range(n_dbl):
        lp16 = [x.astype(BF16) for x in lp]
        if it + 1 < n_dbl:
            upd = [_dot(l16, jnp.concatenate([bd2(a, u), bd(l)], axis=1))
                   for l16, l, a, u in zip(lp16, lp, xa, xu)]
            lp = [x[:, 2 * W:] for x in upd]
        else:
            upd = [_dot(l, bd2(a, u)) for l, a, u in zip(lp16, xa, xu)]
        xa = [a + x[:, :W] for a, x in zip(xa, upd)]
        xu = [u + x[:, W:2 * W] for u, x in zip(xu, upd)]
        chain_step()
        yield
    rb = [_dot(m, bd2(a, u)) for m, a, u in zip(Mrb, xa, xu)]
    while chain["n"] < nc:
        chain_step()
    s_ref[...] = chain["s"]
    yield
    au = [jnp.concatenate([a, u], axis=1).astype(BF16) for a, u in zip(xa, xu)]
    pb = [_dot_tn(bh_all[sl].astype(BF16), x) for sl, x in zip(sls, au)]
    kv = [_dot_tn(kh_all[sl].astype(BF16), x.astype(BF16)) for sl, x in zip(sls, V)]
    yield
    for n in range(nc):
        rp_ref[n] = (Rt[n] + rb[n][:, :W]).astype(BF16)
        y0_ref[n] = rb[n][:, W:] + _dot(Mrk[n], bdV[n])
        pm_ref[n] = (jnp.where(bd_mask, pb[n][:, :W], 0.0)
                     + jnp.where(eye, e_c[ends[n]:ends[n] + 1, :], 0.0)).astype(BF16)
        qm_ref[n] = jnp.where(bd_mask, pb[n][:, W:] + kv[n], 0.0)


def _rwkv_scan(r, kk, k_dirs, b_dirs, v, lw_dirs, *, tb):
    B, S, Wd = r.shape
    nblk = S // tb
    nc = tb // CHUNK
    in_specs, out_specs = [], []
    for rev in (False, True):
        flip = (lambda n: nblk - 1 - n) if rev else (lambda n: n)
        in_idx = lambda b, p, i, flip=flip: (b, flip(jnp.minimum(i, nblk - 1)), p)
        out_idx = lambda b, p, i, flip=flip: (b, flip(jnp.maximum(i - 1, 0)), p)
        in_specs += [pl.BlockSpec((1, tb, LANES), in_idx)] * 6
        out_specs.append(pl.BlockSpec((1, tb, LANES), out_idx))
    scratch = [pltpu.VMEM((LANES, LANES), F32),
               pltpu.VMEM((nc, CHUNK, LANES), BF16), pltpu.VMEM((nc, CHUNK, LANES), F32),
               pltpu.VMEM((nc, LANES, LANES), BF16), pltpu.VMEM((nc, LANES, LANES), F32)]
    operands = [x for d in range(N_DIR) for x in (r, kk, k_dirs[d], b_dirs[d], v, lw_dirs[d])]
    return pl.pallas_call(
        functools.partial(_scan_kernel, tb=tb),
        out_shape=(jax.ShapeDtypeStruct((B, S, Wd), F32),) * N_DIR,
        grid=(B, Wd // LANES, nblk + 1),
        in_specs=in_specs,
        out_specs=tuple(out_specs),
        scratch_shapes=scratch * N_DIR,
        compiler_params=pltpu.CompilerParams(dimension_semantics=("parallel", "parallel", "arbitrary")),
        name="rwkv_scan",
    )(*operands)


def _outproj_kernel(xp_ref, xs_ref, attn_ref, yf_ref, yb_ref, bonus_ref, gate_ref, lw_ref, lb_ref, w_ref, g_ref,
                    o_ref, *, mp):
    y = yf_ref[...] + yb_ref[...]
    mean_mat = _head_ones(LANES)
    inv_n = 1.0 / RWKV_HEAD
    mu = _head_sum(y, mean_mat) * inv_n
    yc = y - mu
    var = _head_sum(yc * yc, mean_mat) * inv_n
    yn = yc * lax.rsqrt(var + LNX_EPS) * lw_ref[...] + lb_ref[...]
    rw = ((yn + bonus_ref[...]) * gate_ref[...]).astype(BF16)
    m = _dot(attn_ref[...], w_ref[0:ATTN_WIDTH, :]) + _dot(rw, w_ref[ATTN_WIDTH:, :])
    ms = jnp.mean(m * m, axis=-1, keepdims=True)
    upd = m * lax.rsqrt(ms + NORM_EPS) * g_ref[...]

    @pl.when(pl.program_id(0) < mp)
    def _():
        o_ref[...] = xp_ref[...] + upd

    @pl.when(pl.program_id(0) >= mp)
    def _():
        o_ref[...] = xs_ref[...] + upd


def _outproj(xp2d, xs2d, attn2d, yf, yb, bonus, gate, lnx_w, lnx_b, w_out_bf16, gain, *, tm):
    mp = xp2d.shape[0] // tm
    T = xp2d.shape[0] + xs2d.shape[0]
    row = lambda w: pl.BlockSpec((tm, w), lambda m: (m, 0))
    const = lambda shape: pl.BlockSpec(shape, lambda m: (0, 0))
    return pl.pallas_call(
        functools.partial(_outproj_kernel, mp=mp),
        out_shape=jax.ShapeDtypeStruct((T, D_MODEL), F32),
        grid=(T // tm,),
        in_specs=[*_row_block_specs(tm, D_MODEL, mp),
                  row(ATTN_WIDTH), row(RWKV_WIDTH), row(RWKV_WIDTH), row(RWKV_WIDTH), row(RWKV_WIDTH),
                  const((1, RWKV_WIDTH)), const((1, RWKV_WIDTH)),
                  const((D_MODEL, D_MODEL)), const((1, D_MODEL))],
        out_specs=row(D_MODEL),
        compiler_params=pltpu.CompilerParams(dimension_semantics=("parallel",)),
        name="outproj",
    )(xp2d, xs2d, attn2d, yf, yb, bonus, gate, lnx_w, lnx_b, w_out_bf16, gain)


def _ffn_kernel(x_ref, gpre_ref, wg_ref, wu_ref, wd_ref, gpost_ref, op_ref, os_ref, h_ref, acc_ref, *, mp):
    j = pl.program_id(1)

    @pl.when(j == 0)
    def _():
        x = x_ref[...]
        ms = jnp.mean(x * x, axis=-1, keepdims=True)
        h_ref[...] = (x * lax.rsqrt(ms + NORM_EPS) * gpre_ref[...]).astype(BF16)
        acc_ref[...] = jnp.zeros_like(acc_ref)

    h = h_ref[...]
    g = _dot(h, wg_ref[...])
    u = _dot(h, wu_ref[...])
    a = (g * _sigmoid(g) * u).astype(BF16)
    acc_ref[...] += _dot(a, wd_ref[...])

    def finish(o_ref):
        f = acc_ref[...]
        ms = jnp.mean(f * f, axis=-1, keepdims=True)
        o_ref[...] = x_ref[...] + f * lax.rsqrt(ms + NORM_EPS) * gpost_ref[...]

    last = j == pl.num_programs(1) - 1

    @pl.when(last & (pl.program_id(0) < mp))
    def _():
        finish(op_ref)

    @pl.when(last & (pl.program_id(0) >= mp))
    def _():
        finish(os_ref)


def _ffn(x2d, gpre, wg, wu, wd, gpost, *, tm, tf, rows_first):
    T = x2d.shape[0]
    mp = rows_first // tm
    return pl.pallas_call(
        functools.partial(_ffn_kernel, mp=mp),
        out_shape=(jax.ShapeDtypeStruct((rows_first, D_MODEL), F32),
                   jax.ShapeDtypeStruct((T - rows_first, D_MODEL), F32)),
        grid=(T // tm, D_FF // tf),
        in_specs=[pl.BlockSpec((tm, D_MODEL), lambda m, j: (m, 0)),
                  pl.BlockSpec((1, D_MODEL), lambda m, j: (0, 0)),
                  pl.BlockSpec((D_MODEL, tf), lambda m, j: (0, j)),
                  pl.BlockSpec((D_MODEL, tf), lambda m, j: (0, j)),
                  pl.BlockSpec((tf, D_MODEL), lambda m, j: (j, 0)),
                  pl.BlockSpec((1, D_MODEL), lambda m, j: (0, 0))],
        out_specs=_row_block_specs(tm, D_MODEL, mp),
        scratch_shapes=[pltpu.VMEM((tm, D_MODEL), BF16), pltpu.VMEM((tm, D_MODEL), F32)],
        compiler_params=pltpu.CompilerParams(dimension_semantics=("arbitrary", "arbitrary")),
        name="ffn",
    )(x2d, gpre, wg, wu, wd, gpost)


def _tiles(S):
    return dict(tmi=256, tm=512, ta=512, ts=256, tb=512, tf=512)


def _lora_blockdiag(w):
    zero = jnp.zeros_like(w[0])
    return jnp.concatenate([jnp.concatenate([w[0], zero], axis=1),
                            jnp.concatenate([zero, w[1]], axis=1)], axis=0)


def _encoder_layer(xp, xs, norm_mix_pre, norm_mix_post, w_in, w_out, lambda_q1, lambda_k1, lambda_q2, lambda_k2,
                   attn_subln, tshift_prev, tshift_next, w0, w2, a0, a2, g2, k_k, k_a, r_k, lnx_w, lnx_b,
                   norm_ffn_pre, norm_ffn_post, w_gate, w_up, w_down):
    (Bp, S, D), Bs = xp.shape, xs.shape[0]
    B = Bp + Bs
    T = B * S
    t = _tiles(S)
    row = lambda a: a.reshape(1, -1).astype(F32)
    pad_cols = RWKV_COLS_PAD - RWKV_COLS

    col_scale = jnp.where(jnp.arange(w_in.shape[1]) < ATTN_WIDTH, Q_PRESCALE, 1.0).astype(F32)
    w_in_p = jnp.pad(w_in * col_scale, ((0, 0), (0, pad_cols))).astype(BF16)
    xp2d, xs2d = xp.reshape(Bp * S, D), xs.reshape(Bs * S, D)
    qkv, z = _inproj(xp2d, xs2d, row(norm_mix_pre), w_in_p, tm=t["tmi"])

    lam_params = jnp.stack([lambda_q1, lambda_k1, lambda_q2, lambda_k2]).astype(F32)
    attn = _diff_attention(qkv.reshape(B, S, QKV_COLS), lam_params, row(attn_subln), t=t["ta"])

    tp = jnp.pad(tshift_prev, (0, pad_cols)).reshape(1, -1)
    tn_ = jnp.pad(tshift_next, (0, pad_cols)).reshape(1, -1)
    g2p = jnp.pad(g2, ((0, GATE_LORA_PAD - GATE_LORA), (0, 0))).astype(BF16)
    (r, v, kk, k0, k1, b0, b1, lw0, lw1, bonus, gate) = _rwkv_prep(
        z.reshape(B, S, RWKV_COLS_PAD), tp, tn_, row(w0), _lora_blockdiag(w2), row(a0),
        _lora_blockdiag(a2).astype(BF16), g2p, row(k_k), row(k_a), row(r_k), ts=t["ts"])
    yf, yb = _rwkv_scan(r, kk, (k0, k1), (b0, b1), v, (lw0, lw1), tb=t["tb"])

    flat = lambda a: a.reshape(T, -1)
    x1 = _outproj(xp2d, xs2d, flat(attn), flat(yf), flat(yb), flat(bonus), flat(gate), row(lnx_w), row(lnx_b),
                  w_out.astype(BF16), row(norm_mix_post), tm=t["tm"])
    yp, ys = _ffn(x1, row(norm_ffn_pre), w_gate.astype(BF16), w_up.astype(BF16), w_down.astype(BF16),
                  row(norm_ffn_post), tm=t["tm"], tf=t["tf"], rows_first=Bp * S)
    return yp.reshape(Bp, S, D), ys.reshape(Bs, S, D)


def kernel(x_prompt, x_sample, norm_mix_pre, norm_mix_post, w_in, w_out, lambda_q1, lambda_k1, lambda_q2,
           lambda_k2, attn_subln, tshift_prev, tshift_next, w0, w2, a0, a2, g2, k_k, k_a, r_k, lnx_w, lnx_b,
           norm_ffn_pre, norm_ffn_post, w_gate, w_up, w_down):
    assert x_prompt.shape[1:] == x_sample.shape[1:], "both trunks share the sequence length"
    assert norm_mix_pre.shape[0] == 1, "single layer"
    return _encoder_layer(x_prompt, x_sample, norm_mix_pre[0], norm_mix_post[0], w_in[0], w_out[0], lambda_q1[0],
                          lambda_k1[0], lambda_q2[0], lambda_k2[0], attn_subln[0], tshift_prev[0], tshift_next[0],
                          w0[0], w2[0], a0[0], a2[0], g2[0], k_k[0], k_a[0], r_k[0], lnx_w[0], lnx_b[0],
                          norm_ffn_pre[0], norm_ffn_post[0], w_gate[0], w_up[0], w_down[0])
```

```python
import functools
import math

import jax
import jax.numpy as jnp
from jax import lax
from jax.experimental import pallas as pl
from jax.experimental.pallas import tpu as pltpu

F32 = jnp.float32
BF16 = jnp.bfloat16

D_MODEL = 2048
ATTN_HEADS = 8
ATTN_VDIM = 128
ATTN_QKDIM = 64
ATTN_WIDTH = ATTN_HEADS * ATTN_VDIM
RWKV_HEAD = 64
RWKV_HEADS = 16
RWKV_WIDTH = RWKV_HEAD * RWKV_HEADS
N_DIR = 2
LORA = 64
GATE_LORA = 160
GATE_LORA_PAD = 256
QKV_COLS = 3 * ATTN_WIDTH
RWKV_COLS = 3 * RWKV_WIDTH + 2 * N_DIR * LORA + GATE_LORA
RWKV_COLS_PAD = 3 * RWKV_WIDTH + 2 * N_DIR * LORA + GATE_LORA_PAD
D_FF = 5632
NORM_EPS = 1e-6
LNX_EPS = 64e-5
LAMBDA_INIT = 0.8 - 0.6 * math.exp(-0.3 * 0)

LANES = 128
CHUNK = 64
PAIR = LANES // RWKV_HEAD

def _dot(a, b):
    return jnp.dot(a, b, preferred_element_type=F32)


def _dot_nt(a, b):
    return lax.dot_general(a, b, (((1,), (1,)), ((), ())), preferred_element_type=F32)


def _dot_tn(a, b):
    return lax.dot_general(a, b, (((0,), (0,)), ((), ())), preferred_element_type=F32)


def _split(x):
    hi = x.astype(BF16)
    lo = (x - hi.astype(F32)).astype(BF16)
    return hi, lo


def _dot3(a, b):
    ah, al = _split(a)
    bh, bl = _split(b)
    return _dot(ah, bh) + _dot(al, bh) + _dot(ah, bl)


def _sigmoid(x):
    return 1.0 / (1.0 + jnp.exp(-x))


def _head_ones(width):
    r = lax.broadcasted_iota(jnp.int32, (width, width), 0) // RWKV_HEAD
    c = lax.broadcasted_iota(jnp.int32, (width, width), 1) // RWKV_HEAD
    return jnp.where(r == c, 1.0, 0.0).astype(BF16)


def _head_sum(x, ones, two_pass=True):
    outs = []
    for g in range(x.shape[1] // LANES):
        xg = x[:, g * LANES:(g + 1) * LANES]
        if two_pass:
            hi, lo = _split(xg)
            outs.append(_dot(hi, ones) + _dot(lo, ones))
        else:
            outs.append(_dot(xg.astype(BF16), ones))
    return jnp.concatenate(outs, axis=1)


def _row_block_specs(tm, width, mp):
    return (pl.BlockSpec((tm, width), lambda m, *_: (jnp.minimum(m, mp - 1), 0)),
            pl.BlockSpec((tm, width), lambda m, *_: (jnp.maximum(m - mp, 0), 0)))


def _inproj_kernel(xp_ref, xs_ref, g_ref, w_ref, qkv_ref, z_ref, h_ref, *, mp):
    m = pl.program_id(0)

    def norm(x_ref):
        x = x_ref[...]
        ms = jnp.mean(x * x, axis=-1, keepdims=True)
        h_ref[...] = (x * lax.rsqrt(ms + NORM_EPS) * g_ref[...]).astype(BF16)

    @pl.when(m < mp)
    def _():
        norm(xp_ref)

    @pl.when(m >= mp)
    def _():
        norm(xs_ref)

    h = h_ref[...]
    qkv_ref[...] = _dot(h, w_ref[:, 0:QKV_COLS]).astype(BF16)
    z_ref[...] = _dot(h, w_ref[:, QKV_COLS:])


def _inproj(xp2d, xs2d, gain, w_in_bf16, *, tm):
    mp = xp2d.shape[0] // tm
    T = xp2d.shape[0] + xs2d.shape[0]
    n_cols = w_in_bf16.shape[1]
    return pl.pallas_call(
        functools.partial(_inproj_kernel, mp=mp),
        out_shape=(jax.ShapeDtypeStruct((T, QKV_COLS), BF16),
                   jax.ShapeDtypeStruct((T, RWKV_COLS_PAD), F32)),
        grid=(T // tm,),
        in_specs=[*_row_block_specs(tm, D_MODEL, mp),
                  pl.BlockSpec((1, D_MODEL), lambda m: (0, 0)),
                  pl.BlockSpec((D_MODEL, n_cols), lambda m: (0, 0), pipeline_mode=pl.Buffered(1))],
        out_specs=(pl.BlockSpec((tm, QKV_COLS), lambda m: (m, 0)),
                   pl.BlockSpec((tm, RWKV_COLS_PAD), lambda m: (m, 0))),
        scratch_shapes=[pltpu.VMEM((tm, D_MODEL), BF16)],
        compiler_params=pltpu.CompilerParams(dimension_semantics=("parallel",)),
        name="inproj",
    )(xp2d, xs2d, gain, w_in_bf16)


LOG2E = math.log2(math.e)
Q_PRESCALE = ATTN_QKDIM ** -0.5 * LOG2E
AUX = 6
NEG_BIG = -1e30
SKIP_BITS = 80.0
NORM_SLACK = 1.01
BF16_ROWS = 16
V_ROWS = ATTN_VDIM + BF16_ROWS


def _pos_aux(slope2, n, first, sign_pos, sign_one):
    row = lax.broadcasted_iota(jnp.int32, (n, LANES), 0).astype(F32)
    lane = lax.broadcasted_iota(jnp.int32, (n, LANES), 1)
    val = slope2 * row * sign_pos
    hi = val.astype(BF16).astype(F32)
    mid = (val - hi).astype(BF16).astype(F32)
    lo = (val - hi - mid).astype(BF16).astype(F32)
    out = jnp.where(lane == first, hi, 0.0)
    out = jnp.where(lane == first + 1, mid, out)
    out = jnp.where(lane == first + 2, lo, out)
    out = jnp.where((lane >= first + 3) & (lane < first + AUX), sign_one, out)
    return out.astype(BF16)


def _attn_kernel(slopes_ref, lam_ref, sub_ref, q_ref, k_ref, v_ref, o_ref,
                 ka_ref, vt_ref, kn_ref, qa_ref, bd_ref, qv_ref, m_ref, acc_ref, s_ref, sm_ref, p_ref, al_ref,
                 *, t, seq):
    h = pl.program_id(1)
    qi = pl.program_id(2)
    nt = seq // t
    slope2 = slopes_ref[h] * LOG2E
    lane = lax.broadcasted_iota(jnp.int32, (t, LANES), 1)
    own = (lane < ATTN_QKDIM, lane >= ATTN_QKDIM)
    aux0 = (ATTN_QKDIM, 0)

    @pl.when(qi == 0)
    def _():
        ak = [_pos_aux(slope2, t, aux0[c], 1.0, 1.0) for c in range(2)]
        ones_rows = jnp.where(lax.broadcasted_iota(jnp.int32, (BF16_ROWS, t), 0) == 0, 1.0, 0.0).astype(BF16)

        def build(j, kn):
            off = pl.multiple_of(j * t, t)
            kt = k_ref[0, pl.ds(off, t), :]
            for c in range(2):
                ka_ref[c, j] = jnp.where(own[c], kt, ak[c])
            vt_ref[j, 0:ATTN_VDIM, :] = v_ref[0, pl.ds(off, t), :].astype(F32).T.astype(BF16)
            vt_ref[j, ATTN_VDIM:, :] = ones_rows
            k32 = kt.astype(F32)
            ksq = k32 * k32
            rows = jnp.maximum(jnp.sum(jnp.where(own[0], ksq, 0.0), axis=1, keepdims=True),
                               jnp.sum(jnp.where(own[1], ksq, 0.0), axis=1, keepdims=True))
            return jnp.maximum(kn, jnp.max(rows, axis=0, keepdims=True))

        kn_ref[...] = lax.fori_loop(0, nt, build, jnp.zeros((1, 1), F32))

        for c in range(2):
            a = aux0[c]
            lane_pos = (lane >= a + 3) & (lane < a + AUX)
            lane_one = (lane >= a) & (lane < a + 3)
            pos = _pos_aux(slope2, t, a + 3, 1.0, 0.0).astype(F32)
            qa_ref[c, 0] = jnp.where(lane_one, 1.0, jnp.where(lane_pos, -pos, 0.0)).astype(BF16)
            qa_ref[c, 1] = jnp.where(lane_one, -1.0, jnp.where(lane_pos, pos, 0.0)).astype(BF16)
        rel = (lax.broadcasted_iota(jnp.int32, (t, t), 0)
               - lax.broadcasted_iota(jnp.int32, (t, t), 1)).astype(F32)
        bd_ref[...] = -slope2 * jnp.abs(rel)

    q = q_ref[0]
    zero = jnp.zeros((t, LANES), BF16)
    for c in range(2):
        for side in range(2):
            qv_ref[c, side] = jnp.where(own[c], q, qa_ref[c, side])
    q_diag = [jnp.where(own[c], q, zero) for c in range(2)]

    acc_ref[...] = jnp.zeros(acc_ref.shape, F32)

    q32 = q.astype(F32)
    qsq = q32 * q32
    qn = jnp.maximum(jnp.sum(jnp.where(own[0], qsq, 0.0), axis=1, keepdims=True),
                     jnp.sum(jnp.where(own[1], qsq, 0.0), axis=1, keepdims=True))
    qk_bound = 2.0 * NORM_SLACK * jnp.sqrt(jnp.max(qn, axis=0, keepdims=True) * kn_ref[...])
    dist_needed = (qk_bound + SKIP_BITS) / slope2
    reach = jnp.clip(jnp.ceil((dist_needed - 1.0) / t), 0.0, nt - 1.0)[0, 0].astype(jnp.int32)
    n_left = jnp.minimum(qi, reach)
    n_off = n_left + jnp.minimum(nt - 1 - qi, reach)

    def key_tile(n):
        return jnp.clip(jnp.where(n < n_left, qi - 1 - n, qi + 1 + n - n_left), 0, nt - 1)

    def probs(c, s, smax, cst):
        m_old = m_ref[c]
        m_new = jnp.maximum(m_old, smax + cst)
        m_ref[c] = m_new
        return jnp.exp2(m_old - m_new), jnp.exp2(s - (m_new - cst)).astype(BF16)

    def accumulate(c, alpha, vt, p):
        acc_ref[c] = alpha * acc_ref[c] + _dot(vt, p)

    def stage_scores(n, slot):
        j = key_tile(n)
        for c in range(2):
            s = _dot_nt(ka_ref[c, j], qv_ref[c, (j > qi).astype(jnp.int32)])
            s_ref[slot, c] = s
            sm_ref[slot, c] = jnp.max(s, axis=0, keepdims=True)

    def stage_softmax(n, slot):
        j = key_tile(n)
        cst = jnp.where(n < n_off, -slope2 * (jnp.abs(qi - j) * t).astype(F32), NEG_BIG)
        for c in range(2):
            alpha, p = probs(c, s_ref[slot, c], sm_ref[slot, c], cst)
            p_ref[slot, c] = p
            al_ref[slot, c] = alpha

    def stage_values(n, slot):
        j = key_tile(n)
        for c in range(2):
            accumulate(c, al_ref[slot, c], vt_ref[j], p_ref[slot, c])

    m_ref[...] = jnp.full(m_ref.shape, NEG_BIG, F32)
    s_diag = [_dot_nt(ka_ref[c, qi], q_diag[c]) + bd_ref[...] for c in range(2)]
    stage_scores(0, 0)
    stage_scores(1, 1)
    for c in range(2):
        alpha, p = probs(c, s_diag[c], jnp.max(s_diag[c], axis=0, keepdims=True), 0.0)
        accumulate(c, alpha, vt_ref[qi], p)
    stage_softmax(0, 0)

    def pair_body(mi, carry):
        n0 = 2 + 2 * mi
        stage_scores(n0, 0)
        stage_values(n0 - 2, 0)
        stage_softmax(n0 - 1, 1)
        stage_scores(n0 + 1, 1)
        stage_values(n0 - 1, 1)
        stage_softmax(n0, 0)
        return carry

    lax.fori_loop(0, (n_off + 1) // 2, pair_body, 0)

    lp = lam_ref[...]
    lam = (jnp.exp(jnp.sum(lp[0:1] * lp[1:2], axis=-1, keepdims=True))
           - jnp.exp(jnp.sum(lp[2:3] * lp[3:4], axis=-1, keepdims=True)) + LAMBDA_INIT)
    num = [acc_ref[c, 0:ATTN_VDIM, :] for c in range(2)]
    den = [acc_ref[c, ATTN_VDIM:ATTN_VDIM + 1, :] for c in range(2)]
    o_t = num[0] / den[0] - lam * (num[1] / den[1])
    o = o_t.T
    ms = jnp.mean(o * o, axis=-1, keepdims=True)
    o = o * lax.rsqrt(ms + NORM_EPS) * sub_ref[...] * (1.0 - LAMBDA_INIT)
    o_ref[0] = o.astype(o_ref.dtype)


def _diff_attention(qkv, lam_params, subln, *, t):
    B, S, _ = qkv.shape
    nt = S // t
    slopes = jnp.asarray([2.0 ** (-8.0 * (h + 1.0) / ATTN_HEADS) for h in range(ATTN_HEADS)], F32)
    grid_spec = pltpu.PrefetchScalarGridSpec(
        num_scalar_prefetch=1,
        grid=(B, ATTN_HEADS, nt),
        in_specs=[pl.BlockSpec((4, ATTN_QKDIM), lambda b, h, i, s: (0, 0)),
                  pl.BlockSpec((1, ATTN_VDIM), lambda b, h, i, s: (0, 0)),
                  pl.BlockSpec((1, t, LANES), lambda b, h, i, s: (b, i, h)),
                  pl.BlockSpec((1, S, LANES), lambda b, h, i, s: (b, 0, ATTN_HEADS + h)),
                  pl.BlockSpec((1, S, LANES), lambda b, h, i, s: (b, 0, 2 * ATTN_HEADS + h))],
        out_specs=pl.BlockSpec((1, t, LANES), lambda b, h, i, s: (b, i, h)),
        scratch_shapes=[pltpu.VMEM((2, nt, t, LANES), BF16),
                        pltpu.VMEM((nt, V_ROWS, t), BF16),
                        pltpu.VMEM((1, 1), F32),
                        pltpu.VMEM((2, 2, t, LANES), BF16),
                        pltpu.VMEM((t, t), F32),
                        pltpu.VMEM((2, 2, t, LANES), BF16),
                        pltpu.VMEM((2, 1, t), F32),
                        pltpu.VMEM((2, V_ROWS, t), F32),
                        pltpu.VMEM((2, 2, t, t), F32),
                        pltpu.VMEM((2, 2, 1, t), F32),
                        pltpu.VMEM((2, 2, t, t), BF16),
                        pltpu.VMEM((2, 2, 1, t), F32)],
    )
    return pl.pallas_call(
        functools.partial(_attn_kernel, t=t, seq=S),
        out_shape=jax.ShapeDtypeStruct((B, S, ATTN_WIDTH), BF16),
        grid_spec=grid_spec,
        compiler_params=pltpu.CompilerParams(dimension_semantics=("parallel", "parallel", "arbitrary")),
        name="diff_attn",
    )(slopes, lam_params, subln, qkv, qkv, qkv)


def _prep_kernel(z_ref, zp_ref, zn_ref, tp_ref, tn_ref, w0_ref, w2_ref, a0_ref, a2_ref, g2_ref,
                 kk_ref, ka_ref, rk_ref,
                 r_o, v_o, kk_o, k0_o, k1_o, b0_o, b1_o, lw0_o, lw1_o, bonus_o, gate_o):
    i = pl.program_id(1)
    nt = pl.num_programs(1)
    z = z_ref[0]
    ts = z.shape[0]
    row = lax.broadcasted_iota(jnp.int32, (ts, 1), 0)
    prev_row = jnp.where(i > 0, zp_ref[0, 7:8, :], 0.0)
    next_row = jnp.where(i < nt - 1, zn_ref[0, 0:1, :], 0.0)
    z_prev = jnp.where(row == 0, prev_row, pltpu.roll(z, 1, 0))
    z_next = jnp.where(row == ts - 1, next_row, pltpu.roll(z, ts - 1, 0))
    zs = z + tp_ref[...] * (z_prev - z) + tn_ref[...] * (z_next - z)

    W = RWKV_WIDTH
    r = zs[:, 0:W]
    k = zs[:, W:2 * W]
    v = zs[:, 2 * W:3 * W]
    wd = jnp.tanh(zs[:, 3 * W:3 * W + LANES])
    ad = zs[:, 3 * W + LANES:3 * W + 2 * LANES]
    gd = zs[:, 3 * W + 2 * LANES:]

    w_log = w0_ref[...] + _dot3(wd, w2_ref[...])
    lw = -_sigmoid(w_log) * math.exp(-0.5)
    iclr = _sigmoid(a0_ref[...] + _dot(ad.astype(BF16), a2_ref[...].astype(BF16)))
    gate = _dot(_sigmoid(gd).astype(BF16), g2_ref[...].astype(BF16))

    ones = _head_ones(LANES)
    kk = k * kk_ref[...]
    kk = kk * lax.rsqrt(jnp.maximum(_head_sum(kk * kk, ones), 1e-24))
    ka = ka_ref[...]
    rk = rk_ref[...]
    bonus = jnp.zeros_like(r)
    k_outs = (k0_o, k1_o)
    b_outs = (b0_o, b1_o)
    lw_outs = (lw0_o, lw1_o)
    for d in range(N_DIR):
        a_d = iclr[:, d * W:(d + 1) * W]
        k_d = k * (1.0 + (a_d - 1.0) * ka)
        k_outs[d][0] = k_d.astype(k_outs[d].dtype)
        b_outs[d][0] = (kk * a_d).astype(b_outs[d].dtype)
        lw_outs[d][0] = lw[:, d * W:(d + 1) * W]
        bonus = bonus + _head_sum(r * k_d * rk, ones, two_pass=False) * v
    r_o[0] = r.astype(r_o.dtype)
    v_o[0] = v.astype(v_o.dtype)
    kk_o[0] = kk.astype(kk_o.dtype)
    bonus_o[0] = bonus
    gate_o[0] = gate.astype(gate_o.dtype)


def _rwkv_prep(z, tp, tn, w0, w2bd, a0, a2bd, g2p, k_k, k_a, r_k, *, ts):
    B, S, ZC = z.shape
    nt = S // ts
    hb = ts // 8
    last8 = S // 8 - 1
    const = lambda shape: pl.BlockSpec(shape, lambda b, i: (0, 0))
    out_dtypes = (BF16,) * 7 + (F32, F32, F32, BF16)
    out_spec = pl.BlockSpec((1, ts, RWKV_WIDTH), lambda b, i: (b, i, 0))
    return pl.pallas_call(
        _prep_kernel,
        out_shape=tuple(jax.ShapeDtypeStruct((B, S, RWKV_WIDTH), dt) for dt in out_dtypes),
        grid=(B, nt),
        in_specs=[pl.BlockSpec((1, ts, ZC), lambda b, i: (b, i, 0)),
                  pl.BlockSpec((1, 8, ZC), lambda b, i: (b, jnp.maximum(i * hb - 1, 0), 0)),
                  pl.BlockSpec((1, 8, ZC), lambda b, i: (b, jnp.minimum((i + 1) * hb, last8), 0)),
                  const((1, ZC)), const((1, ZC)),
                  const((1, N_DIR * RWKV_WIDTH)), const((LANES, N_DIR * RWKV_WIDTH)),
                  const((1, N_DIR * RWKV_WIDTH)), const((LANES, N_DIR * RWKV_WIDTH)),
                  const((GATE_LORA_PAD, RWKV_WIDTH)),
                  const((1, RWKV_WIDTH)), const((1, RWKV_WIDTH)), const((1, RWKV_WIDTH))],
        out_specs=(out_spec,) * 11,
        compiler_params=pltpu.CompilerParams(dimension_semantics=("parallel", "arbitrary")),
        name="rwkv_prep",
    )(z, z, z, tp, tn, w0, w2bd, a0, a2bd, g2p, k_k, k_a, r_k)


def _scan_kernel(*refs, tb):
    n_in, n_scr = 6, 5
    ins = (refs[0:n_in], refs[n_in:2 * n_in])
    outs = refs[2 * n_in:2 * n_in + 2]
    scr = (refs[2 * n_in + 2:2 * n_in + 2 + n_scr], refs[2 * n_in + 2 + n_scr:])

    @pl.when(pl.program_id(2) == 0)
    def _():
        for ref in scr[0] + scr[1]:
            ref[...] = jnp.zeros_like(ref)

    last = pl.num_programs(2) - 1

    @pl.when(pl.program_id(2) < last)
    def _():
        live = [_scan_stages(*ins[d], outs[d], *scr[d], tb=tb, reverse=bool(d)) for d in range(N_DIR)]
        while live:
            for g in list(live):
                if next(g, "done") == "done":
                    live.remove(g)

    @pl.when(pl.program_id(2) == last)
    def _():
        nc = tb // CHUNK
        state = [scr[d][0][...] for d in range(N_DIR)]
        for n in range(nc):
            for d in range(N_DIR):
                s_ref, rp_ref, y0_ref, pm_ref, qm_ref = scr[d]
                j = (nc - 1 - n) if d else n
                s16 = state[d].astype(BF16)
                outs[d][0, j * CHUNK:(j + 1) * CHUNK, :] = _dot(rp_ref[n], s16) + y0_ref[n]
                state[d] = _dot(pm_ref[n], s16) + qm_ref[n]


def _scan_stages(r_ref, kk_ref, k_ref, b_ref, v_ref, lw_ref, y_ref, s_ref, rp_ref, y0_ref, pm_ref, qm_ref,
                 *, tb, reverse):
    C = CHUNK
    W = LANES
    nc = tb // C
    order = [(nc - 1 - jj) if reverse else jj for jj in range(nc)]
    sls = [slice(j * C, (j + 1) * C) for j in order]
    ends = [j * C if reverse else (j + 1) * C - 1 for j in order]

    lw = lw_ref[0]
    kk32, r32, b32, k32, v_all = (x[0].astype(F32) for x in (kk_ref, r_ref, b_ref, k_ref, v_ref))
    chain = {"s": s_ref[...], "n": 0}

    def chain_step():
        n = chain["n"]
        if n < nc:
            s16 = chain["s"].astype(BF16)
            y_ref[0, sls[n], :] = _dot(rp_ref[n], s16) + y0_ref[n]
            chain["s"] = _dot(pm_ref[n], s16) + qm_ref[n]
            chain["n"] = n + 1

    pos = lax.broadcasted_iota(jnp.int32, (tb, 1), 0) % C
    c = lw
    shift = 1
    while shift < C:
        if reverse:
            c = c + jnp.where(pos < C - shift, pltpu.roll(c, tb - shift, 0), 0.0)
        else:
            c = c + jnp.where(pos >= shift, pltpu.roll(c, shift, 0), 0.0)
        shift *= 2
    d = jnp.concatenate([c[e:e + 1, :] - c[j * C:(j + 1) * C] for j, e in
                         sorted(zip(order, ends))], axis=0)

    e_c = jnp.exp(c)
    e_nc = jnp.exp(-c)
    e_d = jnp.exp(d)
    at_all = -kk32 * jnp.exp(c - lw)
    rt_all = r32 * e_c
    bt_all = b32 * e_nc
    kt_all = k32 * e_nc
    bh_all = b32 * e_d
    kh_all = k32 * e_d
    yield

    r2 = lax.broadcasted_iota(jnp.int32, (W, W), 0)
    c2 = lax.broadcasted_iota(jnp.int32, (W, W), 1)
    bd_mask = (r2 // C) == (c2 // RWKV_HEAD)
    eye = r2 == c2
    tr = lax.broadcasted_iota(jnp.int32, (C, W), 0)
    sc = lax.broadcasted_iota(jnp.int32, (C, W), 1) % C
    strict = (sc > tr) if reverse else (sc < tr)
    incl = (sc >= tr) if reverse else (sc <= tr)

    def bd(x):
        return jnp.where(bd_mask, jnp.concatenate([x] * PAIR, axis=0), 0.0).astype(BF16)

    def bd2(xa, xb):
        return jnp.concatenate([bd(xa), bd(xb)], axis=1)

    At = [at_all[sl] for sl in sls]
    Rt = [rt_all[sl] for sl in sls]
    V = [v_all[sl] for sl in sls]
    bdV = [bd(x) for x in V]
    lhs = [jnp.concatenate([a, r], axis=0).astype(BF16) for a, r in zip(At, Rt)]
    sbk = [_dot_nt(l, jnp.concatenate([bd(bt_all[sl]), bd(kt_all[sl])], axis=0)) for l, sl in zip(lhs, sls)]
    sb = [x[:, :W] for x in sbk]
    sk = [x[:, W:] for x in sbk]
    lp = [jnp.where(strict, x[:C], 0.0) for x in sb]
    Lak = [jnp.where(strict, x[:C], 0.0).astype(BF16) for x in sk]
    Mrb = [jnp.where(incl, x[C:], 0.0).astype(BF16) for x in sb]
    Mrk = [jnp.where(incl, x[C:], 0.0).astype(BF16) for x in sk]
    chain_step()
    yield
    xa = list(At)
    xu = [_dot(l, b) for l, b in zip(Lak, bdV)]
    chain_step()
    yield
    n_dbl = C.bit_length() - 1
    for it in range(n_dbl):
        lp16 = [x.astype(BF16) for x in lp]
        if it + 1 < n_dbl:
            upd = [_dot(l16, jnp.concatenate([bd2(a, u), bd(l)], axis=1))
                   for l16, l, a, u in zip(lp16, lp, xa, xu)]
            lp = [x[:, 2 * W:] for x in upd]
        else:
            upd = [_dot(l, bd2(a, u)) for l, a, u in zip(lp16, xa, xu)]
        xa = [a + x[:, :W] for a, x in zip(xa, upd)]
        xu = [u + x[:, W:2 * W] for u, x in zip(xu, upd)]
        chain_step()
        yield
    rb = [_dot(m, bd2(a, u)) for m, a, u in zip(Mrb, xa, xu)]
    while chain["n"] < nc:
        chain_step()
    s_ref[...] = chain["s"]
    yield
    au = [jnp.concatenate([a, u], axis=1).astype(BF16) for a, u in zip(xa, xu)]
    pb = [_dot_tn(bh_all[sl].astype(BF16), x) for sl, x in zip(sls, au)]
    kv = [_dot_tn(kh_all[sl].astype(BF16), x.astype(BF16)) for sl, x in zip(sls, V)]
    yield
    for n in range(nc):
        rp_ref[n] = (Rt[n] + rb[n][:, :W]).astype(BF16)
        y0_ref[n] = rb[n][:, W:] + _dot(Mrk[n], bdV[n])
        pm_ref[n] = (jnp.where(bd_mask, pb[n][:, :W], 0.0)
                     + jnp.where(eye, e_c[ends[n]:ends[n] + 1, :], 0.0)).astype(BF16)
        qm_ref[n] = jnp.where(bd_mask, pb[n][:, W:] + kv[n], 0.0)


def _rwkv_scan(r, kk, k_dirs, b_dirs, v, lw_dirs, *, tb):
    B, S, Wd = r.shape
    nblk = S // tb
    nc = tb // CHUNK
    in_specs, out_specs = [], []
    for rev in (False, True):
        flip = (lambda n: nblk - 1 - n) if rev else (lambda n: n)
        in_idx = lambda b, p, i, flip=flip: (b, flip(jnp.minimum(i, nblk - 1)), p)
        out_idx = lambda b, p, i, flip=flip: (b, flip(jnp.maximum(i - 1, 0)), p)
        in_specs += [pl.BlockSpec((1, tb, LANES), in_idx)] * 6
        out_specs.append(pl.BlockSpec((1, tb, LANES), out_idx))
    scratch = [pltpu.VMEM((LANES, LANES), F32),
               pltpu.VMEM((nc, CHUNK, LANES), BF16), pltpu.VMEM((nc, CHUNK, LANES), F32),
               pltpu.VMEM((nc, LANES, LANES), BF16), pltpu.VMEM((nc, LANES, LANES), F32)]
    operands = [x for d in range(N_DIR) for x in (r, kk, k_dirs[d], b_dirs[d], v, lw_dirs[d])]
    return pl.pallas_call(
        functools.partial(_scan_kernel, tb=tb),
        out_shape=(jax.ShapeDtypeStruct((B, S, Wd), F32),) * N_DIR,
        grid=(B, Wd // LANES, nblk + 1),
        in_specs=in_specs,
        out_specs=tuple(out_specs),
        scratch_shapes=scratch * N_DIR,
        compiler_params=pltpu.CompilerParams(dimension_semantics=("parallel", "parallel", "arbitrary")),
        name="rwkv_scan",
    )(*operands)


def _outproj_kernel(xp_ref, xs_ref, attn_ref, yf_ref, yb_ref, bonus_ref, gate_ref, lw_ref, lb_ref, w_ref, g_ref,
                    o_ref, *, mp):
    y = yf_ref[...] + yb_ref[...]
    mean_mat = _head_ones(LANES)
    inv_n = 1.0 / RWKV_HEAD
    mu = _head_sum(y, mean_mat) * inv_n
    yc = y - mu
    var = _head_sum(yc * yc, mean_mat, two_pass=False) * inv_n
    yn = yc * lax.rsqrt(var + LNX_EPS) * lw_ref[...] + lb_ref[...]
    rw = ((yn + bonus_ref[...]) * gate_ref[...]).astype(BF16)
    m = _dot(attn_ref[...], w_ref[0:ATTN_WIDTH, :]) + _dot(rw, w_ref[ATTN_WIDTH:, :])
    ms = jnp.mean(m * m, axis=-1, keepdims=True)
    upd = m * lax.rsqrt(ms + NORM_EPS) * g_ref[...]

    @pl.when(pl.program_id(0) < mp)
    def _():
        o_ref[...] = xp_ref[...] + upd

    @pl.when(pl.program_id(0) >= mp)
    def _():
        o_ref[...] = xs_ref[...] + upd


def _outproj(xp2d, xs2d, attn2d, yf, yb, bonus, gate, lnx_w, lnx_b, w_out_bf16, gain, *, tm):
    mp = xp2d.shape[0] // tm
    T = xp2d.shape[0] + xs2d.shape[0]
    row = lambda w: pl.BlockSpec((tm, w), lambda m: (m, 0))
    const = lambda shape: pl.BlockSpec(shape, lambda m: (0, 0))
    return pl.pallas_call(
        functools.partial(_outproj_kernel, mp=mp),
        out_shape=jax.ShapeDtypeStruct((T, D_MODEL), F32),
        grid=(T // tm,),
        in_specs=[*_row_block_specs(tm, D_MODEL, mp),
                  row(ATTN_WIDTH), row(RWKV_WIDTH), row(RWKV_WIDTH), row(RWKV_WIDTH), row(RWKV_WIDTH),
                  const((1, RWKV_WIDTH)), const((1, RWKV_WIDTH)),
                  const((D_MODEL, D_MODEL)), const((1, D_MODEL))],
        out_specs=row(D_MODEL),
        compiler_params=pltpu.CompilerParams(dimension_semantics=("parallel",)),
        name="outproj",
    )(xp2d, xs2d, attn2d, yf, yb, bonus, gate, lnx_w, lnx_b, w_out_bf16, gain)


def _ffn_kernel(x_ref, gpre_ref, wg_ref, wu_ref, wd_ref, gpost_ref, op_ref, os_ref, h_ref, acc_ref, *, mp):
    j = pl.program_id(1)

    @pl.when(j == 0)
    def _():
        x = x_ref[...]
        ms = jnp.mean(x * x, axis=-1, keepdims=True)
        h_ref[...] = (x * lax.rsqrt(ms + NORM_EPS) * gpre_ref[...]).astype(BF16)
        acc_ref[...] = jnp.zeros_like(acc_ref)

    h = h_ref[...]
    g = _dot(h, wg_ref[...])
    u = _dot(h, wu_ref[...])
    a = (g * _sigmoid(g) * u).astype(BF16)
    acc_ref[...] += _dot(a, wd_ref[...])

    def finish(o_ref):
        f = acc_ref[...]
        ms = jnp.mean(f * f, axis=-1, keepdims=True)
        o_ref[...] = x_ref[...] + f * lax.rsqrt(ms + NORM_EPS) * gpost_ref[...]

    last = j == pl.num_programs(1) - 1

    @pl.when(last & (pl.program_id(0) < mp))
    def _():
        finish(op_ref)

    @pl.when(last & (pl.program_id(0) >= mp))
    def _():
        finish(os_ref)


def _ffn(x2d, gpre, wg, wu, wd, gpost, *, tm, tf, rows_first):
    T = x2d.shape[0]
    mp = rows_first // tm
    return pl.pallas_call(
        functools.partial(_ffn_kernel, mp=mp),
        out_shape=(jax.ShapeDtypeStruct((rows_first, D_MODEL), F32),
                   jax.ShapeDtypeStruct((T - rows_first, D_MODEL), F32)),
        grid=(T // tm, D_FF // tf),
        in_specs=[pl.BlockSpec((tm, D_MODEL), lambda m, j: (m, 0)),
                  pl.BlockSpec((1, D_MODEL), lambda m, j: (0, 0)),
                  pl.BlockSpec((D_MODEL, tf), lambda m, j: (0, j)),
                  pl.BlockSpec((D_MODEL, tf), lambda m, j: (0, j)),
                  pl.BlockSpec((tf, D_MODEL), lambda m, j: (j, 0)),
                  pl.BlockSpec((1, D_MODEL), lambda m, j: (0, 0))],
        out_specs=_row_block_specs(tm, D_MODEL, mp),
        scratch_shapes=[pltpu.VMEM((tm, D_MODEL), BF16), pltpu.VMEM((tm, D_MODEL), F32)],
        compiler_params=pltpu.CompilerParams(dimension_semantics=("arbitrary", "arbitrary")),
        name="ffn",
    )(x2d, gpre, wg, wu, wd, gpost)


def _tiles(S):
    return dict(tmi=256, tm=512, ta=512, ts=256, tb=512, tf=512)


def _lora_blockdiag(w):
    zero = jnp.zeros_like(w[0])
    return jnp.concatenate([jnp.concatenate([w[0], zero], axis=1),
                            jnp.concatenate([zero, w[1]], axis=1)], axis=0)


def _encoder_layer(xp, xs, norm_mix_pre, norm_mix_post, w_in, w_out, lambda_q1, lambda_k1, lambda_q2, lambda_k2,
                   attn_subln, tshift_prev, tshift_next, w0, w2, a0, a2, g2, k_k, k_a, r_k, lnx_w, lnx_b,
                   norm_ffn_pre, norm_ffn_post, w_gate, w_up, w_down):
    (Bp, S, D), Bs = xp.shape, xs.shape[0]
    B = Bp + Bs
    T = B * S
    t = _tiles(S)
    row = lambda a: a.reshape(1, -1).astype(F32)
    pad_cols = RWKV_COLS_PAD - RWKV_COLS

    col_scale = jnp.where(jnp.arange(w_in.shape[1]) < ATTN_WIDTH, Q_PRESCALE, 1.0).astype(F32)
    w_in_p = jnp.pad(w_in * col_scale, ((0, 0), (0, pad_cols))).astype(BF16)
    xp2d, xs2d = xp.reshape(Bp * S, D), xs.reshape(Bs * S, D)
    qkv, z = _inproj(xp2d, xs2d, row(norm_mix_pre), w_in_p, tm=t["tmi"])

    lam_params = jnp.stack([lambda_q1, lambda_k1, lambda_q2, lambda_k2]).astype(F32)
    attn = _diff_attention(qkv.reshape(B, S, QKV_COLS), lam_params, row(attn_subln), t=t["ta"])

    tp = jnp.pad(tshift_prev, (0, pad_cols)).reshape(1, -1)
    tn_ = jnp.pad(tshift_next, (0, pad_cols)).reshape(1, -1)
    g2p = jnp.pad(g2, ((0, GATE_LORA_PAD - GATE_LORA), (0, 0))).astype(BF16)
    (r, v, kk, k0, k1, b0, b1, lw0, lw1, bonus, gate) = _rwkv_prep(
        z.reshape(B, S, RWKV_COLS_PAD), tp, tn_, row(w0), _lora_blockdiag(w2), row(a0),
        _lora_blockdiag(a2).astype(BF16), g2p, row(k_k), row(k_a), row(r_k), ts=t["ts"])
    yf, yb = _rwkv_scan(r, kk, (k0, k1), (b0, b1), v, (lw0, lw1), tb=t["tb"])

    flat = lambda a: a.reshape(T, -1)
    x1 = _outproj(xp2d, xs2d, flat(attn), flat(yf), flat(yb), flat(bonus), flat(gate), row(lnx_w), row(lnx_b),
                  w_out.astype(BF16), row(norm_mix_post), tm=t["tm"])
    yp, ys = _ffn(x1, row(norm_ffn_pre), w_gate.astype(BF16), w_up.astype(BF16), w_down.astype(BF16),
                  row(norm_ffn_post), tm=t["tm"], tf=t["tf"], rows_first=Bp * S)
    return yp.reshape(Bp, S, D), ys.reshape(Bs, S, D)


def kernel(x_prompt, x_sample, norm_mix_pre, norm_mix_post, w_in, w_out, lambda_q1, lambda_k1, lambda_q2,
           lambda_k2, attn_subln, tshift_prev, tshift_next, w0, w2, a0, a2, g2, k_k, k_a, r_k, lnx_w, lnx_b,
           norm_ffn_pre, norm_ffn_post, w_gate, w_up, w_down):
    assert x_prompt.shape[1:] == x_sample.shape[1:], "both trunks share the sequence length"
    assert norm_mix_pre.shape[0] == 1, "single layer"
    return _encoder_layer(x_prompt, x_sample, norm_mix_pre[0], norm_mix_post[0], w_in[0], w_out[0], lambda_q1[0],
                          lambda_k1[0], lambda_q2[0], lambda_k2[0], attn_subln[0], tshift_prev[0], tshift_next[0],
                          w0[0], w2[0], a0[0], a2[0], g2[0], k_k[0], k_a[0], r_k[0], lnx_w[0], lnx_b[0],
                          norm_ffn_pre[0], norm_ffn_post[0], w_gate[0], w_up[0], w_down[0])
```

```python
import functools
import math

import jax
import jax.numpy as jnp
from jax import lax
from jax.experimental import pallas as pl
from jax.experimental.pallas import tpu as pltpu

F32 = jnp.float32
BF16 = jnp.bfloat16

D_MODEL = 2048
ATTN_HEADS = 8
ATTN_VDIM = 128
ATTN_QKDIM = 64
ATTN_WIDTH = ATTN_HEADS * ATTN_VDIM
RWKV_HEAD = 64
RWKV_HEADS = 16
RWKV_WIDTH = RWKV_HEAD * RWKV_HEADS
N_DIR = 2
LORA = 64
GATE_LORA = 160
GATE_LORA_PAD = 256
QKV_COLS = 3 * ATTN_WIDTH
RWKV_COLS = 3 * RWKV_WIDTH + 2 * N_DIR * LORA + GATE_LORA
RWKV_COLS_PAD = 3 * RWKV_WIDTH + 2 * N_DIR * LORA + GATE_LORA_PAD
D_FF = 5632
NORM_EPS = 1e-6
LNX_EPS = 64e-5
LAMBDA_INIT = 0.8 - 0.6 * math.exp(-0.3 * 0)

LANES = 128
CHUNK = 64
PAIR = LANES // RWKV_HEAD

def _dot(a, b):
    return jnp.dot(a, b, preferred_element_type=F32)


def _dot_nt(a, b):
    return lax.dot_general(a, b, (((1,), (1,)), ((), ())), preferred_element_type=F32)


def _dot_tn(a, b):
    return lax.dot_general(a, b, (((0,), (0,)), ((), ())), preferred_element_type=F32)


def _split(x):
    hi = x.astype(BF16)
    lo = (x - hi.astype(F32)).astype(BF16)
    return hi, lo


def _dot3(a, b):
    ah, al = _split(a)
    bh, bl = _split(b)
    return _dot(ah, bh) + _dot(al, bh) + _dot(ah, bl)


def _sigmoid(x):
    return 1.0 / (1.0 + jnp.exp(-x))


def _head_ones(width):
    r = lax.broadcasted_iota(jnp.int32, (width, width), 0) // RWKV_HEAD
    c = lax.broadcasted_iota(jnp.int32, (width, width), 1) // RWKV_HEAD
    return jnp.where(r == c, 1.0, 0.0).astype(BF16)


def _head_sum(x, ones, two_pass=True):
    outs = []
    for g in range(x.shape[1] // LANES):
        xg = x[:, g * LANES:(g + 1) * LANES]
        if two_pass:
            hi, lo = _split(xg)
            outs.append(_dot(hi, ones) + _dot(lo, ones))
        else:
            outs.append(_dot(xg.astype(BF16), ones))
    return jnp.concatenate(outs, axis=1)


def _row_block_specs(tm, width, mp):
    return (pl.BlockSpec((tm, width), lambda m, *_: (jnp.minimum(m, mp - 1), 0)),
            pl.BlockSpec((tm, width), lambda m, *_: (jnp.maximum(m - mp, 0), 0)))


def _inproj_kernel(xp_ref, xs_ref, g_ref, w_ref, qkv_ref, z_ref, h_ref, *, mp):
    m = pl.program_id(0)

    def norm(x_ref):
        x = x_ref[...]
        ms = jnp.mean(x * x, axis=-1, keepdims=True)
        h_ref[...] = (x * lax.rsqrt(ms + NORM_EPS) * g_ref[...]).astype(BF16)

    @pl.when(m < mp)
    def _():
        norm(xp_ref)

    @pl.when(m >= mp)
    def _():
        norm(xs_ref)

    h = h_ref[...]
    qkv_ref[...] = _dot(h, w_ref[:, 0:QKV_COLS]).astype(BF16)
    z_ref[...] = _dot(h, w_ref[:, QKV_COLS:])


def _inproj(xp2d, xs2d, gain, w_in_bf16, *, tm):
    mp = xp2d.shape[0] // tm
    T = xp2d.shape[0] + xs2d.shape[0]
    n_cols = w_in_bf16.shape[1]
    return pl.pallas_call(
        functools.partial(_inproj_kernel, mp=mp),
        out_shape=(jax.ShapeDtypeStruct((T, QKV_COLS), BF16),
                   jax.ShapeDtypeStruct((T, RWKV_COLS_PAD), F32)),
        grid=(T // tm,),
        in_specs=[*_row_block_specs(tm, D_MODEL, mp),
                  pl.BlockSpec((1, D_MODEL), lambda m: (0, 0)),
                  pl.BlockSpec((D_MODEL, n_cols), lambda m: (0, 0), pipeline_mode=pl.Buffered(1))],
        out_specs=(pl.BlockSpec((tm, QKV_COLS), lambda m: (m, 0)),
                   pl.BlockSpec((tm, RWKV_COLS_PAD), lambda m: (m, 0))),
        scratch_shapes=[pltpu.VMEM((tm, D_MODEL), BF16)],
        compiler_params=pltpu.CompilerParams(dimension_semantics=("parallel",)),
        name="inproj",
    )(xp2d, xs2d, gain, w_in_bf16)


LOG2E = math.log2(math.e)
Q_PRESCALE = ATTN_QKDIM ** -0.5 * LOG2E
AUX = 6
NEG_BIG = -1e30
SKIP_BITS = 80.0
NORM_SLACK = 1.01
BF16_ROWS = 16
V_ROWS = ATTN_VDIM + BF16_ROWS


def _pos_aux(slope2, n, first, sign_pos, sign_one):
    row = lax.broadcasted_iota(jnp.int32, (n, LANES), 0).astype(F32)
    lane = lax.broadcasted_iota(jnp.int32, (n, LANES), 1)
    val = slope2 * row * sign_pos
    hi = val.astype(BF16).astype(F32)
    mid = (val - hi).astype(BF16).astype(F32)
    lo = (val - hi - mid).astype(BF16).astype(F32)
    out = jnp.where(lane == first, hi, 0.0)
    out = jnp.where(lane == first + 1, mid, out)
    out = jnp.where(lane == first + 2, lo, out)
    out = jnp.where((lane >= first + 3) & (lane < first + AUX), sign_one, out)
    return out.astype(BF16)


def _attn_kernel(slopes_ref, lam_ref, sub_ref, q_ref, k_ref, v_ref, o_ref,
                 ka_ref, vt_ref, kn_ref, qa_ref, bd_ref, qv_ref, m_ref, acc_ref, s_ref, sm_ref, p_ref, al_ref,
                 *, t, seq):
    h = pl.program_id(1)
    qi = pl.program_id(2)
    nt = seq // t
    slope2 = slopes_ref[h] * LOG2E
    lane = lax.broadcasted_iota(jnp.int32, (t, LANES), 1)
    own = (lane < ATTN_QKDIM, lane >= ATTN_QKDIM)
    aux0 = (ATTN_QKDIM, 0)

    @pl.when(qi == 0)
    def _():
        ak = [_pos_aux(slope2, t, aux0[c], 1.0, 1.0) for c in range(2)]
        ones_rows = jnp.where(lax.broadcasted_iota(jnp.int32, (BF16_ROWS, t), 0) == 0, 1.0, 0.0).astype(BF16)

        def build(j, kn):
            off = pl.multiple_of(j * t, t)
            kt = k_ref[0, pl.ds(off, t), :]
            for c in range(2):
                ka_ref[c, j] = jnp.where(own[c], kt, ak[c])
            vt_ref[j, 0:ATTN_VDIM, :] = v_ref[0, pl.ds(off, t), :].astype(F32).T.astype(BF16)
            vt_ref[j, ATTN_VDIM:, :] = ones_rows
            k32 = kt.astype(F32)
            ksq = k32 * k32
            rows = jnp.maximum(jnp.sum(jnp.where(own[0], ksq, 0.0), axis=1, keepdims=True),
                               jnp.sum(jnp.where(own[1], ksq, 0.0), axis=1, keepdims=True))
            return jnp.maximum(kn, jnp.max(rows, axis=0, keepdims=True))

        kn_ref[...] = lax.fori_loop(0, nt, build, jnp.zeros((1, 1), F32))

        for c in range(2):
            a = aux0[c]
            lane_pos = (lane >= a + 3) & (lane < a + AUX)
            lane_one = (lane >= a) & (lane < a + 3)
            pos = _pos_aux(slope2, t, a + 3, 1.0, 0.0).astype(F32)
            qa_ref[c, 0] = jnp.where(lane_one, 1.0, jnp.where(lane_pos, -pos, 0.0)).astype(BF16)
            qa_ref[c, 1] = jnp.where(lane_one, -1.0, jnp.where(lane_pos, pos, 0.0)).astype(BF16)
        rel = (lax.broadcasted_iota(jnp.int32, (t, t), 0)
               - lax.broadcasted_iota(jnp.int32, (t, t), 1)).astype(F32)
        bd_ref[...] = -slope2 * jnp.abs(rel)

    q = q_ref[0]
    zero = jnp.zeros((t, LANES), BF16)
    for c in range(2):
        for side in range(2):
            qv_ref[c, side] = jnp.where(own[c], q, qa_ref[c, side])
    q_diag = [jnp.where(own[c], q, zero) for c in range(2)]

    acc_ref[...] = jnp.zeros(acc_ref.shape, F32)

    q32 = q.astype(F32)
    qsq = q32 * q32
    qn = jnp.maximum(jnp.sum(jnp.where(own[0], qsq, 0.0), axis=1, keepdims=True),
                     jnp.sum(jnp.where(own[1], qsq, 0.0), axis=1, keepdims=True))
    qk_bound = 2.0 * NORM_SLACK * jnp.sqrt(jnp.max(qn, axis=0, keepdims=True) * kn_ref[...])
    dist_needed = (qk_bound + SKIP_BITS) / slope2
    reach = jnp.clip(jnp.ceil((dist_needed - 1.0) / t), 0.0, nt - 1.0)[0, 0].astype(jnp.int32)
    n_left = jnp.minimum(qi, reach)
    n_off = n_left + jnp.minimum(nt - 1 - qi, reach)

    def key_tile(n):
        return jnp.clip(jnp.where(n < n_left, qi - 1 - n, qi + 1 + n - n_left), 0, nt - 1)

    def probs(c, s, smax, cst):
        m_old = m_ref[c]
        m_new = jnp.maximum(m_old, smax + cst)
        m_ref[c] = m_new
        return jnp.exp2(m_old - m_new), jnp.exp2(s - (m_new - cst)).astype(BF16)

    def accumulate(c, alpha, vt, p):
        acc_ref[c] = alpha * acc_ref[c] + _dot(vt, p)

    def stage_scores(n, slot):
        j = key_tile(n)
        for c in range(2):
            s = _dot_nt(ka_ref[c, j], qv_ref[c, (j > qi).astype(jnp.int32)])
            s_ref[slot, c] = s
            sm_ref[slot, c] = jnp.max(s, axis=0, keepdims=True)

    def stage_softmax(n, slot):
        j = key_tile(n)
        cst = jnp.where(n < n_off, -slope2 * (jnp.abs(qi - j) * t).astype(F32), NEG_BIG)
        for c in range(2):
            alpha, p = probs(c, s_ref[slot, c], sm_ref[slot, c], cst)
            p_ref[slot, c] = p
            al_ref[slot, c] = alpha

    def stage_values(n, slot):
        j = key_tile(n)
        for c in range(2):
            accumulate(c, al_ref[slot, c], vt_ref[j], p_ref[slot, c])

    m_ref[...] = jnp.full(m_ref.shape, NEG_BIG, F32)
    s_diag = [_dot_nt(ka_ref[c, qi], q_diag[c]) + bd_ref[...] for c in range(2)]
    stage_scores(0, 0)
    stage_scores(1, 1)
    for c in range(2):
        alpha, p = probs(c, s_diag[c], jnp.max(s_diag[c], axis=0, keepdims=True), 0.0)
        accumulate(c, alpha, vt_ref[qi], p)
    stage_softmax(0, 0)

    def pair_body(mi, carry):
        n0 = 2 + 2 * mi
        stage_scores(n0, 0)
        stage_values(n0 - 2, 0)
        stage_softmax(n0 - 1, 1)
        stage_scores(n0 + 1, 1)
        stage_values(n0 - 1, 1)
        stage_softmax(n0, 0)
        return carry

    n_pairs = jnp.maximum(n_off - 2, 0) // 2
    lax.fori_loop(0, n_pairs, pair_body, 0)
    n_done = 2 * n_pairs

    @pl.when(n_off % 2 == 0)
    def _():
        stage_values(n_done, 0)
        stage_softmax(n_done + 1, 1)
        stage_values(n_done + 1, 1)

    @pl.when(n_off % 2 == 1)
    def _():
        stage_scores(n_done + 2, 0)
        stage_values(n_done, 0)
        stage_softmax(n_done + 1, 1)
        stage_values(n_done + 1, 1)
        stage_softmax(n_done + 2, 0)
        stage_values(n_done + 2, 0)

    lp = lam_ref[...]
    lam = (jnp.exp(jnp.sum(lp[0:1] * lp[1:2], axis=-1, keepdims=True))
           - jnp.exp(jnp.sum(lp[2:3] * lp[3:4], axis=-1, keepdims=True)) + LAMBDA_INIT)
    num = [acc_ref[c, 0:ATTN_VDIM, :] for c in range(2)]
    den = [acc_ref[c, ATTN_VDIM:ATTN_VDIM + 1, :] for c in range(2)]
    o_t = num[0] / den[0] - lam * (num[1] / den[1])
    o = o_t.T
    ms = jnp.mean(o * o, axis=-1, keepdims=True)
    o = o * lax.rsqrt(ms + NORM_EPS) * sub_ref[...] * (1.0 - LAMBDA_INIT)
    o_ref[0] = o.astype(o_ref.dtype)


def _diff_attention(qkv, lam_params, subln, *, t):
    B, S, _ = qkv.shape
    nt = S // t
    slopes = jnp.asarray([2.0 ** (-8.0 * (h + 1.0) / ATTN_HEADS) for h in range(ATTN_HEADS)], F32)
    grid_spec = pltpu.PrefetchScalarGridSpec(
        num_scalar_prefetch=1,
        grid=(B, ATTN_HEADS, nt),
        in_specs=[pl.BlockSpec((4, ATTN_QKDIM), lambda b, h, i, s: (0, 0)),
                  pl.BlockSpec((1, ATTN_VDIM), lambda b, h, i, s: (0, 0)),
                  pl.BlockSpec((1, t, LANES), lambda b, h, i, s: (b, i, h)),
                  pl.BlockSpec((1, S, LANES), lambda b, h, i, s: (b, 0, ATTN_HEADS + h)),
                  pl.BlockSpec((1, S, LANES), lambda b, h, i, s: (b, 0, 2 * ATTN_HEADS + h))],
        out_specs=pl.BlockSpec((1, t, LANES), lambda b, h, i, s: (b, i, h)),
        scratch_shapes=[pltpu.VMEM((2, nt, t, LANES), BF16),
                        pltpu.VMEM((nt, V_ROWS, t), BF16),
                        pltpu.VMEM((1, 1), F32),
                        pltpu.VMEM((2, 2, t, LANES), BF16),
                        pltpu.VMEM((t, t), F32),
                        pltpu.VMEM((2, 2, t, LANES), BF16),
                        pltpu.VMEM((2, 1, t), F32),
                        pltpu.VMEM((2, V_ROWS, t), F32),
                        pltpu.VMEM((2, 2, t, t), F32),
                        pltpu.VMEM((2, 2, 1, t), F32),
                        pltpu.VMEM((2, 2, t, t), BF16),
                        pltpu.VMEM((2, 2, 1, t), F32)],
    )
    return pl.pallas_call(
        functools.partial(_attn_kernel, t=t, seq=S),
        out_shape=jax.ShapeDtypeStruct((B, S, ATTN_WIDTH), BF16),
        grid_spec=grid_spec,
        compiler_params=pltpu.CompilerParams(dimension_semantics=("parallel", "parallel", "arbitrary")),
        name="diff_attn",
    )(slopes, lam_params, subln, qkv, qkv, qkv)


def _prep_kernel(z_ref, zp_ref, zn_ref, tp_ref, tn_ref, w0_ref, w2_ref, a0_ref, a2_ref, g2_ref,
                 kk_ref, ka_ref, rk_ref,
                 r_o, v_o, kk_o, k0_o, k1_o, b0_o, b1_o, lw0_o, lw1_o, bonus_o, gate_o):
    i = pl.program_id(1)
    nt = pl.num_programs(1)
    z = z_ref[0]
    ts = z.shape[0]
    row = lax.broadcasted_iota(jnp.int32, (ts, 1), 0)
    prev_row = jnp.where(i > 0, zp_ref[0, 7:8, :], 0.0)
    next_row = jnp.where(i < nt - 1, zn_ref[0, 0:1, :], 0.0)
    z_prev = jnp.where(row == 0, prev_row, pltpu.roll(z, 1, 0))
    z_next = jnp.where(row == ts - 1, next_row, pltpu.roll(z, ts - 1, 0))
    zs = z + tp_ref[...] * (z_prev - z) + tn_ref[...] * (z_next - z)

    W = RWKV_WIDTH
    r = zs[:, 0:W]
    k = zs[:, W:2 * W]
    v = zs[:, 2 * W:3 * W]
    wd = jnp.tanh(zs[:, 3 * W:3 * W + LANES])
    ad = zs[:, 3 * W + LANES:3 * W + 2 * LANES]
    gd = zs[:, 3 * W + 2 * LANES:]

    w_log = w0_ref[...] + _dot3(wd, w2_ref[...])
    lw = -_sigmoid(w_log) * math.exp(-0.5)
    iclr = _sigmoid(a0_ref[...] + _dot(ad.astype(BF16), a2_ref[...].astype(BF16)))
    gate = _dot(_sigmoid(gd).astype(BF16), g2_ref[...].astype(BF16))

    ones = _head_ones(LANES)
    kk = k * kk_ref[...]
    kk = kk * lax.rsqrt(jnp.maximum(_head_sum(kk * kk, ones), 1e-24))
    ka = ka_ref[...]
    rk = rk_ref[...]
    bonus = jnp.zeros_like(r)
    k_outs = (k0_o, k1_o)
    b_outs = (b0_o, b1_o)
    lw_outs = (lw0_o, lw1_o)
    for d in range(N_DIR):
        a_d = iclr[:, d * W:(d + 1) * W]
        k_d = k * (1.0 + (a_d - 1.0) * ka)
        k_outs[d][0] = k_d.astype(k_outs[d].dtype)
        b_outs[d][0] = (kk * a_d).astype(b_outs[d].dtype)
        lw_outs[d][0] = lw[:, d * W:(d + 1) * W]
        bonus = bonus + _head_sum(r * k_d * rk, ones, two_pass=False) * v
    r_o[0] = r.astype(r_o.dtype)
    v_o[0] = v.astype(v_o.dtype)
    kk_o[0] = kk.astype(kk_o.dtype)
    bonus_o[0] = bonus
    gate_o[0] = gate.astype(gate_o.dtype)


def _rwkv_prep(z, tp, tn, w0, w2bd, a0, a2bd, g2p, k_k, k_a, r_k, *, ts):
    B, S, ZC = z.shape
    nt = S // ts
    hb = ts // 8
    last8 = S // 8 - 1
    const = lambda shape: pl.BlockSpec(shape, lambda b, i: (0, 0))
    out_dtypes = (BF16,) * 7 + (F32, F32, F32, BF16)
    out_spec = pl.BlockSpec((1, ts, RWKV_WIDTH), lambda b, i: (b, i, 0))
    return pl.pallas_call(
        _prep_kernel,
        out_shape=tuple(jax.ShapeDtypeStruct((B, S, RWKV_WIDTH), dt) for dt in out_dtypes),
        grid=(B, nt),
        in_specs=[pl.BlockSpec((1, ts, ZC), lambda b, i: (b, i, 0)),
                  pl.BlockSpec((1, 8, ZC), lambda b, i: (b, jnp.maximum(i * hb - 1, 0), 0)),
                  pl.BlockSpec((1, 8, ZC), lambda b, i: (b, jnp.minimum((i + 1) * hb, last8), 0)),
                  const((1, ZC)), const((1, ZC)),
                  const((1, N_DIR * RWKV_WIDTH)), const((LANES, N_DIR * RWKV_WIDTH)),
                  const((1, N_DIR * RWKV_WIDTH)), const((LANES, N_DIR * RWKV_WIDTH)),
                  const((GATE_LORA_PAD, RWKV_WIDTH)),
                  const((1, RWKV_WIDTH)), const((1, RWKV_WIDTH)), const((1, RWKV_WIDTH))],
        out_specs=(out_spec,) * 11,
        compiler_params=pltpu.CompilerParams(dimension_semantics=("parallel", "arbitrary")),
        name="rwkv_prep",
    )(z, z, z, tp, tn, w0, w2bd, a0, a2bd, g2p, k_k, k_a, r_k)


def _scan_kernel(*refs, tb):
    n_in, n_scr = 6, 5
    ins = (refs[0:n_in], refs[n_in:2 * n_in])
    outs = refs[2 * n_in:2 * n_in + 2]
    scr = (refs[2 * n_in + 2:2 * n_in + 2 + n_scr], refs[2 * n_in + 2 + n_scr:])

    @pl.when(pl.program_id(2) == 0)
    def _():
        for ref in scr[0] + scr[1]:
            ref[...] = jnp.zeros_like(ref)

    last = pl.num_programs(2) - 1

    @pl.when(pl.program_id(2) < last)
    def _():
        live = [_scan_stages(*ins[d], outs[d], *scr[d], tb=tb, reverse=bool(d)) for d in range(N_DIR)]
        while live:
            for g in list(live):
                if next(g, "done") == "done":
                    live.remove(g)

    @pl.when(pl.program_id(2) == last)
    def _():
        nc = tb // CHUNK
        state = [scr[d][0][...] for d in range(N_DIR)]
        for n in range(nc):
            for d in range(N_DIR):
                s_ref, rp_ref, y0_ref, pm_ref, qm_ref = scr[d]
                j = (nc - 1 - n) if d else n
                s16 = state[d].astype(BF16)
                outs[d][0, j * CHUNK:(j + 1) * CHUNK, :] = _dot(rp_ref[n], s16) + y0_ref[n]
                state[d] = _dot(pm_ref[n], s16) + qm_ref[n]


def _scan_stages(r_ref, kk_ref, k_ref, b_ref, v_ref, lw_ref, y_ref, s_ref, rp_ref, y0_ref, pm_ref, qm_ref,
                 *, tb, reverse):
    C = CHUNK
    W = LANES
    nc = tb // C
    order = [(nc - 1 - jj) if reverse else jj for jj in range(nc)]
    sls = [slice(j * C, (j + 1) * C) for j in order]
    ends = [j * C if reverse else (j + 1) * C - 1 for j in order]

    lw = lw_ref[0]
    kk32, r32, b32, k32, v_all = (x[0].astype(F32) for x in (kk_ref, r_ref, b_ref, k_ref, v_ref))
    chain = {"s": s_ref[...], "n": 0}

    def chain_step():
        n = chain["n"]
        if n < nc:
            s16 = chain["s"].astype(BF16)
            y_ref[0, sls[n], :] = _dot(rp_ref[n], s16) + y0_ref[n]
            chain["s"] = _dot(pm_ref[n], s16) + qm_ref[n]
            chain["n"] = n + 1

    pos = lax.broadcasted_iota(jnp.int32, (tb, 1), 0) % C
    c = lw
    shift = 1
    while shift < C:
        if reverse:
            c = c + jnp.where(pos < C - shift, pltpu.roll(c, tb - shift, 0), 0.0)
        else:
            c = c + jnp.where(pos >= shift, pltpu.roll(c, shift, 0), 0.0)
        shift *= 2
    d = jnp.concatenate([c[e:e + 1, :] - c[j * C:(j + 1) * C] for j, e in
                         sorted(zip(order, ends))], axis=0)

    e_c = jnp.exp(c)
    e_nc = jnp.exp(-c)
    e_d = jnp.exp(d)
    at_all = -kk32 * jnp.exp(c - lw)
    rt_all = r32 * e_c
    bt_all = b32 * e_nc
    kt_all = k32 * e_nc
    bh_all = b32 * e_d
    kh_all = k32 * e_d
    yield

    r2 = lax.broadcasted_iota(jnp.int32, (W, W), 0)
    c2 = lax.broadcasted_iota(jnp.int32, (W, W), 1)
    bd_mask = (r2 // C) == (c2 // RWKV_HEAD)
    eye = r2 == c2
    tr = lax.broadcasted_iota(jnp.int32, (C, W), 0)
    sc = lax.broadcasted_iota(jnp.int32, (C, W), 1) % C
    strict = (sc > tr) if reverse else (sc < tr)
    incl = (sc >= tr) if reverse else (sc <= tr)

    def bd(x):
        return jnp.where(bd_mask, jnp.concatenate([x] * PAIR, axis=0), 0.0).astype(BF16)

    def bd2(xa, xb):
        return jnp.concatenate([bd(xa), bd(xb)], axis=1)

    At = [at_all[sl] for sl in sls]
    Rt = [rt_all[sl] for sl in sls]
    V = [v_all[sl] for sl in sls]
    bdV = [bd(x) for x in V]
    lhs = [jnp.concatenate([a, r], axis=0).astype(BF16) for a, r in zip(At, Rt)]
    sbk = [_dot_nt(l, jnp.concatenate([bd(bt_all[sl]), bd(kt_all[sl])], axis=0)) for l, sl in zip(lhs, sls)]
    sb = [x[:, :W] for x in sbk]
    sk = [x[:, W:] for x in sbk]
    lp = [jnp.where(strict, x[:C], 0.0) for x in sb]
    Lak = [jnp.where(strict, x[:C], 0.0).astype(BF16) for x in sk]
    Mrb = [jnp.where(incl, x[C:], 0.0).astype(BF16) for x in sb]
    Mrk = [jnp.where(incl, x[C:], 0.0).astype(BF16) for x in sk]
    chain_step()
    yield
    xa = list(At)
    xu = [_dot(l, b) for l, b in zip(Lak, bdV)]
    chain_step()
    yield
    n_dbl = C.bit_length() - 1
    for it in range(n_dbl):
        lp16 = [x.astype(BF16) for x in lp]
        if it + 1 < n_dbl:
            upd = [_dot(l16, jnp.concatenate([bd2(a, u), bd(l)], axis=1))
                   for l16, l, a, u in zip(lp16, lp, xa, xu)]
            lp = [x[:, 2 * W:] for x in upd]
        else:
            upd = [_dot(l, bd2(a, u)) for l, a, u in zip(lp16, xa, xu)]
        xa = [a + x[:, :W] for a, x in zip(xa, upd)]
        xu = [u + x[:, W:2 * W] for u, x in zip(xu, upd)]
        chain_step()
        yield
    rb = [_dot(m, bd2(a, u)) for m, a, u in zip(Mrb, xa, xu)]
    while chain["n"] < nc:
        chain_step()
    s_ref[...] = chain["s"]
    yield
    au = [jnp.concatenate([a, u], axis=1).astype(BF16) for a, u in zip(xa, xu)]
    pb = [_dot_tn(bh_all[sl].astype(BF16), x) for sl, x in zip(sls, au)]
    kv = [_dot_tn(kh_all[sl].astype(BF16), x.astype(BF16)) for sl, x in zip(sls, V)]
    yield
    for n in range(nc):
        rp_ref[n] = (Rt[n] + rb[n][:, :W]).astype(BF16)
        y0_ref[n] = rb[n][:, W:] + _dot(Mrk[n], bdV[n])
        pm_ref[n] = (jnp.where(bd_mask, pb[n][:, :W], 0.0)
                     + jnp.where(eye, e_c[ends[n]:ends[n] + 1, :], 0.0)).astype(BF16)
        qm_ref[n] = jnp.where(bd_mask, pb[n][:, W:] + kv[n], 0.0)


def _rwkv_scan(r, kk, k_dirs, b_dirs, v, lw_dirs, *, tb):
    B, S, Wd = r.shape
    nblk = S // tb
    nc = tb // CHUNK
    in_specs, out_specs = [], []
    for rev in (False, True):
        flip = (lambda n: nblk - 1 - n) if rev else (lambda n: n)
        in_idx = lambda b, p, i, flip=flip: (b, flip(jnp.minimum(i, nblk - 1)), p)
        out_idx = lambda b, p, i, flip=flip: (b, flip(jnp.maximum(i - 1, 0)), p)
        in_specs += [pl.BlockSpec((1, tb, LANES), in_idx)] * 6
        out_specs.append(pl.BlockSpec((1, tb, LANES), out_idx))
    scratch = [pltpu.VMEM((LANES, LANES), F32),
               pltpu.VMEM((nc, CHUNK, LANES), BF16), pltpu.VMEM((nc, CHUNK, LANES), F32),
               pltpu.VMEM((nc, LANES, LANES), BF16), pltpu.VMEM((nc, LANES, LANES), F32)]
    operands = [x for d in range(N_DIR) for x in (r, kk, k_dirs[d], b_dirs[d], v, lw_dirs[d])]
    return pl.pallas_call(
        functools.partial(_scan_kernel, tb=tb),
        out_shape=(jax.ShapeDtypeStruct((B, S, Wd), F32),) * N_DIR,
        grid=(B, Wd // LANES, nblk + 1),
        in_specs=in_specs,
        out_specs=tuple(out_specs),
        scratch_shapes=scratch * N_DIR,
        compiler_params=pltpu.CompilerParams(dimension_semantics=("parallel", "parallel", "arbitrary")),
        name="rwkv_scan",
    )(*operands)


def _outproj_kernel(xp_ref, xs_ref, attn_ref, yf_ref, yb_ref, bonus_ref, gate_ref, lw_ref, lb_ref, w_ref, g_ref,
                    o_ref, *, mp):
    y = yf_ref[...] + yb_ref[...]
    mean_mat = _head_ones(LANES)
    inv_n = 1.0 / RWKV_HEAD
    mu = _head_sum(y, mean_mat) * inv_n
    yc = y - mu
    var = _head_sum(yc * yc, mean_mat, two_pass=False) * inv_n
    yn = yc * lax.rsqrt(var + LNX_EPS) * lw_ref[...] + lb_ref[...]
    rw = ((yn + bonus_ref[...]) * gate_ref[...]).astype(BF16)
    m = _dot(attn_ref[...], w_ref[0:ATTN_WIDTH, :]) + _dot(rw, w_ref[ATTN_WIDTH:, :])
    ms = jnp.mean(m * m, axis=-1, keepdims=True)
    upd = m * lax.rsqrt(ms + NORM_EPS) * g_ref[...]

    @pl.when(pl.program_id(0) < mp)
    def _():
        o_ref[...] = xp_ref[...] + upd

    @pl.when(pl.program_id(0) >= mp)
    def _():
        o_ref[...] = xs_ref[...] + upd


def _outproj(xp2d, xs2d, attn2d, yf, yb, bonus, gate, lnx_w, lnx_b, w_out_bf16, gain, *, tm):
    mp = xp2d.shape[0] // tm
    T = xp2d.shape[0] + xs2d.shape[0]
    row = lambda w: pl.BlockSpec((tm, w), lambda m: (m, 0))
    const = lambda shape: pl.BlockSpec(shape, lambda m: (0, 0))
    return pl.pallas_call(
        functools.partial(_outproj_kernel, mp=mp),
        out_shape=jax.ShapeDtypeStruct((T, D_MODEL), F32),
        grid=(T // tm,),
        in_specs=[*_row_block_specs(tm, D_MODEL, mp),
                  row(ATTN_WIDTH), row(RWKV_WIDTH), row(RWKV_WIDTH), row(RWKV_WIDTH), row(RWKV_WIDTH),
                  const((1, RWKV_WIDTH)), const((1, RWKV_WIDTH)),
                  const((D_MODEL, D_MODEL)), const((1, D_MODEL))],
        out_specs=row(D_MODEL),
        compiler_params=pltpu.CompilerParams(dimension_semantics=("parallel",)),
        name="outproj",
    )(xp2d, xs2d, attn2d, yf, yb, bonus, gate, lnx_w, lnx_b, w_out_bf16, gain)


def _ffn_kernel(x_ref, gpre_ref, wg_ref, wu_ref, wd_ref, gpost_ref, op_ref, os_ref, h_ref, acc_ref, *, mp):
    j = pl.program_id(1)

    @pl.when(j == 0)
    def _():
        x = x_ref[...]
        ms = jnp.mean(x * x, axis=-1, keepdims=True)
        h_ref[...] = (x * lax.rsqrt(ms + NORM_EPS) * gpre_ref[...]).astype(BF16)
        acc_ref[...] = jnp.zeros_like(acc_ref)

    h = h_ref[...]
    g = _dot(h, wg_ref[...])
    u = _dot(h, wu_ref[...])
    a = (g * _sigmoid(g) * u).astype(BF16)
    acc_ref[...] += _dot(a, wd_ref[...])

    def finish(o_ref):
        f = acc_ref[...]
        ms = jnp.mean(f * f, axis=-1, keepdims=True)
        o_ref[...] = x_ref[...] + f * lax.rsqrt(ms + NORM_EPS) * gpost_ref[...]

    last = j == pl.num_programs(1) - 1

    @pl.when(last & (pl.program_id(0) < mp))
    def _():
        finish(op_ref)

    @pl.when(last & (pl.program_id(0) >= mp))
    def _():
        finish(os_ref)


def _ffn(x2d, gpre, wg, wu, wd, gpost, *, tm, tf, rows_first):
    T = x2d.shape[0]
    mp = rows_first // tm
    return pl.pallas_call(
        functools.partial(_ffn_kernel, mp=mp),
        out_shape=(jax.ShapeDtypeStruct((rows_first, D_MODEL), F32),
                   jax.ShapeDtypeStruct((T - rows_first, D_MODEL), F32)),
        grid=(T // tm, D_FF // tf),
        in_specs=[pl.BlockSpec((tm, D_MODEL), lambda m, j: (m, 0)),
                  pl.BlockSpec((1, D_MODEL), lambda m, j: (0, 0)),
                  pl.BlockSpec((D_MODEL, tf), lambda m, j: (0, j)),
                  pl.BlockSpec((D_MODEL, tf), lambda m, j: (0, j)),
                  pl.BlockSpec((tf, D_MODEL), lambda m, j: (j, 0)),
                  pl.BlockSpec((1, D_MODEL), lambda m, j: (0, 0))],
        out_specs=_row_block_specs(tm, D_MODEL, mp),
        scratch_shapes=[pltpu.VMEM((tm, D_MODEL), BF16), pltpu.VMEM((tm, D_MODEL), F32)],
        compiler_params=pltpu.CompilerParams(dimension_semantics=("arbitrary", "arbitrary")),
        name="ffn",
    )(x2d, gpre, wg, wu, wd, gpost)


def _tiles(S):
    return dict(tmi=256, tm=512, ta=512, ts=256, tb=512, tf=512)


def _lora_blockdiag(w):
    zero = jnp.zeros_like(w[0])
    return jnp.concatenate([jnp.concatenate([w[0], zero], axis=1),
                            jnp.concatenate([zero, w[1]], axis=1)], axis=0)


def _encoder_layer(xp, xs, norm_mix_pre, norm_mix_post, w_in, w_out, lambda_q1, lambda_k1, lambda_q2, lambda_k2,
                   attn_subln, tshift_prev, tshift_next, w0, w2, a0, a2, g2, k_k, k_a, r_k, lnx_w, lnx_b,
                   norm_ffn_pre, norm_ffn_post, w_gate, w_up, w_down):
    (Bp, S, D), Bs = xp.shape, xs.shape[0]
    B = Bp + Bs
    T = B * S
    t = _tiles(S)
    row = lambda a: a.reshape(1, -1).astype(F32)
    pad_cols = RWKV_COLS_PAD - RWKV_COLS

    col_scale = jnp.where(jnp.arange(w_in.shape[1]) < ATTN_WIDTH, Q_PRESCALE, 1.0).astype(F32)
    w_in_p = jnp.pad(w_in * col_scale, ((0, 0), (0, pad_cols))).astype(BF16)
    xp2d, xs2d = xp.reshape(Bp * S, D), xs.reshape(Bs * S, D)
    qkv, z = _inproj(xp2d, xs2d, row(norm_mix_pre), w_in_p, tm=t["tmi"])

    lam_params = jnp.stack([lambda_q1, lambda_k1, lambda_q2, lambda_k2]).astype(F32)
    attn = _diff_attention(qkv.reshape(B, S, QKV_COLS), lam_params, row(attn_subln), t=t["ta"])

    tp = jnp.pad(tshift_prev, (0, pad_cols)).reshape(1, -1)
    tn_ = jnp.pad(tshift_next, (0, pad_cols)).reshape(1, -1)
    g2p = jnp.pad(g2, ((0, GATE_LORA_PAD - GATE_LORA), (0, 0))).astype(BF16)
    (r, v, kk, k0, k1, b0, b1, lw0, lw1, bonus, gate) = _rwkv_prep(
        z.reshape(B, S, RWKV_COLS_PAD), tp, tn_, row(w0), _lora_blockdiag(w2), row(a0),
        _lora_blockdiag(a2).astype(BF16), g2p, row(k_k), row(k_a), row(r_k), ts=t["ts"])
    yf, yb = _rwkv_scan(r, kk, (k0, k1), (b0, b1), v, (lw0, lw1), tb=t["tb"])

    flat = lambda a: a.reshape(T, -1)
    x1 = _outproj(xp2d, xs2d, flat(attn), flat(yf), flat(yb), flat(bonus), flat(gate), row(lnx_w), row(lnx_b),
                  w_out.astype(BF16), row(norm_mix_post), tm=t["tm"])
    yp, ys = _ffn(x1, row(norm_ffn_pre), w_gate.astype(BF16), w_up.astype(BF16), w_down.astype(BF16),
                  row(norm_ffn_post), tm=t["tm"], tf=t["tf"], rows_first=Bp * S)
    return yp.reshape(Bp, S, D), ys.reshape(Bs, S, D)


def kernel(x_prompt, x_sample, norm_mix_pre, norm_mix_post, w_in, w_out, lambda_q1, lambda_k1, lambda_q2,
           lambda_k2, attn_subln, tshift_prev, tshift_next, w0, w2, a0, a2, g2, k_k, k_a, r_k, lnx_w, lnx_b,
           norm_ffn_pre, norm_ffn_post, w_gate, w_up, w_down):
    assert x_prompt.shape[1:] == x_sample.shape[1:], "both trunks share the sequence length"
    assert norm_mix_pre.shape[0] == 1, "single layer"
    return _encoder_layer(x_prompt, x_sample, norm_mix_pre[0], norm_mix_post[0], w_in[0], w_out[0], lambda_q1[0],
                          lambda_k1[0], lambda_q2[0], lambda_k2[0], attn_subln[0], tshift_prev[0], tshift_next[0],
                          w0[0], w2[0], a0[0], a2[0], g2[0], k_k[0], k_a[0], r_k[0], lnx_w[0], lnx_b[0],
                          norm_ffn_pre[0], norm_ffn_post[0], w_gate[0], w_up[0], w_down[0])
```

```python
import functools
import math

import jax
import jax.numpy as jnp
from jax import lax
from jax.experimental import pallas as pl
from jax.experimental.pallas import tpu as pltpu

F32 = jnp.float32
BF16 = jnp.bfloat16

D_MODEL = 2048
ATTN_HEADS = 8
ATTN_VDIM = 128
ATTN_QKDIM = 64
ATTN_WIDTH = ATTN_HEADS * ATTN_VDIM
RWKV_HEAD = 64
RWKV_HEADS = 16
RWKV_WIDTH = RWKV_HEAD * RWKV_HEADS
N_DIR = 2
LORA = 64
GATE_LORA = 160
GATE_LORA_PAD = 256
QKV_COLS = 3 * ATTN_WIDTH
RWKV_COLS = 3 * RWKV_WIDTH + 2 * N_DIR * LORA + GATE_LORA
RWKV_COLS_PAD = 3 * RWKV_WIDTH + 2 * N_DIR * LORA + GATE_LORA_PAD
D_FF = 5632
NORM_EPS = 1e-6
LNX_EPS = 64e-5
LAMBDA_INIT = 0.8 - 0.6 * math.exp(-0.3 * 0)

LANES = 128
CHUNK = 64
PAIR = LANES // RWKV_HEAD


def _dot(a, b):
    return jnp.dot(a, b, preferred_element_type=F32)


def _dot_nt(a, b):
    return lax.dot_general(a, b, (((1,), (1,)), ((), ())), preferred_element_type=F32)


def _dot_tn(a, b):
    return lax.dot_general(a, b, (((0,), (0,)), ((), ())), preferred_element_type=F32)


def _split(x):
    hi = x.astype(BF16)
    lo = (x - hi.astype(F32)).astype(BF16)
    return hi, lo


def _dot3(a, b):
    ah, al = _split(a)
    bh, bl = _split(b)
    return _dot(ah, bh) + _dot(al, bh) + _dot(ah, bl)


def _sigmoid(x):
    return 1.0 / (1.0 + jnp.exp(-x))


def _head_ones(width):
    r = lax.broadcasted_iota(jnp.int32, (width, width), 0) // RWKV_HEAD
    c = lax.broadcasted_iota(jnp.int32, (width, width), 1) // RWKV_HEAD
    return jnp.where(r == c, 1.0, 0.0).astype(BF16)


def _head_sum(x, ones, two_pass=True):
    outs = []
    for g in range(x.shape[1] // LANES):
        xg = x[:, g * LANES:(g + 1) * LANES]
        if two_pass:
            hi, lo = _split(xg)
            outs.append(_dot(hi, ones) + _dot(lo, ones))
        else:
            outs.append(_dot(xg.astype(BF16), ones))
    return jnp.concatenate(outs, axis=1)


def _row_block_specs(tm, width, mp):
    return (pl.BlockSpec((tm, width), lambda m, *_: (jnp.minimum(m, mp - 1), 0)),
            pl.BlockSpec((tm, width), lambda m, *_: (jnp.maximum(m - mp, 0), 0)))


def _inproj_kernel(xp_ref, xs_ref, g_ref, w_ref, qkv_ref, z_ref, h_ref, *, mp):
    m = pl.program_id(0)

    def norm(x_ref):
        x = x_ref[...]
        ms = jnp.mean(x * x, axis=-1, keepdims=True)
        h_ref[...] = (x * lax.rsqrt(ms + NORM_EPS) * g_ref[...]).astype(BF16)

    @pl.when(m < mp)
    def _():
        norm(xp_ref)

    @pl.when(m >= mp)
    def _():
        norm(xs_ref)

    h = h_ref[...]
    qkv_ref[...] = _dot(h, w_ref[:, 0:QKV_COLS]).astype(BF16)
    z_ref[...] = _dot(h, w_ref[:, QKV_COLS:])


def _inproj(xp2d, xs2d, gain, w_in_bf16, *, tm):
    mp = xp2d.shape[0] // tm
    T = xp2d.shape[0] + xs2d.shape[0]
    n_cols = w_in_bf16.shape[1]
    return pl.pallas_call(
        functools.partial(_inproj_kernel, mp=mp),
        out_shape=(jax.ShapeDtypeStruct((T, QKV_COLS), BF16),
                   jax.ShapeDtypeStruct((T, RWKV_COLS_PAD), F32)),
        grid=(T // tm,),
        in_specs=[*_row_block_specs(tm, D_MODEL, mp),
                  pl.BlockSpec((1, D_MODEL), lambda m: (0, 0)),
                  pl.BlockSpec((D_MODEL, n_cols), lambda m: (0, 0), pipeline_mode=pl.Buffered(1))],
        out_specs=(pl.BlockSpec((tm, QKV_COLS), lambda m: (m, 0)),
                   pl.BlockSpec((tm, RWKV_COLS_PAD), lambda m: (m, 0))),
        scratch_shapes=[pltpu.VMEM((tm, D_MODEL), BF16)],
        compiler_params=pltpu.CompilerParams(dimension_semantics=("parallel",)),
        name="inproj",
    )(xp2d, xs2d, gain, w_in_bf16)


LOG2E = math.log2(math.e)
Q_PRESCALE = ATTN_QKDIM ** -0.5 * LOG2E
AUX = 6
NEG_BIG = -1e30
SKIP_BITS = 80.0
NORM_SLACK = 1.01
BF16_ROWS = 16
V_ROWS = ATTN_VDIM + BF16_ROWS


def _pos_aux(slope2, n, first, sign_pos, sign_one):
    row = lax.broadcasted_iota(jnp.int32, (n, LANES), 0).astype(F32)
    lane = lax.broadcasted_iota(jnp.int32, (n, LANES), 1)
    val = slope2 * row * sign_pos
    hi = val.astype(BF16).astype(F32)
    mid = (val - hi).astype(BF16).astype(F32)
    lo = (val - hi - mid).astype(BF16).astype(F32)
    out = jnp.where(lane == first, hi, 0.0)
    out = jnp.where(lane == first + 1, mid, out)
    out = jnp.where(lane == first + 2, lo, out)
    out = jnp.where((lane >= first + 3) & (lane < first + AUX), sign_one, out)
    return out.astype(BF16)


def _attn_kernel(slopes_ref, lam_ref, sub_ref, q_ref, k_ref, v_ref, o_ref,
                 ka_ref, vt_ref, kn_ref, qa_ref, bd_ref, qv_ref, m_ref, acc_ref, s_ref, sm_ref, p_ref, al_ref,
                 *, t, seq):
    h = pl.program_id(1)
    qi = pl.program_id(2)
    nt = seq // t
    slope2 = slopes_ref[h] * LOG2E
    lane = lax.broadcasted_iota(jnp.int32, (t, LANES), 1)
    own = (lane < ATTN_QKDIM, lane >= ATTN_QKDIM)
    aux0 = (ATTN_QKDIM, 0)

    @pl.when(qi == 0)
    def _():
        ak = [_pos_aux(slope2, t, aux0[c], 1.0, 1.0) for c in range(2)]
        ones_rows = jnp.where(lax.broadcasted_iota(jnp.int32, (BF16_ROWS, t), 0) == 0, 1.0, 0.0).astype(BF16)

        def build(j, kn):
            off = pl.multiple_of(j * t, t)
            kt = k_ref[0, pl.ds(off, t), :]
            for c in range(2):
                ka_ref[c, j] = jnp.where(own[c], kt, ak[c])
            vt_ref[j, 0:ATTN_VDIM, :] = v_ref[0, pl.ds(off, t), :].astype(F32).T.astype(BF16)
            vt_ref[j, ATTN_VDIM:, :] = ones_rows
            k32 = kt.astype(F32)
            ksq = k32 * k32
            rows = jnp.maximum(jnp.sum(jnp.where(own[0], ksq, 0.0), axis=1, keepdims=True),
                               jnp.sum(jnp.where(own[1], ksq, 0.0), axis=1, keepdims=True))
            return jnp.maximum(kn, jnp.max(rows, axis=0, keepdims=True))

        kn_ref[...] = lax.fori_loop(0, nt, build, jnp.zeros((1, 1), F32))

        for c in range(2):
            a = aux0[c]
            lane_pos = (lane >= a + 3) & (lane < a + AUX)
            lane_one = (lane >= a) & (lane < a + 3)
            pos = _pos_aux(slope2, t, a + 3, 1.0, 0.0).astype(F32)
            qa_ref[c, 0] = jnp.where(lane_one, 1.0, jnp.where(lane_pos, -pos, 0.0)).astype(BF16)
            qa_ref[c, 1] = jnp.where(lane_one, -1.0, jnp.where(lane_pos, pos, 0.0)).astype(BF16)
        rel = (lax.broadcasted_iota(jnp.int32, (t, t), 0)
               - lax.broadcasted_iota(jnp.int32, (t, t), 1)).astype(F32)
        bd_ref[...] = -slope2 * jnp.abs(rel)

    q = q_ref[0]
    zero = jnp.zeros((t, LANES), BF16)
    for c in range(2):
        for side in range(2):
            qv_ref[c, side] = jnp.where(own[c], q, qa_ref[c, side])
    q_diag = [jnp.where(own[c], q, zero) for c in range(2)]

    acc_ref[...] = jnp.zeros(acc_ref.shape, F32)

    q32 = q.astype(F32)
    qsq = q32 * q32
    qn = jnp.maximum(jnp.sum(jnp.where(own[0], qsq, 0.0), axis=1, keepdims=True),
                     jnp.sum(jnp.where(own[1], qsq, 0.0), axis=1, keepdims=True))
    qk_bound = 2.0 * NORM_SLACK * jnp.sqrt(jnp.max(qn, axis=0, keepdims=True) * kn_ref[...])
    dist_needed = (qk_bound + SKIP_BITS) / slope2
    reach = jnp.clip(jnp.ceil((dist_needed - 1.0) / t), 0.0, nt - 1.0)[0, 0].astype(jnp.int32)
    n_left = jnp.minimum(qi, reach)
    n_off = n_left + jnp.minimum(nt - 1 - qi, reach)

    def key_tile(n):
        return jnp.clip(jnp.where(n < n_left, qi - 1 - n, qi + 1 + n - n_left), 0, nt - 1)

    def probs(c, s, smax, cst):
        m_old = m_ref[c]
        m_new = jnp.maximum(m_old, smax + cst)
        m_ref[c] = m_new
        return jnp.exp2(m_old - m_new), jnp.exp2(s - (m_new - cst)).astype(BF16)

    def accumulate(c, alpha, vt, p):
        acc_ref[c] = alpha * acc_ref[c] + _dot(vt, p)

    def stage_scores(n, slot):
        j = key_tile(n)
        for c in range(2):
            s = _dot_nt(ka_ref[c, j], qv_ref[c, (j > qi).astype(jnp.int32)])
            s_ref[slot, c] = s
            sm_ref[slot, c] = jnp.max(s, axis=0, keepdims=True)

    def stage_softmax(n, slot):
        j = key_tile(n)
        cst = jnp.where(n < n_off, -slope2 * (jnp.abs(qi - j) * t).astype(F32), NEG_BIG)
        for c in range(2):
            alpha, p = probs(c, s_ref[slot, c], sm_ref[slot, c], cst)
            p_ref[slot, c] = p
            al_ref[slot, c] = alpha

    def stage_values(n, slot):
        j = key_tile(n)
        for c in range(2):
            accumulate(c, al_ref[slot, c], vt_ref[j], p_ref[slot, c])

    m_ref[...] = jnp.full(m_ref.shape, NEG_BIG, F32)
    s_diag = [_dot_nt(ka_ref[c, qi], q_diag[c]) + bd_ref[...] for c in range(2)]
    stage_scores(0, 0)
    stage_scores(1, 1)
    for c in range(2):
        alpha, p = probs(c, s_diag[c], jnp.max(s_diag[c], axis=0, keepdims=True), 0.0)
        accumulate(c, alpha, vt_ref[qi], p)
    stage_softmax(0, 0)

    def pair_body(mi, carry):
        n0 = 2 + 2 * mi
        stage_scores(n0, 0)
        stage_values(n0 - 2, 0)
        stage_softmax(n0 - 1, 1)
        stage_scores(n0 + 1, 1)
        stage_values(n0 - 1, 1)
        stage_softmax(n0, 0)
        return carry

    n_pairs = jnp.maximum(n_off - 2, 0) // 2
    lax.fori_loop(0, n_pairs, pair_body, 0)
    n_done = 2 * n_pairs

    @pl.when(n_off % 2 == 0)
    def _():
        stage_values(n_done, 0)
        stage_softmax(n_done + 1, 1)
        stage_values(n_done + 1, 1)

    @pl.when(n_off % 2 == 1)
    def _():
        stage_scores(n_done + 2, 0)
        stage_values(n_done, 0)
        stage_softmax(n_done + 1, 1)
        stage_values(n_done + 1, 1)
        stage_softmax(n_done + 2, 0)
        stage_values(n_done + 2, 0)

    lp = lam_ref[...]
    lam = (jnp.exp(jnp.sum(lp[0:1] * lp[1:2], axis=-1, keepdims=True))
           - jnp.exp(jnp.sum(lp[2:3] * lp[3:4], axis=-1, keepdims=True)) + LAMBDA_INIT)
    num = [acc_ref[c, 0:ATTN_VDIM, :] for c in range(2)]
    den = [acc_ref[c, ATTN_VDIM:ATTN_VDIM + 1, :] for c in range(2)]
    o_t = num[0] / den[0] - lam * (num[1] / den[1])
    o = o_t.T
    ms = jnp.mean(o * o, axis=-1, keepdims=True)
    o = o * lax.rsqrt(ms + NORM_EPS) * sub_ref[...] * (1.0 - LAMBDA_INIT)
    o_ref[0] = o.astype(o_ref.dtype)


def _diff_attention(qkv, lam_params, subln, *, t):
    B, S, _ = qkv.shape
    nt = S // t
    slopes = jnp.asarray([2.0 ** (-8.0 * (h + 1.0) / ATTN_HEADS) for h in range(ATTN_HEADS)], F32)
    grid_spec = pltpu.PrefetchScalarGridSpec(
        num_scalar_prefetch=1,
        grid=(B, ATTN_HEADS, nt),
        in_specs=[pl.BlockSpec((4, ATTN_QKDIM), lambda b, h, i, s: (0, 0)),
                  pl.BlockSpec((1, ATTN_VDIM), lambda b, h, i, s: (0, 0)),
                  pl.BlockSpec((1, t, LANES), lambda b, h, i, s: (b, i, h)),
                  pl.BlockSpec((1, S, LANES), lambda b, h, i, s: (b, 0, ATTN_HEADS + h)),
                  pl.BlockSpec((1, S, LANES), lambda b, h, i, s: (b, 0, 2 * ATTN_HEADS + h))],
        out_specs=pl.BlockSpec((1, t, LANES), lambda b, h, i, s: (b, i, h)),
        scratch_shapes=[pltpu.VMEM((2, nt, t, LANES), BF16),
                        pltpu.VMEM((nt, V_ROWS, t), BF16),
                        pltpu.VMEM((1, 1), F32),
                        pltpu.VMEM((2, 2, t, LANES), BF16),
                        pltpu.VMEM((t, t), F32),
                        pltpu.VMEM((2, 2, t, LANES), BF16),
                        pltpu.VMEM((2, 1, t), F32),
                        pltpu.VMEM((2, V_ROWS, t), F32),
                        pltpu.VMEM((2, 2, t, t), F32),
                        pltpu.VMEM((2, 2, 1, t), F32),
                        pltpu.VMEM((2, 2, t, t), BF16),
                        pltpu.VMEM((2, 2, 1, t), F32)],
    )
    return pl.pallas_call(
        functools.partial(_attn_kernel, t=t, seq=S),
        out_shape=jax.ShapeDtypeStruct((B, S, ATTN_WIDTH), BF16),
        grid_spec=grid_spec,
        compiler_params=pltpu.CompilerParams(dimension_semantics=("parallel", "parallel", "arbitrary")),
        name="diff_attn",
    )(slopes, lam_params, subln, qkv, qkv, qkv)


def _prep_kernel(z_ref, zp_ref, zn_ref, tp_ref, tn_ref, w0_ref, w2_ref, a0_ref, a2_ref, g2_ref,
                 kk_ref, ka_ref, rk_ref,
                 r_o, v_o, kk_o, k0_o, k1_o, b0_o, b1_o, lw0_o, lw1_o, bonus_o, gate_o):
    i = pl.program_id(1)
    nt = pl.num_programs(1)
    z = z_ref[0]
    ts = z.shape[0]
    row = lax.broadcasted_iota(jnp.int32, (ts, 1), 0)
    prev_row = jnp.where(i > 0, zp_ref[0, 7:8, :], 0.0)
    next_row = jnp.where(i < nt - 1, zn_ref[0, 0:1, :], 0.0)
    z_prev = jnp.where(row == 0, prev_row, pltpu.roll(z, 1, 0))
    z_next = jnp.where(row == ts - 1, next_row, pltpu.roll(z, ts - 1, 0))
    zs = z + tp_ref[...] * (z_prev - z) + tn_ref[...] * (z_next - z)

    W = RWKV_WIDTH
    r = zs[:, 0:W]
    k = zs[:, W:2 * W]
    v = zs[:, 2 * W:3 * W]
    wd = jnp.tanh(zs[:, 3 * W:3 * W + LANES])
    ad = zs[:, 3 * W + LANES:3 * W + 2 * LANES]
    gd = zs[:, 3 * W + 2 * LANES:]

    w_log = w0_ref[...] + _dot3(wd, w2_ref[...])
    lw = -_sigmoid(w_log) * math.exp(-0.5)
    iclr = _sigmoid(a0_ref[...] + _dot(ad.astype(BF16), a2_ref[...].astype(BF16)))
    gate = _dot(_sigmoid(gd).astype(BF16), g2_ref[...].astype(BF16))

    ones = _head_ones(LANES)
    kk = k * kk_ref[...]
    kk = kk * lax.rsqrt(jnp.maximum(_head_sum(kk * kk, ones), 1e-24))
    ka = ka_ref[...]
    rk = rk_ref[...]
    bonus = jnp.zeros_like(r)
    k_outs = (k0_o, k1_o)
    b_outs = (b0_o, b1_o)
    lw_outs = (lw0_o, lw1_o)
    for d in range(N_DIR):
        a_d = iclr[:, d * W:(d + 1) * W]
        k_d = k * (1.0 + (a_d - 1.0) * ka)
        k_outs[d][0] = k_d.astype(k_outs[d].dtype)
        b_outs[d][0] = (kk * a_d).astype(b_outs[d].dtype)
        lw_outs[d][0] = lw[:, d * W:(d + 1) * W]
        bonus = bonus + _head_sum(r * k_d * rk, ones, two_pass=False) * v
    r_o[0] = r.astype(r_o.dtype)
    v_o[0] = v.astype(v_o.dtype)
    kk_o[0] = kk.astype(kk_o.dtype)
    bonus_o[0] = bonus
    gate_o[0] = gate.astype(gate_o.dtype)


def _rwkv_prep(z, tp, tn, w0, w2bd, a0, a2bd, g2p, k_k, k_a, r_k, *, ts):
    B, S, ZC = z.shape
    nt = S // ts
    hb = ts // 8
    last8 = S // 8 - 1
    const = lambda shape: pl.BlockSpec(shape, lambda b, i: (0, 0))
    out_dtypes = (BF16,) * 7 + (F32, F32, F32, BF16)
    out_spec = pl.BlockSpec((1, ts, RWKV_WIDTH), lambda b, i: (b, i, 0))
    return pl.pallas_call(
        _prep_kernel,
        out_shape=tuple(jax.ShapeDtypeStruct((B, S, RWKV_WIDTH), dt) for dt in out_dtypes),
        grid=(B, nt),
        in_specs=[pl.BlockSpec((1, ts, ZC), lambda b, i: (b, i, 0)),
                  pl.BlockSpec((1, 8, ZC), lambda b, i: (b, jnp.maximum(i * hb - 1, 0), 0)),
                  pl.BlockSpec((1, 8, ZC), lambda b, i: (b, jnp.minimum((i + 1) * hb, last8), 0)),
                  const((1, ZC)), const((1, ZC)),
                  const((1, N_DIR * RWKV_WIDTH)), const((LANES, N_DIR * RWKV_WIDTH)),
                  const((1, N_DIR * RWKV_WIDTH)), const((LANES, N_DIR * RWKV_WIDTH)),
                  const((GATE_LORA_PAD, RWKV_WIDTH)),
                  const((1, RWKV_WIDTH)), const((1, RWKV_WIDTH)), const((1, RWKV_WIDTH))],
        out_specs=(out_spec,) * 11,
        compiler_params=pltpu.CompilerParams(dimension_semantics=("parallel", "arbitrary")),
        name="rwkv_prep",
    )(z, z, z, tp, tn, w0, w2bd, a0, a2bd, g2p, k_k, k_a, r_k)


def _scan_kernel(*refs, tb):
    n_in, n_scr = 6, 5
    ins = (refs[0:n_in], refs[n_in:2 * n_in])
    outs = refs[2 * n_in:2 * n_in + 2]
    scr = (refs[2 * n_in + 2:2 * n_in + 2 + n_scr], refs[2 * n_in + 2 + n_scr:])

    @pl.when(pl.program_id(2) == 0)
    def _():
        for ref in scr[0] + scr[1]:
            ref[...] = jnp.zeros_like(ref)

    last = pl.num_programs(2) - 1

    @pl.when(pl.program_id(2) < last)
    def _():
        live = [_scan_stages(*ins[d], outs[d], *scr[d], tb=tb, reverse=bool(d)) for d in range(N_DIR)]
        while live:
            for g in list(live):
                if next(g, "done") == "done":
                    live.remove(g)

    @pl.when(pl.program_id(2) == last)
    def _():
        nc = tb // CHUNK
        state = [scr[d][0][...] for d in range(N_DIR)]
        for n in range(nc):
            for d in range(N_DIR):
                s_ref, rp_ref, y0_ref, pm_ref, qm_ref = scr[d]
                j = (nc - 1 - n) if d else n
                s16 = state[d].astype(BF16)
                outs[d][0, j * CHUNK:(j + 1) * CHUNK, :] = _dot(rp_ref[n], s16) + y0_ref[n]
                state[d] = _dot(pm_ref[n], s16) + qm_ref[n]


def _scan_stages(r_ref, kk_ref, k_ref, b_ref, v_ref, lw_ref, y_ref, s_ref, rp_ref, y0_ref, pm_ref, qm_ref,
                 *, tb, reverse):
    C = CHUNK
    W = LANES
    nc = tb // C
    order = [(nc - 1 - jj) if reverse else jj for jj in range(nc)]
    sls = [slice(j * C, (j + 1) * C) for j in order]
    ends = [j * C if reverse else (j + 1) * C - 1 for j in order]

    lw = lw_ref[0]
    kk32, r32, b32, k32, v_all = (x[0].astype(F32) for x in (kk_ref, r_ref, b_ref, k_ref, v_ref))
    chain = {"s": s_ref[...], "n": 0}

    def chain_step():
        n = chain["n"]
        if n < nc:
            s16 = chain["s"].astype(BF16)
            y_ref[0, sls[n], :] = _dot(rp_ref[n], s16) + y0_ref[n]
            chain["s"] = _dot(pm_ref[n], s16) + qm_ref[n]
            chain["n"] = n + 1

    pos = lax.broadcasted_iota(jnp.int32, (tb, 1), 0) % C
    c = lw
    shift = 1
    while shift < C:
        if reverse:
            c = c + jnp.where(pos < C - shift, pltpu.roll(c, tb - shift, 0), 0.0)
        else:
            c = c + jnp.where(pos >= shift, pltpu.roll(c, shift, 0), 0.0)
        shift *= 2
    d = jnp.concatenate([c[e:e + 1, :] - c[j * C:(j + 1) * C] for j, e in
                         sorted(zip(order, ends))], axis=0)

    e_c = jnp.exp(c)
    e_nc = jnp.exp(-c)
    e_d = jnp.exp(d)
    at_all = -kk32 * jnp.exp(c - lw)
    rt_all = r32 * e_c
    bt_all = b32 * e_nc
    kt_all = k32 * e_nc
    bh_all = b32 * e_d
    kh_all = k32 * e_d
    yield

    r2 = lax.broadcasted_iota(jnp.int32, (W, W), 0)
    c2 = lax.broadcasted_iota(jnp.int32, (W, W), 1)
    bd_mask = (r2 // C) == (c2 // RWKV_HEAD)
    eye = r2 == c2
    tr = lax.broadcasted_iota(jnp.int32, (C, W), 0)
    sc = lax.broadcasted_iota(jnp.int32, (C, W), 1) % C
    strict = (sc > tr) if reverse else (sc < tr)
    incl = (sc >= tr) if reverse else (sc <= tr)

    def bd(x):
        return jnp.where(bd_mask, jnp.concatenate([x] * PAIR, axis=0), 0.0).astype(BF16)

    def bd2(xa, xb):
        return jnp.concatenate([bd(xa), bd(xb)], axis=1)

    At = [at_all[sl] for sl in sls]
    Rt = [rt_all[sl] for sl in sls]
    V = [v_all[sl] for sl in sls]
    bdV = [bd(x) for x in V]
    lhs = [jnp.concatenate([a, r], axis=0).astype(BF16) for a, r in zip(At, Rt)]
    sbk = [_dot_nt(l, jnp.concatenate([bd(bt_all[sl]), bd(kt_all[sl])], axis=0)) for l, sl in zip(lhs, sls)]
    sb = [x[:, :W] for x in sbk]
    sk = [x[:, W:] for x in sbk]
    lp = [jnp.where(strict, x[:C], 0.0) for x in sb]
    Lak = [jnp.where(strict, x[:C], 0.0).astype(BF16) for x in sk]
    Mrb = [jnp.where(incl, x[C:], 0.0).astype(BF16) for x in sb]
    Mrk = [jnp.where(incl, x[C:], 0.0).astype(BF16) for x in sk]
    chain_step()
    yield
    xa = list(At)
    xu = [_dot(l, b) for l, b in zip(Lak, bdV)]
    chain_step()
    yield
    n_dbl = C.bit_length() - 1
    for it in range(n_dbl):
        lp16 = [x.astype(BF16) for x in lp]
        if it + 1 < n_dbl:
            upd = [_dot(l16, jnp.concatenate([bd2(a, u), bd(l)], axis=1))
                   for l16, l, a, u in zip(lp16, lp, xa, xu)]
            lp = [x[:, 2 * W:] for x in upd]
        else:
            upd = [_dot(l, bd2(a, u)) for l, a, u in zip(lp16, xa, xu)]
        xa = [a + x[:, :W] for a, x in zip(xa, upd)]
        xu = [u + x[:, W:2 * W] for u, x in zip(xu, upd)]
        chain_step()
        yield
    rb = [_dot(m, bd2(a, u)) for m, a, u in zip(Mrb, xa, xu)]
    while chain["n"] < nc:
        chain_step()
    s_ref[...] = chain["s"]
    yield
    au = [jnp.concatenate([a, u], axis=1).astype(BF16) for a, u in zip(xa, xu)]
    pb = [_dot_tn(bh_all[sl].astype(BF16), x) for sl, x in zip(sls, au)]
    kv = [_dot_tn(kh_all[sl].astype(BF16), x.astype(BF16)) for sl, x in zip(sls, V)]
    yield
    for n in range(nc):
        rp_ref[n] = (Rt[n] + rb[n][:, :W]).astype(BF16)
        y0_ref[n] = rb[n][:, W:] + _dot(Mrk[n], bdV[n])
        pm_ref[n] = (jnp.where(bd_mask, pb[n][:, :W], 0.0)
                     + jnp.where(eye, e_c[ends[n]:ends[n] + 1, :], 0.0)).astype(BF16)
        qm_ref[n] = jnp.where(bd_mask, pb[n][:, W:] + kv[n], 0.0)


def _rwkv_scan(r, kk, k_dirs, b_dirs, v, lw_dirs, *, tb):
    B, S, Wd = r.shape
    nblk = S // tb
    nc = tb // CHUNK
    in_specs, out_specs = [], []
    for rev in (False, True):
        flip = (lambda n: nblk - 1 - n) if rev else (lambda n: n)
        in_idx = lambda b, p, i, flip=flip: (b, flip(jnp.minimum(i, nblk - 1)), p)
        out_idx = lambda b, p, i, flip=flip: (b, flip(jnp.maximum(i - 1, 0)), p)
        in_specs += [pl.BlockSpec((1, tb, LANES), in_idx)] * 6
        out_specs.append(pl.BlockSpec((1, tb, LANES), out_idx))
    scratch = [pltpu.VMEM((LANES, LANES), F32),
               pltpu.VMEM((nc, CHUNK, LANES), BF16), pltpu.VMEM((nc, CHUNK, LANES), F32),
               pltpu.VMEM((nc, LANES, LANES), BF16), pltpu.VMEM((nc, LANES, LANES), F32)]
    operands = [x for d in range(N_DIR) for x in (r, kk, k_dirs[d], b_dirs[d], v, lw_dirs[d])]
    return pl.pallas_call(
        functools.partial(_scan_kernel, tb=tb),
        out_shape=(jax.ShapeDtypeStruct((B, S, Wd), F32),) * N_DIR,
        grid=(B, Wd // LANES, nblk + 1),
        in_specs=in_specs,
        out_specs=tuple(out_specs),
        scratch_shapes=scratch * N_DIR,
        compiler_params=pltpu.CompilerParams(dimension_semantics=("parallel", "parallel", "arbitrary")),
        name="rwkv_scan",
    )(*operands)


def _outproj_kernel(xp_ref, xs_ref, attn_ref, yf_ref, yb_ref, bonus_ref, gate_ref, lw_ref, lb_ref, w_ref, g_ref,
                    o_ref, *, mp):
    y = yf_ref[...] + yb_ref[...]
    mean_mat = _head_ones(LANES)
    inv_n = 1.0 / RWKV_HEAD
    mu = _head_sum(y, mean_mat) * inv_n
    yc = y - mu
    var = _head_sum(yc * yc, mean_mat, two_pass=False) * inv_n
    yn = yc * lax.rsqrt(var + LNX_EPS) * lw_ref[...] + lb_ref[...]
    rw = ((yn + bonus_ref[...]) * gate_ref[...]).astype(BF16)
    m = _dot(attn_ref[...], w_ref[0:ATTN_WIDTH, :]) + _dot(rw, w_ref[ATTN_WIDTH:, :])
    ms = jnp.mean(m * m, axis=-1, keepdims=True)
    upd = m * lax.rsqrt(ms + NORM_EPS) * g_ref[...]

    @pl.when(pl.program_id(0) < mp)
    def _():
        o_ref[...] = xp_ref[...] + upd

    @pl.when(pl.program_id(0) >= mp)
    def _():
        o_ref[...] = xs_ref[...] + upd


def _outproj(xp2d, xs2d, attn2d, yf, yb, bonus, gate, lnx_w, lnx_b, w_out_bf16, gain, *, tm):
    mp = xp2d.shape[0] // tm
    T = xp2d.shape[0] + xs2d.shape[0]
    row = lambda w: pl.BlockSpec((tm, w), lambda m: (m, 0))
    const = lambda shape: pl.BlockSpec(shape, lambda m: (0, 0))
    return pl.pallas_call(
        functools.partial(_outproj_kernel, mp=mp),
        out_shape=jax.ShapeDtypeStruct((T, D_MODEL), F32),
        grid=(T // tm,),
        in_specs=[*_row_block_specs(tm, D_MODEL, mp),
                  row(ATTN_WIDTH), row(RWKV_WIDTH), row(RWKV_WIDTH), row(RWKV_WIDTH), row(RWKV_WIDTH),
                  const((1, RWKV_WIDTH)), const((1, RWKV_WIDTH)),
                  const((D_MODEL, D_MODEL)), const((1, D_MODEL))],
        out_specs=row(D_MODEL),
        compiler_params=pltpu.CompilerParams(dimension_semantics=("parallel",)),
        name="outproj",
    )(xp2d, xs2d, attn2d, yf, yb, bonus, gate, lnx_w, lnx_b, w_out_bf16, gain)


def _ffn_kernel(x_ref, gpre_ref, wg_ref, wu_ref, wd_ref, gpost_ref, op_ref, os_ref, h_ref, acc_ref, *, mp):
    j = pl.program_id(1)

    @pl.when(j == 0)
    def _():
        x = x_ref[...]
        ms = jnp.mean(x * x, axis=-1, keepdims=True)
        h_ref[...] = (x * lax.rsqrt(ms + NORM_EPS) * gpre_ref[...]).astype(BF16)
        acc_ref[...] = jnp.zeros_like(acc_ref)

    h = h_ref[...]
    g = _dot(h, wg_ref[...])
    u = _dot(h, wu_ref[...])
    a = (g * _sigmoid(g) * u).astype(BF16)
    acc_ref[...] += _dot(a, wd_ref[...])

    def finish(o_ref):
        f = acc_ref[...]
        ms = jnp.mean(f * f, axis=-1, keepdims=True)
        o_ref[...] = x_ref[...] + f * lax.rsqrt(ms + NORM_EPS) * gpost_ref[...]

    last = j == pl.num_programs(1) - 1

    @pl.when(last & (pl.program_id(0) < mp))
    def _():
        finish(op_ref)

    @pl.when(last & (pl.program_id(0) >= mp))
    def _():
        finish(os_ref)


def _ffn(x2d, gpre, wg, wu, wd, gpost, *, tm, tf, rows_first):
    T = x2d.shape[0]
    mp = rows_first // tm
    return pl.pallas_call(
        functools.partial(_ffn_kernel, mp=mp),
        out_shape=(jax.ShapeDtypeStruct((rows_first, D_MODEL), F32),
                   jax.ShapeDtypeStruct((T - rows_first, D_MODEL), F32)),
        grid=(T // tm, D_FF // tf),
        in_specs=[pl.BlockSpec((tm, D_MODEL), lambda m, j: (m, 0)),
                  pl.BlockSpec((1, D_MODEL), lambda m, j: (0, 0)),
                  pl.BlockSpec((D_MODEL, tf), lambda m, j: (0, j)),
                  pl.BlockSpec((D_MODEL, tf), lambda m, j: (0, j)),
                  pl.BlockSpec((tf, D_MODEL), lambda m, j: (j, 0)),
                  pl.BlockSpec((1, D_MODEL), lambda m, j: (0, 0))],
        out_specs=_row_block_specs(tm, D_MODEL, mp),
        scratch_shapes=[pltpu.VMEM((tm, D_MODEL), BF16), pltpu.VMEM((tm, D_MODEL), F32)],
        compiler_params=pltpu.CompilerParams(dimension_semantics=("arbitrary", "arbitrary")),
        name="ffn",
    )(x2d, gpre, wg, wu, wd, gpost)


def _tiles(S):
    return dict(tmi=256, tm=512, ta=512, ts=256, tb=512, tf=512)


def _lora_blockdiag(w):
    zero = jnp.zeros_like(w[0])
    return jnp.concatenate([jnp.concatenate([w[0], zero], axis=1),
                            jnp.concatenate([zero, w[1]], axis=1)], axis=0)


def _encoder_layer(xp, xs, norm_mix_pre, norm_mix_post, w_in, w_out, lambda_q1, lambda_k1, lambda_q2, lambda_k2,
                   attn_subln, tshift_prev, tshift_next, w0, w2, a0, a2, g2, k_k, k_a, r_k, lnx_w, lnx_b,
                   norm_ffn_pre, norm_ffn_post, w_gate, w_up, w_down):
    (Bp, S, D), Bs = xp.shape, xs.shape[0]
    B = Bp + Bs
    T = B * S
    t = _tiles(S)
    row = lambda a: a.reshape(1, -1).astype(F32)
    pad_cols = RWKV_COLS_PAD - RWKV_COLS

    col_scale = jnp.where(jnp.arange(w_in.shape[1]) < ATTN_WIDTH, Q_PRESCALE, 1.0).astype(F32)
    w_in_p = jnp.pad(w_in * col_scale, ((0, 0), (0, pad_cols))).astype(BF16)
    xp2d, xs2d = xp.reshape(Bp * S, D), xs.reshape(Bs * S, D)
    qkv, z = _inproj(xp2d, xs2d, row(norm_mix_pre), w_in_p, tm=t["tmi"])

    lam_params = jnp.stack([lambda_q1, lambda_k1, lambda_q2, lambda_k2]).astype(F32)
    attn = _diff_attention(qkv.reshape(B, S, QKV_COLS), lam_params, row(attn_subln), t=t["ta"])

    tp = jnp.pad(tshift_prev, (0, pad_cols)).reshape(1, -1)
    tn_ = jnp.pad(tshift_next, (0, pad_cols)).reshape(1, -1)
    g2p = jnp.pad(g2, ((0, GATE_LORA_PAD - GATE_LORA), (0, 0))).astype(BF16)
    (r, v, kk, k0, k1, b0, b1, lw0, lw1, bonus, gate) = _rwkv_prep(
        z.reshape(B, S, RWKV_COLS_PAD), tp, tn_, row(w0), _lora_blockdiag(w2), row(a0),
        _lora_blockdiag(a2).astype(BF16), g2p, row(k_k), row(k_a), row(r_k), ts=t["ts"])
    yf, yb = _rwkv_scan(r, kk, (k0, k1), (b0, b1), v, (lw0, lw1), tb=t["tb"])

    flat = lambda a: a.reshape(T, -1)
    x1 = _outproj(xp2d, xs2d, flat(attn), flat(yf), flat(yb), flat(bonus), flat(gate), row(lnx_w), row(lnx_b),
                  w_out.astype(BF16), row(norm_mix_post), tm=t["tm"])
    yp, ys = _ffn(x1, row(norm_ffn_pre), w_gate.astype(BF16), w_up.astype(BF16), w_down.astype(BF16),
                  row(norm_ffn_post), tm=t["tm"], tf=t["tf"], rows_first=Bp * S)
    return yp.reshape(Bp, S, D), ys.reshape(Bs, S, D)


def kernel(x_prompt, x_sample, norm_mix_pre, norm_mix_post, w_in, w_out, lambda_q1, lambda_k1, lambda_q2,
           lambda_k2, attn_subln, tshift_prev, tshift_next, w0, w2, a0, a2, g2, k_k, k_a, r_k, lnx_w, lnx_b,
           norm_ffn_pre, norm_ffn_post, w_gate, w_up, w_down):
    assert x_prompt.shape[1:] == x_sample.shape[1:], "both trunks share the sequence length"
    assert norm_mix_pre.shape[0] == 1, "single layer"
    return _encoder_layer(x_prompt, x_sample, norm_mix_pre[0], norm_mix_post[0], w_in[0], w_out[0], lambda_q1[0],
                          lambda_k1[0], lambda_q2[0], lambda_k2[0], attn_subln[0], tshift_prev[0], tshift_next[0],
                          w0[0], w2[0], a0[0], a2[0], g2[0], k_k[0], k_a[0], r_k[0], lnx_w[0], lnx_b[0],
                          norm_ffn_pre[0], norm_ffn_post[0], w_gate[0], w_up[0], w_down[0])
```
